```python
import jax, jax.numpy as jnp
from jax import lax
import numpy as np

D_MODEL = 2048
BATCH = 8
SEQ = 8192
DEPTH = 1

MEM_LEN = 256
HEAD_DIM = 128
N_CONV_GROUPS = 4
CONV_W = N_CONV_GROUPS * HEAD_DIM
CONV_K = 31
N_FOX_HEADS = 8
FOX_W = N_FOX_HEADS * HEAD_DIM
N_MEM_HEADS = 4
MEM_W = N_MEM_HEADS * HEAD_DIM
MIX_W = CONV_W + FOX_W + MEM_W
Q_BLOCK = 128
EPS = 1e-6
FORGET_BIAS_INIT = 2.0

SPLITS = (
    CONV_W,
    2 * CONV_W,
    3 * CONV_W,
    3 * CONV_W + FOX_W,
    3 * CONV_W + 2 * FOX_W,
    3 * CONV_W + 3 * FOX_W,
    3 * CONV_W + 3 * FOX_W + N_FOX_HEADS,
    3 * CONV_W + 4 * FOX_W + N_FOX_HEADS,
    3 * CONV_W + 4 * FOX_W + N_FOX_HEADS + MEM_W,
)
D_IN = 3 * CONV_W + 4 * FOX_W + N_FOX_HEADS + 2 * MEM_W

kernel_name = "hybrid_conformer_fox_memory_layer"


def rmsnorm(x, g):
    xf = x.astype(jnp.float32)
    y = xf * lax.rsqrt(jnp.mean(xf * xf, axis=-1, keepdims=True) + EPS)
    return (y * g.astype(jnp.float32)).astype(x.dtype)


def layernorm(x, g, b):
    xf = x.astype(jnp.float32)
    mu = jnp.mean(xf, axis=-1, keepdims=True)
    var = jnp.mean(jnp.square(xf - mu), axis=-1, keepdims=True)
    y = (xf - mu) * lax.rsqrt(var + EPS)
    return (y * g.astype(jnp.float32) + b.astype(jnp.float32)).astype(x.dtype)


def causal_depthwise_conv(u, w, b):
    y = lax.conv_general_dilated(
        u, w[:, None, :], window_strides=(1,), padding=[(CONV_K - 1, 0)],
        dimension_numbers=('NWC', 'WIO', 'NWC'), feature_group_count=u.shape[-1])
    return y + b


def fox_attention(q, k, v, logf):
    B, S, H, Dh = q.shape
    nb = S // Q_BLOCK
    cT = jnp.cumsum(logf, axis=1).transpose(0, 2, 1)
    qb = q.reshape(B, nb, Q_BLOCK, H, Dh).transpose(1, 0, 2, 3, 4)
    cb = cT.reshape(B, H, nb, Q_BLOCK).transpose(2, 0, 1, 3)
    k_pos = jnp.arange(S)
    scale = Dh ** -0.5

    def block(args):
        i, qi, ci = args
        s = jnp.einsum('bqhd,bkhd->bhqk', qi, k).astype(jnp.float32) * scale
        bias = ci[..., :, None] - cT[..., None, :]
        q_pos = i * Q_BLOCK + jnp.arange(Q_BLOCK)
        mask = k_pos[None, :] <= q_pos[:, None]
        s = jnp.where(mask, s + bias, -jnp.inf)
        p = jax.nn.softmax(s, axis=-1).astype(v.dtype)
        return jnp.einsum('bhqk,bkhd->bqhd', p, v)

    out = lax.map(block, (jnp.arange(nb), qb, cb))
    return out.transpose(1, 0, 2, 3, 4).reshape(B, S, H * Dh)


def memory_attention(q, mk, mv):
    B, S, H, Dh = q.shape
    s = jnp.einsum('bshd,bmhd->bhsm', q, mk).astype(jnp.float32) * (Dh ** -0.5)
    p = jax.nn.softmax(s, axis=-1).astype(mv.dtype)
    return jnp.einsum('bhsm,bmhd->bshd', p, mv).reshape(B, S, H * Dh)


def _fwd_setup_inputs(seed: int = 0) -> dict:
    key = jax.random.key(seed)
    ks = jax.random.split(key, 16)
    n = jax.random.normal
    f32 = jnp.float32
    return {
        "x": n(ks[0], (BATCH, SEQ, D_MODEL), f32),
        "mem": n(ks[1], (BATCH, MEM_LEN, D_MODEL), f32),
        "norm_g": 1.0 + 0.05 * n(ks[2], (DEPTH, D_MODEL), f32),
        "mem_norm_g": 1.0 + 0.05 * n(ks[3], (DEPTH, D_MODEL), f32),
        "w_in": n(ks[4], (DEPTH, D_MODEL, D_IN), f32) * D_MODEL ** -0.5,
        "b_f": FORGET_BIAS_INIT + 0.1 * n(ks[5], (DEPTH, N_FOX_HEADS), f32),
        "conv_w": n(ks[6], (DEPTH, CONV_K, CONV_W), f32) * CONV_K ** -0.5,
        "conv_b": 0.02 * n(ks[7], (DEPTH, CONV_W), f32),
        "conv_ln_g": 1.0 + 0.05 * n(ks[8], (DEPTH, CONV_W), f32),
        "conv_ln_b": 0.02 * n(ks[9], (DEPTH, CONV_W), f32),
        "w_conv_pw": n(ks[10], (DEPTH, CONV_W, CONV_W), f32) * CONV_W ** -0.5,
        "w_mem_kv": n(ks[11], (DEPTH, D_MODEL, 2 * MEM_W), f32) * D_MODEL ** -0.5,
        "w_out": n(ks[12], (DEPTH, MIX_W, D_MODEL), f32) * MIX_W ** -0.5,
        "final_g": 1.0 + 0.05 * n(ks[13], (D_MODEL,), f32),
    }


def _fwd_reference(x, mem, norm_g, mem_norm_g, w_in, b_f, conv_w, conv_b, conv_ln_g,
              conv_ln_b, w_conv_pw, w_mem_kv, w_out, final_g):
    B, S, _ = x.shape
    for l in range(DEPTH):
        h = rmsnorm(x, norm_g[l])
        proj = h @ w_in[l]
        (cv_a, cv_b, cv_gate, fq, fk, fv, f_logit, fox_gate,
         mq, mem_gate) = jnp.split(proj, SPLITS, axis=-1)

        u = cv_a * jax.nn.sigmoid(cv_b)
        u = causal_depthwise_conv(u, conv_w[l], conv_b[l])
        u = jax.nn.silu(layernorm(u, conv_ln_g[l], conv_ln_b[l]))
        y_conv = (u @ w_conv_pw[l]) * jax.nn.silu(cv_gate)

        logf = jax.nn.log_sigmoid(f_logit.astype(jnp.float32) + b_f[l].astype(jnp.float32))
        shp = (B, S, N_FOX_HEADS, HEAD_DIM)
        y_fox = fox_attention(fq.reshape(shp), fk.reshape(shp), fv.reshape(shp), logf)
        y_fox = y_fox * jax.nn.silu(fox_gate)

        mkv = rmsnorm(mem, mem_norm_g[l]) @ w_mem_kv[l]
        mk, mv = jnp.split(mkv, 2, axis=-1)
        mshp = (B, mem.shape[1], N_MEM_HEADS, HEAD_DIM)
        y_mem = memory_attention(mq.reshape(B, S, N_MEM_HEADS, HEAD_DIM),
                                 mk.reshape(mshp), mv.reshape(mshp))
        y_mem = y_mem * jax.nn.silu(mem_gate)

        y = jnp.concatenate([y_conv, y_fox, y_mem], axis=-1)
        x = x + y @ w_out[l]
    return rmsnorm(x, final_g)


import jax as _jax
import jax.numpy as _jnp

TWIN_FORMAT = 'train_step'
FWD_PARAMS = ['x', 'mem', 'norm_g', 'mem_norm_g', 'w_in', 'b_f', 'conv_w', 'conv_b', 'conv_ln_g', 'conv_ln_b', 'w_conv_pw', 'w_mem_kv', 'w_out', 'final_g']
TWIN_WEIGHTS = ['norm_g', 'mem_norm_g', 'w_in', 'b_f', 'conv_w', 'conv_b', 'conv_ln_g', 'conv_ln_b', 'w_conv_pw', 'w_mem_kv', 'w_out', 'final_g']
TWIN_DIFF_INPUT = 'x'
TWIN_INPUTS = ['x', 'mem', 'norm_g', 'mem_norm_g', 'w_in', 'b_f', 'conv_w', 'conv_b', 'conv_ln_g', 'conv_ln_b', 'w_conv_pw', 'w_mem_kv', 'w_out', 'final_g', 'loss_target', 'm_norm_g', 'm_mem_norm_g', 'm_w_in', 'm_b_f', 'm_conv_w', 'm_conv_b', 'm_conv_ln_g', 'm_conv_ln_b', 'm_w_conv_pw', 'm_w_mem_kv', 'm_w_out', 'm_final_g', 'v_norm_g', 'v_mem_norm_g', 'v_w_in', 'v_b_f', 'v_conv_w', 'v_conv_b', 'v_conv_ln_g', 'v_conv_ln_b', 'v_w_conv_pw', 'v_w_mem_kv', 'v_w_out', 'v_final_g']
TWIN_OUTPUTS = ['loss', 'grad_x', 'grad_norm_g', 'grad_mem_norm_g', 'grad_w_in', 'grad_b_f', 'grad_conv_w', 'grad_conv_b', 'grad_conv_ln_g', 'grad_conv_ln_b', 'grad_w_conv_pw', 'grad_w_mem_kv', 'grad_w_out', 'grad_final_g', 'delta_norm_g', 'delta_mem_norm_g', 'delta_w_in', 'delta_b_f', 'delta_conv_w', 'delta_conv_b', 'delta_conv_ln_g', 'delta_conv_ln_b', 'delta_w_conv_pw', 'delta_w_mem_kv', 'delta_w_out', 'delta_final_g', 'new_m_norm_g', 'new_m_mem_norm_g', 'new_m_w_in', 'new_m_b_f', 'new_m_conv_w', 'new_m_conv_b', 'new_m_conv_ln_g', 'new_m_conv_ln_b', 'new_m_w_conv_pw', 'new_m_w_mem_kv', 'new_m_w_out', 'new_m_final_g', 'new_v_norm_g', 'new_v_mem_norm_g', 'new_v_w_in', 'new_v_b_f', 'new_v_conv_w', 'new_v_conv_b', 'new_v_conv_ln_g', 'new_v_conv_ln_b', 'new_v_w_conv_pw', 'new_v_w_mem_kv', 'new_v_w_out', 'new_v_final_g']
TWIN_LEAF_KINDS = {'loss': 'loss', 'grad_x': 'grad_x', 'grad_norm_g': 'grad_w', 'grad_mem_norm_g': 'grad_w', 'grad_w_in': 'grad_w', 'grad_b_f': 'grad_w', 'grad_conv_w': 'grad_w', 'grad_conv_b': 'grad_w', 'grad_conv_ln_g': 'grad_w', 'grad_conv_ln_b': 'grad_w', 'grad_w_conv_pw': 'grad_w', 'grad_w_mem_kv': 'grad_w', 'grad_w_out': 'grad_w', 'grad_final_g': 'grad_w', 'delta_norm_g': 'delta_w', 'delta_mem_norm_g': 'delta_w', 'delta_w_in': 'delta_w', 'delta_b_f': 'delta_w', 'delta_conv_w': 'delta_w', 'delta_conv_b': 'delta_w', 'delta_conv_ln_g': 'delta_w', 'delta_conv_ln_b': 'delta_w', 'delta_w_conv_pw': 'delta_w', 'delta_w_mem_kv': 'delta_w', 'delta_w_out': 'delta_w', 'delta_final_g': 'delta_w', 'new_m_norm_g': 'new_m', 'new_m_mem_norm_g': 'new_m', 'new_m_w_in': 'new_m', 'new_m_b_f': 'new_m', 'new_m_conv_w': 'new_m', 'new_m_conv_b': 'new_m', 'new_m_conv_ln_g': 'new_m', 'new_m_conv_ln_b': 'new_m', 'new_m_w_conv_pw': 'new_m', 'new_m_w_mem_kv': 'new_m', 'new_m_w_out': 'new_m', 'new_m_final_g': 'new_m', 'new_v_norm_g': 'new_v', 'new_v_mem_norm_g': 'new_v', 'new_v_w_in': 'new_v', 'new_v_b_f': 'new_v', 'new_v_conv_w': 'new_v', 'new_v_conv_b': 'new_v', 'new_v_conv_ln_g': 'new_v', 'new_v_conv_ln_b': 'new_v', 'new_v_w_conv_pw': 'new_v', 'new_v_w_mem_kv': 'new_v', 'new_v_w_out': 'new_v', 'new_v_final_g': 'new_v'}


def _forward(args):
    return _fwd_reference(*[args[k] for k in FWD_PARAMS])


def _output_shape():
    def fwd():
        inp = _fwd_setup_inputs(0)
        return _fwd_reference(*[inp[k] for k in FWD_PARAMS])
    out = _jax.eval_shape(fwd)
    return out.shape, out.dtype

N_MICROBATCH = 1
ADAM_LR = 0.001
ADAM_B1 = 0.9
ADAM_B2 = 0.999
ADAM_EPS = 1e-08
ADAM_WD = 0.01
ADAM_STEP = 10
PER_EXAMPLE_BATCH_AXIS = {'x': 0, 'mem': 0, 'loss_target': 0}
SHARED_INPUTS = []
_WEIGHT_DTYPES = {'norm_g': _jnp.float32, 'mem_norm_g': _jnp.float32, 'w_in': _jnp.float32, 'b_f': _jnp.float32, 'conv_w': _jnp.float32, 'conv_b': _jnp.float32, 'conv_ln_g': _jnp.float32, 'conv_ln_b': _jnp.float32, 'w_conv_pw': _jnp.float32, 'w_mem_kv': _jnp.float32, 'w_out': _jnp.float32, 'final_g': _jnp.float32}
MOMENT_SCALE = {'norm_g': 5.137452e-02, 'mem_norm_g': 5.510250e-03, 'w_in': 2.842761e-02, 'b_f': 1.729838e-01, 'conv_w': 4.580143e-02, 'conv_b': 9.558713e-02, 'conv_ln_g': 5.081318e-02, 'conv_ln_b': 4.471122e-02, 'w_conv_pw': 4.462533e-02, 'w_mem_kv': 7.645989e-03, 'w_out': 3.090789e-02, 'final_g': 3.202770e+01}


def _to_microbatches(a, axis):
    t = _jnp.moveaxis(a, axis, 0)
    t = t.reshape((N_MICROBATCH, t.shape[0] // N_MICROBATCH) + t.shape[1:])
    return _jnp.moveaxis(t, 1, axis + 1)


def setup_inputs(seed: int = 0) -> dict:
    inp = _fwd_setup_inputs(seed)
    key = _jax.random.fold_in(_jax.random.key(seed), 7919)
    shape, _ = _output_shape()
    out = dict(inp)
    out["loss_target"] = _jax.random.normal(_jax.random.fold_in(key, 0), shape, _jnp.float32)
    for i, name in enumerate(TWIN_WEIGHTS):
        w = inp[name].astype(_jnp.float32)
        if MOMENT_SCALE is None:
            s = _jnp.sqrt(_jnp.mean(_jnp.square(w)) + 1e-30)
        else:
            s = MOMENT_SCALE[name]
        km, kv = _jax.random.split(_jax.random.fold_in(key, i + 1))
        out[name] = w
        out["m_" + name] = s * _jax.random.normal(km, w.shape, _jnp.float32)
        out["v_" + name] = (s * s) * _jax.random.uniform(kv, w.shape, _jnp.float32, 0.5, 1.5)
    if N_MICROBATCH > 1:
        for name, axis in PER_EXAMPLE_BATCH_AXIS.items():
            out[name] = _to_microbatches(out[name], axis)
    return {'x': out['x'], 'mem': out['mem'], 'norm_g': out['norm_g'], 'mem_norm_g': out['mem_norm_g'], 'w_in': out['w_in'], 'b_f': out['b_f'], 'conv_w': out['conv_w'], 'conv_b': out['conv_b'], 'conv_ln_g': out['conv_ln_g'], 'conv_ln_b': out['conv_ln_b'], 'w_conv_pw': out['w_conv_pw'], 'w_mem_kv': out['w_mem_kv'], 'w_out': out['w_out'], 'final_g': out['final_g'], 'loss_target': out['loss_target'], 'm_norm_g': out['m_norm_g'], 'm_mem_norm_g': out['m_mem_norm_g'], 'm_w_in': out['m_w_in'], 'm_b_f': out['m_b_f'], 'm_conv_w': out['m_conv_w'], 'm_conv_b': out['m_conv_b'], 'm_conv_ln_g': out['m_conv_ln_g'], 'm_conv_ln_b': out['m_conv_ln_b'], 'm_w_conv_pw': out['m_w_conv_pw'], 'm_w_mem_kv': out['m_w_mem_kv'], 'm_w_out': out['m_w_out'], 'm_final_g': out['m_final_g'], 'v_norm_g': out['v_norm_g'], 'v_mem_norm_g': out['v_mem_norm_g'], 'v_w_in': out['v_w_in'], 'v_b_f': out['v_b_f'], 'v_conv_w': out['v_conv_w'], 'v_conv_b': out['v_conv_b'], 'v_conv_ln_g': out['v_conv_ln_g'], 'v_conv_ln_b': out['v_conv_ln_b'], 'v_w_conv_pw': out['v_w_conv_pw'], 'v_w_mem_kv': out['v_w_mem_kv'], 'v_w_out': out['v_w_out'], 'v_final_g': out['v_final_g']}


def _loss(weights, diff, rest, loss_target):
    with _jax.named_scope("forward"):
        args = {**rest, TWIN_DIFF_INPUT: diff, **{k: w.astype(_WEIGHT_DTYPES[k]) for k, w in weights.items()}}
        y = _forward(args)
    with _jax.named_scope("loss_head"):
        err = _jnp.square(y.astype(_jnp.float32) - loss_target)
        return 0.5 * _jnp.sum(_jnp.mean(err, axis=-1)) if err.ndim else 0.5 * err


def _adamw(w, g, m, v):
    m = ADAM_B1 * m + (1.0 - ADAM_B1) * g
    v = ADAM_B2 * v + (1.0 - ADAM_B2) * _jnp.square(g)
    m_hat = m / (1.0 - ADAM_B1 ** ADAM_STEP)
    v_hat = v / (1.0 - ADAM_B2 ** ADAM_STEP)
    delta = -ADAM_LR * (m_hat / (_jnp.sqrt(v_hat) + ADAM_EPS) + ADAM_WD * w)
    return delta, m, v


def reference(x, mem, norm_g, mem_norm_g, w_in, b_f, conv_w, conv_b, conv_ln_g, conv_ln_b, w_conv_pw, w_mem_kv, w_out, final_g, loss_target, m_norm_g, m_mem_norm_g, m_w_in, m_b_f, m_conv_w, m_conv_b, m_conv_ln_g, m_conv_ln_b, m_w_conv_pw, m_w_mem_kv, m_w_out, m_final_g, v_norm_g, v_mem_norm_g, v_w_in, v_b_f, v_conv_w, v_conv_b, v_conv_ln_g, v_conv_ln_b, v_w_conv_pw, v_w_mem_kv, v_w_out, v_final_g):
    given = dict(x=x, mem=mem, norm_g=norm_g, mem_norm_g=mem_norm_g, w_in=w_in, b_f=b_f, conv_w=conv_w, conv_b=conv_b, conv_ln_g=conv_ln_g, conv_ln_b=conv_ln_b, w_conv_pw=w_conv_pw, w_mem_kv=w_mem_kv, w_out=w_out, final_g=final_g, loss_target=loss_target, m_norm_g=m_norm_g, m_mem_norm_g=m_mem_norm_g, m_w_in=m_w_in, m_b_f=m_b_f, m_conv_w=m_conv_w, m_conv_b=m_conv_b, m_conv_ln_g=m_conv_ln_g, m_conv_ln_b=m_conv_ln_b, m_w_conv_pw=m_w_conv_pw, m_w_mem_kv=m_w_mem_kv, m_w_out=m_w_out, m_final_g=m_final_g, v_norm_g=v_norm_g, v_mem_norm_g=v_mem_norm_g, v_w_in=v_w_in, v_b_f=v_b_f, v_conv_w=v_conv_w, v_conv_b=v_conv_b, v_conv_ln_g=v_conv_ln_g, v_conv_ln_b=v_conv_ln_b, v_w_conv_pw=v_w_conv_pw, v_w_mem_kv=v_w_mem_kv, v_w_out=v_w_out, v_final_g=v_final_g)
    weights = {n: given[n] for n in TWIN_WEIGHTS}
    shared = {n: given[n] for n in SHARED_INPUTS}
    per_example = {n: given[n] for n in ['x', 'mem']}
    grad_fn = _jax.value_and_grad(_loss, argnums=(0, 1))

    def one_microbatch(ex, loss_target):
        ex = dict(ex)
        diff = ex.pop(TWIN_DIFF_INPUT)
        return grad_fn(weights, diff, {**shared, **ex}, loss_target)

    if N_MICROBATCH == 1:
        loss, (grad_w, grad_x) = one_microbatch(per_example, given["loss_target"])
    else:
        def body(carry, xs):
            loss_sum, grad_sum = carry
            l_k, (gw_k, gx_k) = one_microbatch(xs[0], xs[1])
            with _jax.named_scope("update"):
                return (loss_sum + l_k, _jax.tree.map(_jnp.add, grad_sum, gw_k)), gx_k

        init = (_jnp.zeros((), _jnp.float32), _jax.tree.map(_jnp.zeros_like, weights))
        (loss, grad_w), grad_x = _jax.lax.scan(body, init, (per_example, given["loss_target"]))
    with _jax.named_scope("update"):
        delta_w, new_m, new_v = {}, {}, {}
        for n in TWIN_WEIGHTS:
            delta_w[n], new_m[n], new_v[n] = _adamw(weights[n], grad_w[n], given["m_" + n], given["v_" + n])
    return (loss, grad_x, *[grad_w[n] for n in TWIN_WEIGHTS], *[delta_w[n] for n in TWIN_WEIGHTS],
            *[new_m[n] for n in TWIN_WEIGHTS], *[new_v[n] for n in TWIN_WEIGHTS])
```

```python
import functools

import jax
import jax.numpy as jnp
from jax import lax
from jax.experimental import pallas as pl
from jax.experimental.pallas import tpu as pltpu

F32 = jnp.float32
BF16 = jnp.bfloat16
MESH = pl.DeviceIdType.MESH

HEAD = 128
N_FOX = 8
N_MEMH = 4
CONV_W = 512
FOX_W = N_FOX * HEAD
MEM_W = N_MEMH * HEAD
MIX_W = CONV_W + FOX_W + MEM_W
CONV_K = 31
CONV_HALO = 32
EPS = 1e-6
NEG = -1e30
LANE = 128
N_CHIPS = 4
N_DEV = 8
VMEM_LIMIT = 56 * 1024 * 1024

ADAM_LR = 0.001
ADAM_B1 = 0.9
ADAM_B2 = 0.999
ADAM_EPS = 1e-08
ADAM_WD = 0.01
ADAM_STEP = 10

NT = (((1,), (1,)), ((), ()))
TN = (((0,), (0,)), ((), ()))
NN = (((1,), (0,)), ((), ()))


def _params(sem=None):
    kw = dict(vmem_limit_bytes=VMEM_LIMIT)
    if sem is not None:
        kw["dimension_semantics"] = sem
    return pltpu.CompilerParams(**kw)


def _sigmoid(v):
    return jax.nn.sigmoid(v)


def _dsilu(v, s):
    return s * (1.0 + v * (1.0 - s))


def _rms_fwd(x, g, *, tm, name):
    R, D = x.shape

    def body(x_ref, g_ref, h_ref):
        xv = x_ref[...]
        r = lax.rsqrt(jnp.mean(xv * xv, axis=-1, keepdims=True) + EPS)
        h_ref[...] = (xv * r * g_ref[...]).astype(BF16)

    return pl.pallas_call(
        body, name=name, grid=(R // tm,),
        in_specs=[pl.BlockSpec((tm, D), lambda i: (i, 0)), pl.BlockSpec((1, D), lambda i: (0, 0))],
        out_specs=pl.BlockSpec((tm, D), lambda i: (i, 0)),
        out_shape=jax.ShapeDtypeStruct((R, D), BF16),
        compiler_params=_params(("parallel",)),
    )(x, g)


def _rms_bwd(x, g, dh, dres, *, tm, name):
    R, D = x.shape
    has_res = dres is not None

    def body(*refs):
        if has_res:
            x_ref, g_ref, dh_ref, dres_ref, dx_ref, dg_ref = refs
        else:
            x_ref, g_ref, dh_ref, dx_ref, dg_ref = refs
        i = pl.program_id(0)

        @pl.when(i == 0)
        def _():
            dg_ref[...] = jnp.zeros_like(dg_ref)

        xv = x_ref[...]
        r = lax.rsqrt(jnp.mean(xv * xv, axis=-1, keepdims=True) + EPS)
        n = xv * r
        dh = dh_ref[...]
        dg_ref[...] += jnp.sum(dh * n, axis=0, keepdims=True)
        dn = dh * g_ref[...]
        dx = r * (dn - n * jnp.mean(dn * n, axis=-1, keepdims=True))
        if has_res:
            dx = dx + dres_ref[...]
        dx_ref[...] = dx

    row = pl.BlockSpec((tm, D), lambda i: (i, 0))
    vec = pl.BlockSpec((1, D), lambda i: (0, 0))
    ins = [row, vec, row] + ([row] if has_res else [])
    args = (x, g, dh) + ((dres,) if has_res else ())
    return pl.pallas_call(
        body, name=name, grid=(R // tm,),
        in_specs=ins, out_specs=[row, vec],
        out_shape=[jax.ShapeDtypeStruct((R, D), F32), jax.ShapeDtypeStruct((1, D), F32)],
        compiler_params=_params(("arbitrary",)),
    )(*args)


def _mm(a, b, *, ta=False, tb=False, out_dtype=F32, add=None, tm, tn, tk, n=None, b_off=0, name):
    M, K = (a.shape[1], a.shape[0]) if ta else a.shape
    nb = b.shape[0] if tb else b.shape[1]
    n = nb if n is None else n
    tm, tn, tk = min(tm, M), min(tn, n), min(tk, K)
    assert M % tm == 0 and n % tn == 0 and K % tk == 0, (name, M, n, K, tm, tn, tk)
    nk = K // tk
    has_add = add is not None

    def body(*refs):
        if has_add:
            a_ref, b_ref, add_ref, o_ref, acc_ref = refs
        else:
            a_ref, b_ref, o_ref, acc_ref = refs
        k = pl.program_id(2)

        @pl.when(k == 0)
        def _():
            acc_ref[...] = jnp.zeros_like(acc_ref)

        av = a_ref[...].astype(BF16)
        bv = b_ref[...].astype(BF16)
        dims = (((0 if ta else 1,), (1 if tb else 0,)), ((), ()))
        acc_ref[...] += lax.dot_general(av, bv, dims, preferred_element_type=F32)

        @pl.when(k == nk - 1)
        def _():
            r = acc_ref[...]
            if has_add:
                r = r + add_ref[...]
            o_ref[...] = r.astype(out_dtype)

    a_spec = pl.BlockSpec((tk, tm), lambda i, j, k: (k, i)) if ta else pl.BlockSpec((tm, tk), lambda i, j, k: (i, k))
    b_spec = (pl.BlockSpec((tn, tk), lambda i, j, k: (j + b_off, k)) if tb
              else pl.BlockSpec((tk, tn), lambda i, j, k: (k, j + b_off)))
    o_spec = pl.BlockSpec((tm, tn), lambda i, j, k: (i, j))
    ins = [a_spec, b_spec] + ([o_spec] if has_add else [])
    args = (a, b) + ((add,) if has_add else ())
    return pl.pallas_call(
        body, name=name, grid=(M // tm, n // tn, nk),
        in_specs=ins, out_specs=o_spec,
        out_shape=jax.ShapeDtypeStruct((M, n), out_dtype),
        scratch_shapes=[pltpu.VMEM((tm, tn), F32)],
        compiler_params=_params(("parallel", "parallel", "arbitrary")),
    )(*args)


def _conv_taps(w_ref, e_ref, first, tt):
    acc = w_ref[0:1, :] * e_ref[first:first + tt, :]
    for k in range(1, CONV_K):
        acc = acc + w_ref[k:k + 1, :] * e_ref[first + k:first + k + tt, :]
    return acc


def _conv_fwd(pc, cw, cb, lng, lnb, wpw, *, tt, name):
    S = pc.shape[0]
    C = CONV_W
    lead = CONV_HALO - (CONV_K - 1)

    def body(a_ref, b_ref, gc_ref, cw_ref, cb_ref, lng_ref, lnb_ref, wpw_ref, y_ref, e_ref):
        i = pl.program_id(0)

        @pl.when(i == 0)
        def _():
            e_ref[0:CONV_HALO, :] = jnp.zeros((CONV_HALO, C), F32)

        @pl.when(i > 0)
        def _():
            e_ref[0:CONV_HALO, :] = e_ref[tt:tt + CONV_HALO, :]

        e_ref[CONV_HALO:CONV_HALO + tt, :] = a_ref[...] * _sigmoid(b_ref[...])
        u1 = _conv_taps(cw_ref, e_ref, lead, tt) + cb_ref[...]
        mu = jnp.mean(u1, axis=-1, keepdims=True)
        xc = u1 - mu
        rstd = lax.rsqrt(jnp.mean(xc * xc, axis=-1, keepdims=True) + EPS)
        u2 = xc * rstd * lng_ref[...] + lnb_ref[...]
        u3 = u2 * _sigmoid(u2)
        z = jnp.dot(u3.astype(BF16), wpw_ref[...], preferred_element_type=F32)
        gc = gc_ref[...]
        y_ref[...] = (z * gc * _sigmoid(gc)).astype(BF16)

    col = lambda c: pl.BlockSpec((tt, C), lambda i, c=c: (i, c))
    vec = pl.BlockSpec((1, C), lambda i: (0, 0))
    return pl.pallas_call(
        body, name=name, grid=(S // tt,),
        in_specs=[col(0), col(1), col(2), pl.BlockSpec((CONV_HALO, C), lambda i: (0, 0)), vec, vec, vec,
                  pl.BlockSpec((C, C), lambda i: (0, 0))],
        out_specs=pl.BlockSpec((tt, C), lambda i: (i, 0)),
        out_shape=jax.ShapeDtypeStruct((S, C), BF16),
        scratch_shapes=[pltpu.VMEM((tt + CONV_HALO, C), F32)],
        compiler_params=_params(("arbitrary",)),
    )(pc, pc, pc, cw, cb, lng, lnb, wpw)


def _conv_bwd(pc, dy, cw, cb, lng, lnb, wpw, *, tt, name):
    S = pc.shape[0]
    C = CONV_W
    nt = S // tt
    hb = tt // CONV_HALO
    lead = CONV_HALO - (CONV_K - 1)

    def body(a_ref, b_ref, gc_ref, ah_ref, bh_ref, dy_ref, cw_ref, cb_ref, lng_ref, lnb_ref, wpw_ref,
             dpc_ref, dwpw_ref, dcw_ref, dsm_ref, eu_ref, ed_ref):
        i = pl.program_id(0)
        ti = nt - 1 - i

        @pl.when(i == 0)
        def _():
            ed_ref[tt:tt + CONV_HALO, :] = jnp.zeros((CONV_HALO, C), F32)
            dwpw_ref[...] = jnp.zeros_like(dwpw_ref)
            dcw_ref[...] = jnp.zeros_like(dcw_ref)
            dsm_ref[...] = jnp.zeros_like(dsm_ref)

        @pl.when(i > 0)
        def _():
            ed_ref[tt:tt + CONV_HALO, :] = ed_ref[0:CONV_HALO, :]

        keep = jnp.where(ti > 0, 1.0, 0.0).astype(F32)
        eu_ref[0:CONV_HALO, :] = ah_ref[...] * _sigmoid(bh_ref[...]) * keep
        a = a_ref[...]
        sb = _sigmoid(b_ref[...])
        eu_ref[CONV_HALO:CONV_HALO + tt, :] = a * sb

        u1 = _conv_taps(cw_ref, eu_ref, lead, tt) + cb_ref[...]
        mu = jnp.mean(u1, axis=-1, keepdims=True)
        xc = u1 - mu
        rstd = lax.rsqrt(jnp.mean(xc * xc, axis=-1, keepdims=True) + EPS)
        nhat = xc * rstd
        g = lng_ref[...]
        u2 = nhat * g + lnb_ref[...]
        s2 = _sigmoid(u2)
        u3 = (u2 * s2).astype(BF16)
        z = jnp.dot(u3, wpw_ref[...], preferred_element_type=F32)

        gc = gc_ref[...]
        sg = _sigmoid(gc)
        dyv = dy_ref[...]
        dz = (dyv * gc * sg).astype(BF16)
        dpc_ref[:, 2 * C:3 * C] = (dyv * z * _dsilu(gc, sg)).astype(BF16)

        du3 = lax.dot_general(dz, wpw_ref[...], NT, preferred_element_type=F32)
        dwpw_ref[...] += lax.dot_general(u3, dz, TN, preferred_element_type=F32)
        du2 = du3 * _dsilu(u2, s2)
        dsm_ref[1:2, :] += jnp.sum(du2 * nhat, axis=0, keepdims=True)
        dsm_ref[2:3, :] += jnp.sum(du2, axis=0, keepdims=True)
        dn = du2 * g
        du1 = rstd * (dn - jnp.mean(dn, axis=-1, keepdims=True)
                      - nhat * jnp.mean(dn * nhat, axis=-1, keepdims=True))
        dsm_ref[0:1, :] += jnp.sum(du1, axis=0, keepdims=True)
        ed_ref[0:tt, :] = du1

        du0 = cw_ref[0:1, :] * ed_ref[CONV_K - 1:CONV_K - 1 + tt, :]
        for k in range(1, CONV_K):
            o = CONV_K - 1 - k
            du0 = du0 + cw_ref[k:k + 1, :] * ed_ref[o:o + tt, :]
        for k in range(CONV_K):
            dcw_ref[k:k + 1, :] += jnp.sum(du1 * eu_ref[lead + k:lead + k + tt, :], axis=0, keepdims=True)

        dpc_ref[:, 0:C] = (du0 * sb).astype(BF16)
        dpc_ref[:, C:2 * C] = (du0 * a * sb * (1.0 - sb)).astype(BF16)

    col = lambda c: pl.BlockSpec((tt, C), lambda i, c=c: (nt - 1 - i, c))
    halo = lambda c: pl.BlockSpec((CONV_HALO, C), lambda i, c=c: (jnp.maximum((nt - 1 - i) * hb - 1, 0), c))
    vec = pl.BlockSpec((1, C), lambda i: (0, 0))
    fixed = lambda r: pl.BlockSpec((r, C), lambda i: (0, 0))
    return pl.pallas_call(
        body, name=name, grid=(nt,),
        in_specs=[col(0), col(1), col(2), halo(0), halo(1),
                  pl.BlockSpec((tt, C), lambda i: (nt - 1 - i, 0)),
                  fixed(CONV_HALO), vec, vec, vec, fixed(C)],
        out_specs=[pl.BlockSpec((tt, 3 * C), lambda i: (nt - 1 - i, 0)), fixed(C), fixed(CONV_HALO), fixed(8)],
        out_shape=[jax.ShapeDtypeStruct((S, 3 * C), BF16), jax.ShapeDtypeStruct((C, C), F32),
                   jax.ShapeDtypeStruct((CONV_HALO, C), F32), jax.ShapeDtypeStruct((8, C), F32)],
        scratch_shapes=[pltpu.VMEM((tt + CONV_HALO, C), F32), pltpu.VMEM((tt + CONV_HALO, C), F32)],
        compiler_params=_params(("arbitrary",)),
    )(pc, pc, pc, pc, pc, dy, cw, cb, lng, lnb, wpw)


def _tri(n, lower):
    r = lax.broadcasted_iota(jnp.int32, (n, n), 0)
    c = lax.broadcasted_iota(jnp.int32, (n, n), 1)
    return jnp.where((r >= c) if lower else (r <= c), 1.0, 0.0).astype(F32)


def _fox_cumsum(pfl, bf, *, tc, name):
    S = pfl.shape[0]

    def body(fl_ref, bf_ref, c_ref, carry_ref):
        i = pl.program_id(0)

        @pl.when(i == 0)
        def _():
            carry_ref[...] = jnp.zeros_like(carry_ref)

        z = fl_ref[...] + bf_ref[...]
        logf = jnp.minimum(z, 0.0) - jnp.log1p(jnp.exp(-jnp.abs(z)))
        c = jnp.dot(_tri(tc, True), logf, precision=lax.Precision.HIGHEST,
                    preferred_element_type=F32) + carry_ref[0:1, :]
        c_ref[...] = c
        carry_ref[0:1, :] = c[tc - 1:tc, :]

    return pl.pallas_call(
        body, name=name, grid=(S // tc,),
        in_specs=[pl.BlockSpec((tc, LANE), lambda i: (i, 0)), pl.BlockSpec((1, LANE), lambda i: (0, 0))],
        out_specs=pl.BlockSpec((tc, LANE), lambda i: (i, 0)),
        out_shape=jax.ShapeDtypeStruct((S, LANE), F32),
        scratch_shapes=[pltpu.VMEM((8, LANE), F32)],
        compiler_params=_params(("arbitrary",)),
    )(pfl, bf)


def _fox_dlogit(dc, pfl, bf, *, tc, name):
    S = pfl.shape[0]
    nt = S // tc

    def body(dc_ref, fl_ref, bf_ref, dfl_ref, dbf_ref, carry_ref):
        i = pl.program_id(0)

        @pl.when(i == 0)
        def _():
            carry_ref[...] = jnp.zeros_like(carry_ref)
            dbf_ref[...] = jnp.zeros_like(dbf_ref)

        dlogf = jnp.dot(_tri(tc, False), dc_ref[...], precision=lax.Precision.HIGHEST,
                        preferred_element_type=F32) + carry_ref[0:1, :]
        carry_ref[0:1, :] = dlogf[0:1, :]
        dz = dlogf * _sigmoid(-(fl_ref[...] + bf_ref[...]))
        dfl_ref[...] = dz.astype(BF16)
        dbf_ref[0:1, :] += jnp.sum(dz, axis=0, keepdims=True)

    rev = pl.BlockSpec((tc, LANE), lambda i: (nt - 1 - i, 0))
    return pl.pallas_call(
        body, name=name, grid=(nt,),
        in_specs=[rev, rev, pl.BlockSpec((1, LANE), lambda i: (0, 0))],
        out_specs=[rev, pl.BlockSpec((8, LANE), lambda i: (0, 0))],
        out_shape=[jax.ShapeDtypeStruct((S, LANE), BF16), jax.ShapeDtypeStruct((8, LANE), F32)],
        scratch_shapes=[pltpu.VMEM((8, LANE), F32)],
        compiler_params=_params(("arbitrary",)),
    )(dc, pfl, bf)


def _causal(tq, q_is_row):
    r = lax.broadcasted_iota(jnp.int32, (tq, tq), 0)
    c = lax.broadcasted_iota(jnp.int32, (tq, tq), 1)
    return (r >= c) if q_is_row else (c >= r)


def _fox_fwd(qkv, pg, c_col, c_row, *, tq, name):
    S = qkv.shape[0]
    nq = S // tq
    scale = HEAD ** -0.5

    def body(q_ref, k_ref, v_ref, g_ref, cq_ref, cr_ref, y_ref, o_ref, lse_ref):
        i = pl.program_id(1)
        q = q_ref[...]
        cq = cq_ref[...]

        def blk(j, carry, masked):
            m, l, acc = carry
            off = pl.multiple_of(j * tq, tq)
            kj = k_ref[pl.ds(off, tq), :]
            vj = v_ref[pl.ds(off, tq), :]
            s = lax.dot_general(q, kj, NT, preferred_element_type=F32) * scale + (cq - cr_ref[j])
            if masked:
                s = jnp.where(_causal(tq, True), s, NEG)
            m_new = jnp.maximum(m, jnp.max(s, axis=-1, keepdims=True))
            alpha = jnp.exp(m - m_new)
            p = jnp.exp(s - m_new)
            l = alpha * l + jnp.sum(p, axis=-1, keepdims=True)
            acc = alpha * acc + jnp.dot(p.astype(BF16), vj, preferred_element_type=F32)
            return m_new, l, acc

        init = (jnp.full((tq, 1), NEG, F32), jnp.zeros((tq, 1), F32), jnp.zeros((tq, HEAD), F32))
        carry = lax.fori_loop(0, i, lambda j, c: blk(j, c, False), init)
        m, l, acc = blk(i, carry, True)
        o = acc / l
        g = g_ref[...]
        y_ref[...] = (o * g * _sigmoid(g)).astype(BF16)
        o_ref[...] = o
        lse_ref[...] = m + jnp.log(l)

    tile = lambda base: pl.BlockSpec((tq, HEAD), lambda h, i, base=base: (i, base + h))
    full = lambda base: pl.BlockSpec((S, HEAD), lambda h, i, base=base: (0, base + h))
    colv = pl.BlockSpec((None, tq, 1), lambda h, i: (h, i, 0))
    return pl.pallas_call(
        body, name=name, grid=(N_FOX, nq),
        in_specs=[tile(0), full(N_FOX), full(2 * N_FOX), tile(0), colv,
                  pl.BlockSpec((None, nq, 1, tq), lambda h, i: (h, 0, 0, 0))],
        out_specs=[tile(0), tile(0), colv],
        out_shape=[jax.ShapeDtypeStruct((S, FOX_W), BF16), jax.ShapeDtypeStruct((S, FOX_W), F32),
                   jax.ShapeDtypeStruct((N_FOX, S, 1), F32)],
        compiler_params=_params(("parallel", "arbitrary")),
    )(qkv, qkv, qkv, pg, c_col, c_row)


def _fox_bwd_prep(dy, o, pg, *, tq, name):
    S = o.shape[0]
    base = CONV_W // HEAD

    def body(dy_ref, o_ref, g_ref, do_ref, dg_ref, dl_ref):
        g = g_ref[...]
        sg = _sigmoid(g)
        dyv = dy_ref[...]
        ov = o_ref[...]
        do = dyv * g * sg
        do_ref[...] = do.astype(BF16)
        dg_ref[...] = (dyv * ov * _dsilu(g, sg)).astype(BF16)
        dl_ref[...] = jnp.sum(do * ov, axis=-1, keepdims=True)

    tile = lambda b: pl.BlockSpec((tq, HEAD), lambda h, i, b=b: (i, b + h))
    return pl.pallas_call(
        body, name=name, grid=(N_FOX, S // tq),
        in_specs=[tile(base), tile(0), tile(0)],
        out_specs=[tile(0), tile(0), pl.BlockSpec((None, tq, 1), lambda h, i: (h, i, 0))],
        out_shape=[jax.ShapeDtypeStruct((S, FOX_W), BF16), jax.ShapeDtypeStruct((S, FOX_W), BF16),
                   jax.ShapeDtypeStruct((N_FOX, S, 1), F32)],
        compiler_params=_params(("parallel", "parallel")),
    )(dy, o, pg)


def _fox_bwd_dq(qkv, do, c_col, c_row, lse, delta0, *, tq, name):
    S = qkv.shape[0]
    nq = S // tq
    scale = HEAD ** -0.5

    def body(q_ref, k_ref, v_ref, do_ref, cq_ref, cr_ref, lse_ref, dl_ref, dq_ref, dlo_ref):
        i = pl.program_id(1)
        q = q_ref[...]
        dov = do_ref[...]
        cq = cq_ref[...]
        lsev = lse_ref[...]
        dlv = dl_ref[...]

        def blk(j, carry, masked):
            acc, pk, r = carry
            off = pl.multiple_of(j * tq, tq)
            kj = k_ref[pl.ds(off, tq), :]
            vj = v_ref[pl.ds(off, tq), :]
            s = lax.dot_general(q, kj, NT, preferred_element_type=F32) * scale + (cq - cr_ref[j])
            if masked:
                s = jnp.where(_causal(tq, True), s, NEG)
            p = jnp.exp(s - lsev)
            dp = lax.dot_general(dov, vj, NT, preferred_element_type=F32)
            ds = p * (dp - dlv)
            acc = acc + jnp.dot(ds.astype(BF16), kj, preferred_element_type=F32)
            pk = pk + jnp.dot(p.astype(BF16), kj, preferred_element_type=F32)
            r = r + jnp.sum(ds, axis=-1, keepdims=True)
            return acc, pk, r

        init = (jnp.zeros((tq, HEAD), F32), jnp.zeros((tq, HEAD), F32), jnp.zeros((tq, 1), F32))
        carry = lax.fori_loop(0, i, lambda j, c: blk(j, c, False), init)
        acc, pk, r = blk(i, carry, True)
        dq_ref[...] = ((acc - r * pk) * scale).astype(BF16)
        dlo_ref[...] = dlv + r

    tile = lambda base: pl.BlockSpec((tq, HEAD), lambda h, i, base=base: (i, base + h))
    full = lambda base: pl.BlockSpec((S, HEAD), lambda h, i, base=base: (0, base + h))
    colv = pl.BlockSpec((None, tq, 1), lambda h, i: (h, i, 0))
    return pl.pallas_call(
        body, name=name, grid=(N_FOX, nq),
        in_specs=[tile(0), full(N_FOX), full(2 * N_FOX), tile(0), colv,
                  pl.BlockSpec((None, nq, 1, tq), lambda h, i: (h, 0, 0, 0)), colv, colv],
        out_specs=[tile(0), colv],
        out_shape=[jax.ShapeDtypeStruct((S, FOX_W), BF16), jax.ShapeDtypeStruct((N_FOX, S, 1), F32)],
        compiler_params=_params(("parallel", "arbitrary")),
    )(qkv, qkv, qkv, do, c_col, c_row, lse, delta0)


def _fox_bwd_dkv(qkv, do, c_col, c_row, lse_row, delta_row, *, tq, name):
    S = qkv.shape[0]
    nq = S // tq
    scale = HEAD ** -0.5

    def body(k_ref, v_ref, q_ref, do_ref, ck_ref, cqr_ref, lser_ref, dlr_ref, dk_ref, dv_ref, dc_ref):
        j = pl.program_id(1)
        kj = k_ref[...]
        vj = v_ref[...]
        ck = ck_ref[...]

        def blk(i, carry, masked):
            dk, dv, dc = carry
            off = pl.multiple_of(i * tq, tq)
            qi = q_ref[pl.ds(off, tq), :]
            doi = do_ref[pl.ds(off, tq), :]
            st = lax.dot_general(kj, qi, NT, preferred_element_type=F32) * scale + (cqr_ref[i] - ck)
            if masked:
                st = jnp.where(_causal(tq, False), st, NEG)
            pt = jnp.exp(st - lser_ref[i])
            dv = dv + jnp.dot(pt.astype(BF16), doi, preferred_element_type=F32)
            dpt = lax.dot_general(vj, doi, NT, preferred_element_type=F32)
            dst = pt * (dpt - dlr_ref[i])
            dk = dk + jnp.dot(dst.astype(BF16), qi, preferred_element_type=F32)
            dc = dc - jnp.sum(dst, axis=-1, keepdims=True)
            return dk, dv, dc

        init = (jnp.zeros((tq, HEAD), F32), jnp.zeros((tq, HEAD), F32), jnp.zeros((tq, 1), F32))
        carry = blk(j, init, True)
        dk, dv, dc = lax.fori_loop(j + 1, nq, lambda i, c: blk(i, c, False), carry)
        dk_ref[...] = (dk * scale).astype(BF16)
        dv_ref[...] = dv.astype(BF16)
        dc_ref[...] = dc

    tile = lambda base: pl.BlockSpec((tq, HEAD), lambda h, j, base=base: (j, base + h))
    full = lambda base: pl.BlockSpec((S, HEAD), lambda h, j, base=base: (0, base + h))
    colv = pl.BlockSpec((None, tq, 1), lambda h, j: (h, j, 0))
    rowv = pl.BlockSpec((None, nq, 1, tq), lambda h, j: (h, 0, 0, 0))
    return pl.pallas_call(
        body, name=name, grid=(N_FOX, nq),
        in_specs=[tile(N_FOX), tile(2 * N_FOX), full(0), full(0), colv, rowv, rowv, rowv],
        out_specs=[tile(0), tile(0), colv],
        out_shape=[jax.ShapeDtypeStruct((S, FOX_W), BF16), jax.ShapeDtypeStruct((S, FOX_W), BF16),
                   jax.ShapeDtypeStruct((N_FOX, S, 1), F32)],
        compiler_params=_params(("parallel", "arbitrary")),
    )(qkv, qkv, qkv, do, c_col, c_row, lse_row, delta_row)


def _mem_heads(mq, mkv, h):
    lo = h * HEAD
    qh = mq[:, lo:lo + HEAD].astype(BF16)
    kh = mkv[:, lo:lo + HEAD].astype(BF16)
    vh = mkv[:, MEM_W + lo:MEM_W + lo + HEAD].astype(BF16)
    return qh, kh, vh


def _mem_softmax(qh, kh):
    s = lax.dot_general(qh, kh, NT, preferred_element_type=F32) * (HEAD ** -0.5)
    e = jnp.exp(s - jnp.max(s, axis=-1, keepdims=True))
    return e / jnp.sum(e, axis=-1, keepdims=True)


def _mem_fwd(pg, mkv, *, tq, name):
    S = pg.shape[0]
    M = mkv.shape[0]
    qb = FOX_W // MEM_W

    def body(mq_ref, g_ref, mkv_ref, y_ref):
        mq = mq_ref[...]
        mkvv = mkv_ref[...]
        for h in range(N_MEMH):
            qh, kh, vh = _mem_heads(mq, mkvv, h)
            p = _mem_softmax(qh, kh)
            o = jnp.dot(p.astype(BF16), vh, preferred_element_type=F32)
            g = g_ref[:, h * HEAD:(h + 1) * HEAD]
            y_ref[:, h * HEAD:(h + 1) * HEAD] = (o * g * _sigmoid(g)).astype(BF16)

    return pl.pallas_call(
        body, name=name, grid=(S // tq,),
        in_specs=[pl.BlockSpec((tq, MEM_W), lambda i: (i, qb)), pl.BlockSpec((tq, MEM_W), lambda i: (i, qb + 1)),
                  pl.BlockSpec((M, 2 * MEM_W), lambda i: (0, 0))],
        out_specs=pl.BlockSpec((tq, MEM_W), lambda i: (i, 0)),
        out_shape=jax.ShapeDtypeStruct((S, MEM_W), BF16),
        compiler_params=_params(("parallel",)),
    )(pg, pg, mkv)


def _mem_bwd(pg, mkv, dy, *, tq, name):
    S = pg.shape[0]
    M = mkv.shape[0]
    qb = FOX_W // MEM_W
    yb = (CONV_W + FOX_W) // MEM_W
    scale = HEAD ** -0.5

    def body(mq_ref, g_ref, mkv_ref, dy_ref, dmq_ref, dg_ref, dmkv_ref):
        i = pl.program_id(0)

        @pl.when(i == 0)
        def _():
            dmkv_ref[...] = jnp.zeros_like(dmkv_ref)

        mq = mq_ref[...]
        mkvv = mkv_ref[...]
        for h in range(N_MEMH):
            lo = h * HEAD
            qh, kh, vh = _mem_heads(mq, mkvv, h)
            p = _mem_softmax(qh, kh)
            o = jnp.dot(p.astype(BF16), vh, preferred_element_type=F32)
            g = g_ref[:, lo:lo + HEAD]
            sg = _sigmoid(g)
            dyh = dy_ref[:, lo:lo + HEAD]
            do = dyh * g * sg
            dg_ref[:, lo:lo + HEAD] = (dyh * o * _dsilu(g, sg)).astype(BF16)
            dob = do.astype(BF16)
            dp = lax.dot_general(dob, vh, NT, preferred_element_type=F32)
            ds = p * (dp - jnp.sum(do * o, axis=-1, keepdims=True))
            dsb = ds.astype(BF16)
            dmq_ref[:, lo:lo + HEAD] = (jnp.dot(dsb, kh, preferred_element_type=F32) * scale).astype(BF16)
            dmkv_ref[:, lo:lo + HEAD] += lax.dot_general(dsb, qh, TN, preferred_element_type=F32) * scale
            dmkv_ref[:, MEM_W + lo:MEM_W + lo + HEAD] += lax.dot_general(
                p.astype(BF16), dob, TN, preferred_element_type=F32)

    return pl.pallas_call(
        body, name=name, grid=(S // tq,),
        in_specs=[pl.BlockSpec((tq, MEM_W), lambda i: (i, qb)), pl.BlockSpec((tq, MEM_W), lambda i: (i, qb + 1)),
                  pl.BlockSpec((M, 2 * MEM_W), lambda i: (0, 0)), pl.BlockSpec((tq, MEM_W), lambda i: (i, yb))],
        out_specs=[pl.BlockSpec((tq, MEM_W), lambda i: (i, 0)), pl.BlockSpec((tq, MEM_W), lambda i: (i, 0)),
                   pl.BlockSpec((M, 2 * MEM_W), lambda i: (0, 0))],
        out_shape=[jax.ShapeDtypeStruct((S, MEM_W), BF16), jax.ShapeDtypeStruct((S, MEM_W), BF16),
                   jax.ShapeDtypeStruct((M, 2 * MEM_W), F32)],
        compiler_params=_params(("arbitrary",)),
    )(pg, pg, mkv, dy)


def _final(x2, target, fg, *, tm, name):
    S, D = x2.shape

    def body(x_ref, t_ref, g_ref, dx_ref, dg_ref, ls_ref):
        i = pl.program_id(0)

        @pl.when(i == 0)
        def _():
            dg_ref[...] = jnp.zeros_like(dg_ref)
            ls_ref[...] = jnp.zeros_like(ls_ref)

        xv = x_ref[...]
        r = lax.rsqrt(jnp.mean(xv * xv, axis=-1, keepdims=True) + EPS)
        n = xv * r
        g = g_ref[...]
        diff = n * g - t_ref[...]
        ls_ref[...] += jnp.sum(diff * diff)
        dout = diff * (1.0 / D)
        dg_ref[...] += jnp.sum(dout * n, axis=0, keepdims=True)
        dn = dout * g
        dx_ref[...] = r * (dn - n * jnp.mean(dn * n, axis=-1, keepdims=True))

    row = pl.BlockSpec((tm, D), lambda i: (i, 0))
    vec = pl.BlockSpec((1, D), lambda i: (0, 0))
    return pl.pallas_call(
        body, name=name, grid=(S // tm,),
        in_specs=[row, row, vec],
        out_specs=[row, vec, pl.BlockSpec((8, LANE), lambda i: (0, 0))],
        out_shape=[jax.ShapeDtypeStruct((S, D), F32), jax.ShapeDtypeStruct((1, D), F32),
                   jax.ShapeDtypeStruct((8, LANE), F32)],
        compiler_params=_params(("arbitrary",)),
    )(x2, target, fg)


def _adamw(w, g, m, v, *, name):
    R, C = w.shape
    tr = R
    for cand in (256, 128, 64, 32, 16, 8):
        if R % cand == 0 and R > cand:
            tr = cand
            break
    c1 = 1.0 - ADAM_B1 ** ADAM_STEP
    c2 = 1.0 - ADAM_B2 ** ADAM_STEP

    def body(w_ref, g_ref, m_ref, v_ref, d_ref, nm_ref, nv_ref):
        gv = g_ref[...]
        nm = ADAM_B1 * m_ref[...] + (1.0 - ADAM_B1) * gv
        nv = ADAM_B2 * v_ref[...] + (1.0 - ADAM_B2) * (gv * gv)
        nm_ref[...] = nm
        nv_ref[...] = nv
        d_ref[...] = -ADAM_LR * ((nm / c1) / (jnp.sqrt(nv / c2) + ADAM_EPS) + ADAM_WD * w_ref[...])

    spec = pl.BlockSpec((tr, C), lambda i: (i, 0))
    shp = jax.ShapeDtypeStruct((R, C), F32)
    return pl.pallas_call(
        body, name=name, grid=(R // tr,),
        in_specs=[spec] * 4, out_specs=[spec] * 3, out_shape=[shp] * 3,
        compiler_params=_params(("parallel",)),
    )(w, g, m, v)


def _sum4(q, *, name):
    _, R, C = q.shape
    tr = R
    for cand in (256, 128, 64, 32, 16, 8):
        if R % cand == 0 and R > cand:
            tr = cand
            break

    def body(q_ref, o_ref):
        o_ref[...] = ((q_ref[0] + q_ref[1]) + q_ref[2]) + q_ref[3]

    return pl.pallas_call(
        body, name=name, grid=(R // tr,),
        in_specs=[pl.BlockSpec((N_CHIPS, tr, C), lambda i: (0, i, 0))],
        out_specs=pl.BlockSpec((tr, C), lambda i: (i, 0)),
        out_shape=jax.ShapeDtypeStruct((R, C), F32),
        compiler_params=_params(("parallel",)),
    )(q)


def _add2(a, b, *, name):
    K, R, C = a.shape
    tr = R
    for cand in (256, 128, 64, 32, 16, 8):
        if R % cand == 0 and R > cand:
            tr = cand
            break

    def body(a_ref, b_ref, o_ref):
        o_ref[...] = a_ref[...] + b_ref[...]

    spec = pl.BlockSpec((None, tr, C), lambda k, i: (k, i, 0))
    return pl.pallas_call(
        body, name=name, grid=(K, R // tr),
        in_specs=[spec, spec], out_specs=spec,
        out_shape=jax.ShapeDtypeStruct((K, R, C), F32),
        compiler_params=_params(("parallel", "parallel")),
    )(a, b)


ANY = pl.BlockSpec(memory_space=pl.ANY)


def _chip_exchange(srcs, *, gather, name):
    n = len(srcs)
    outs = [jax.ShapeDtypeStruct((N_CHIPS,) + (s.shape if gather else s.shape[1:]), s.dtype) for s in srcs]

    def body(*refs):
        src_refs, out_refs = refs[:n], refs[n:2 * n]
        send_sems, recv_sems, local_sems = refs[2 * n:]
        x, y, c = lax.axis_index("x"), lax.axis_index("y"), lax.axis_index("c")
        me = 2 * x + y
        copies = []
        for t in range(n):
            own = src_refs[t] if gather else src_refs[t].at[me]
            cp = pltpu.make_async_copy(own, out_refs[t].at[me], local_sems.at[t])
            cp.start()
            copies.append(cp)
        for t in range(n):
            for d in (1, 2, 3):
                tx = 1 - x if d & 2 else x
                ty = 1 - y if d & 1 else y
                src = src_refs[t] if gather else src_refs[t].at[2 * tx + ty]
                cp = pltpu.make_async_remote_copy(
                    src_ref=src, dst_ref=out_refs[t].at[me],
                    send_sem=send_sems.at[3 * t + d - 1], recv_sem=recv_sems.at[3 * t + d - 1],
                    device_id=(tx, ty, c), device_id_type=MESH)
                cp.start()
                copies.append(cp)
        for cp in copies:
            cp.wait()

    return pl.pallas_call(
        body, name=name,
        in_specs=[ANY] * n, out_specs=[ANY] * n, out_shape=outs,
        scratch_shapes=[pltpu.SemaphoreType.DMA((3 * n,)), pltpu.SemaphoreType.DMA((3 * n,)),
                        pltpu.SemaphoreType.DMA((n,))],
    )(*srcs)


def _sibling_swap(srcs, *, name):
    n = len(srcs)

    def body(*refs):
        src_refs, out_refs = refs[:n], refs[n:2 * n]
        send_sems, recv_sems = refs[2 * n:]
        x, y, c = lax.axis_index("x"), lax.axis_index("y"), lax.axis_index("c")
        copies = []
        for t in range(n):
            cp = pltpu.make_async_remote_copy(
                src_ref=src_refs[t], dst_ref=out_refs[t],
                send_sem=send_sems.at[t], recv_sem=recv_sems.at[t],
                device_id=(x, y, 1 - c), device_id_type=MESH)
            cp.start()
            copies.append(cp)
        for cp in copies:
            cp.wait()

    return pl.pallas_call(
        body, name=name,
        in_specs=[ANY] * n, out_specs=[ANY] * n,
        out_shape=[jax.ShapeDtypeStruct(s.shape, s.dtype) for s in srcs],
        scratch_shapes=[pltpu.SemaphoreType.DMA((n,)), pltpu.SemaphoreType.DMA((n,))],
    )(*srcs)


def _allreduce_small(v, *, name):
    R, C = v.shape

    def body(v_ref, o_ref, buf_ref, send_sems, recv_sems):
        x, y, c = lax.axis_index("x"), lax.axis_index("y"), lax.axis_index("c")
        me = 4 * x + 2 * y + c
        buf_ref[me] = v_ref[...]
        copies = []
        for d in range(1, N_DEV):
            tx = 1 - x if d & 4 else x
            ty = 1 - y if d & 2 else y
            tc = 1 - c if d & 1 else c
            cp = pltpu.make_async_remote_copy(
                src_ref=v_ref, dst_ref=buf_ref.at[me],
                send_sem=send_sems.at[d - 1], recv_sem=recv_sems.at[d - 1],
                device_id=(tx, ty, tc), device_id_type=MESH)
            cp.start()
            copies.append(cp)
        for cp in copies:
            cp.wait()
        acc = buf_ref[0]
        for k in range(1, N_DEV):
            acc = acc + buf_ref[k]
        o_ref[...] = acc

    return pl.pallas_call(
        body, name=name,
        in_specs=[pl.BlockSpec(memory_space=pltpu.VMEM)],
        out_specs=pl.BlockSpec(memory_space=pltpu.VMEM),
        out_shape=jax.ShapeDtypeStruct((R, C), F32),
        scratch_shapes=[pltpu.VMEM((N_DEV, R, C), F32), pltpu.SemaphoreType.DMA((N_DEV - 1,)),
                        pltpu.SemaphoreType.DMA((N_DEV - 1,))],
    )(v)


_A0, _B0, _GC0 = 0, CONV_W, 2 * CONV_W
_Q0 = 3 * CONV_W
_FL0 = _Q0 + 3 * FOX_W
_FG0 = _FL0 + N_FOX
_MQ0 = _FG0 + FOX_W
_MG0 = _MQ0 + MEM_W
_DIN = _MG0 + MEM_W
_WMAIN = _DIN - N_FOX
_PC_N = 3 * CONV_W
_QKV_N = 3 * FOX_W
_PG_N = FOX_W + 2 * MEM_W


def _rows_pad8(a):
    r = a.shape[0]
    p = (-r) % 8
    return jnp.pad(a, ((0, p), (0, 0))) if p else a


def _local_step(x, mem, target, norm_g, mem_norm_g, final_g, b_f, conv_w, conv_b, ln_g, ln_b,
                w_in, w_pw, w_mkv, w_out):
    S, D = x.shape
    M = mem.shape[0]
    tq = min(512, S)
    tt = min(256, S)
    tm = min(256, S)

    w_main = jnp.concatenate([w_in[:, :_FL0], w_in[:, _FG0:]], axis=1)
    w_fl = jnp.pad(w_in[:, _FL0:_FG0], ((0, 0), (0, LANE - N_FOX)))
    bf_pad = jnp.pad(b_f, ((0, 0), (0, LANE - N_FOX)))
    cw_pad = jnp.pad(conv_w, ((0, CONV_HALO - CONV_K), (0, 0)))

    h = _rms_fwd(x, norm_g, tm=tm, name="rms_fwd")
    pc = _mm(h, w_main, tm=512, tn=512, tk=2048, n=_PC_N, b_off=0, name="proj_conv")
    qkv = _mm(h, w_main, out_dtype=BF16, tm=512, tn=512, tk=2048, n=_QKV_N, b_off=_PC_N // 512, name="proj_qkv")
    pg = _mm(h, w_main, tm=512, tn=512, tk=2048, n=_PG_N, b_off=(_PC_N + _QKV_N) // 512, name="proj_gate")
    pfl = _mm(h, w_fl, tm=512, tn=LANE, tk=2048, name="proj_logit")

    y_conv = _conv_fwd(pc, cw_pad, conv_b, ln_g, ln_b, w_pw, tt=tt, name="conv_fwd")

    cum = _fox_cumsum(pfl, bf_pad, tc=tt, name="fox_cumsum")
    c_t = cum[:, :N_FOX].T
    c_col = c_t.reshape(N_FOX, S, 1)
    c_row = c_t.reshape(N_FOX, S // tq, 1, tq)
    y_fox, o_fox, lse = _fox_fwd(qkv, pg, c_col, c_row, tq=tq, name="fox_fwd")

    hm = _rms_fwd(mem, mem_norm_g, tm=min(256, M), name="rms_mem")
    mkv = _mm(hm, w_mkv, tm=256, tn=512, tk=2048, name="mem_kv")
    y_mem = _mem_fwd(pg, mkv, tq=tq, name="mem_fwd")

    y = jnp.concatenate([y_conv, y_fox, y_mem], axis=1)
    x2 = _mm(y, w_out, add=x, tm=512, tn=512, tk=2048, name="out_proj")
    dx2, dfg, sq = _final(x2, target, final_g.reshape(1, D), tm=tm, name="final")

    dy = _mm(dx2, w_out, tb=True, tm=512, tn=512, tk=2048, name="d_y")
    dw_out = _mm(y, dx2, ta=True, tm=512, tn=512, tk=1024, name="d_w_out")

    dpc, dw_pw, dcw, dsm = _conv_bwd(pc, dy, cw_pad, conv_b, ln_g, ln_b, w_pw, tt=tt, name="conv_bwd")

    do, dfgate, delta0 = _fox_bwd_prep(dy, o_fox, pg, tq=tq, name="fox_bwd_prep")
    dq, delta = _fox_bwd_dq(qkv, do, c_col, c_row, lse, delta0, tq=tq, name="fox_bwd_dq")
    rowv = lambda a: a.reshape(N_FOX, S // tq, 1, tq)
    dk, dv, dc = _fox_bwd_dkv(qkv, do, c_col, c_row, rowv(lse), rowv(delta), tq=tq, name="fox_bwd_dkv")
    dc_pad = jnp.pad(dc.reshape(N_FOX, S).T, ((0, 0), (0, LANE - N_FOX)))
    dfl, dbf = _fox_dlogit(dc_pad, pfl, bf_pad, tc=tt, name="fox_dlogit")

    dmq, dmgate, dmkv = _mem_bwd(pg, mkv, dy, tq=tq, name="mem_bwd")
    dw_mkv = _mm(hm, dmkv, ta=True, tm=512, tn=512, tk=256, name="d_w_mkv")
    dhm = _mm(dmkv, w_mkv, tb=True, tm=256, tn=512, tk=1024, name="d_hm")
    _, dmg = _rms_bwd(mem, mem_norm_g, dhm, None, tm=min(256, M), name="rms_mem_bwd")

    dproj = jnp.concatenate([dpc, dq, dk, dv, dfgate, dmq, dmgate], axis=1)
    dw_main = _mm(h, dproj, ta=True, tm=512, tn=512, tk=1024, name="d_w_main")
    dw_fl = _mm(h, dfl, ta=True, tm=512, tn=LANE, tk=1024, name="d_w_logit")
    dh0 = _mm(dfl, w_fl, tb=True, tm=512, tn=512, tk=LANE, name="d_h_logit")
    dh = _mm(dproj, w_main, tb=True, add=dh0, tm=512, tn=512, tk=_WMAIN // 4, name="d_h")
    grad_x, dng = _rms_bwd(x, norm_g, dh, dx2, tm=tm, name="rms_bwd")

    dw_in = jnp.concatenate([dw_main[:, :_FL0], dw_fl[:, :N_FOX], dw_main[:, _FL0:]], axis=1)
    small = dict(norm_g=dng, mem_norm_g=dmg, final_g=dfg, b_f=dbf[0:1, :], conv_w=dcw,
                 conv_b=dsm[0:1], conv_ln_g=dsm[1:2], conv_ln_b=dsm[2:3])
    return sq[0, 0], grad_x, dw_in, dw_pw, dw_mkv, dw_out, small


_SMALL_ORDER = ("norm_g", "mem_norm_g", "final_g", "b_f", "conv_w", "conv_b", "conv_ln_g", "conv_ln_b")


def _pack_small(small):
    parts, layout = [], []
    row = 0
    for k in _SMALL_ORDER:
        p = _rows_pad8(small[k].reshape(-1, LANE))
        layout.append((k, row, small[k].shape))
        parts.append(p)
        row += p.shape[0]
    return jnp.concatenate(parts, axis=0), layout


def _unpack_small(packed, layout):
    out = {}
    for k, row, shape in layout:
        nrow = (shape[0] * shape[1]) // LANE
        out[k] = packed[row:row + nrow].reshape(shape)
    return out


def _reduce_shards(grads, c):
    halves = [g.reshape(N_CHIPS, 2, g.shape[1] // 2, g.shape[2]) for g in grads]
    mine = [lax.dynamic_index_in_dim(hv, c, axis=1, keepdims=False) for hv in halves]
    theirs = [lax.dynamic_index_in_dim(hv, 1 - c, axis=1, keepdims=False) for hv in halves]
    got = _sibling_swap(theirs, name="grad_swap_halves")
    chip = [_add2(a, b, name=f"grad_add_sibling_{t}") for t, (a, b) in enumerate(zip(mine, got))]
    parts = _chip_exchange(chip, gather=False, name="grad_chip_scatter")
    red = [_sum4(p, name=f"grad_sum_chips_{t}") for t, p in enumerate(parts)]
    other = _sibling_swap(red, name="grad_swap_result")
    full = []
    for r, o in zip(red, other):
        lo = jnp.where(c == 0, r, o)
        hi = jnp.where(c == 0, o, r)
        full.append(jnp.concatenate([lo, hi], axis=0))
    return full


def kernel(x, mem, norm_g, mem_norm_g, w_in, b_f, conv_w, conv_b, conv_ln_g, conv_ln_b, w_conv_pw, w_mem_kv, w_out, final_g, loss_target, m_norm_g, m_mem_norm_g, m_w_in, m_b_f, m_conv_w, m_conv_b, m_conv_ln_g, m_conv_ln_b, m_w_conv_pw, m_w_mem_kv, m_w_out, m_final_g, v_norm_g, v_mem_norm_g, v_w_in, v_b_f, v_conv_w, v_conv_b, v_conv_ln_g, v_conv_ln_b, v_w_conv_pw, v_w_mem_kv, v_w_out, v_final_g):
    S, D = x.shape[1], x.shape[2]
    xi, yi, ci = lax.axis_index("x"), lax.axis_index("y"), lax.axis_index("c")
    chip = 2 * xi + yi

    g_in, g_pw, g_mkv, g_out, g_cw = _chip_exchange(
        [w_in[0].astype(BF16), w_conv_pw[0].astype(BF16), w_mem_kv[0].astype(BF16), w_out[0].astype(BF16),
         conv_w[0]], gather=True, name="weight_all_gather")
    nin = w_in.shape[2]
    w_in_full = jnp.transpose(g_in, (1, 0, 2)).reshape(D, N_CHIPS * nin)
    w_pw_full = g_pw.reshape(CONV_W, CONV_W)
    w_mkv_full = g_mkv.reshape(D, 2 * MEM_W)
    w_out_full = g_out.reshape(MIX_W, D)
    conv_w_full = jnp.transpose(g_cw, (1, 0, 2)).reshape(CONV_K, CONV_W)

    sq, grad_x, dw_in, dw_pw, dw_mkv, dw_out, small = _local_step(
        x[0], mem[0], loss_target[0], norm_g, mem_norm_g, final_g, b_f, conv_w_full, conv_b, conv_ln_g,
        conv_ln_b, w_in_full, w_pw_full, w_mkv_full, w_out_full)

    loss = lax.psum(sq, ("x", "y", "c")) * (0.5 / D)

    big = [jnp.transpose(dw_in.reshape(D, N_CHIPS, nin), (1, 0, 2)),
           dw_pw.reshape(N_CHIPS, CONV_W // N_CHIPS, CONV_W),
           dw_mkv.reshape(N_CHIPS, D // N_CHIPS, 2 * MEM_W),
           dw_out.reshape(N_CHIPS, MIX_W // N_CHIPS, D)]
    g_w_in, g_w_pw, g_w_mkv, g_w_out = _reduce_shards(big, ci)

    packed, layout = _pack_small(small)
    sm = _unpack_small(_allreduce_small(packed, name="small_all_reduce"), layout)
    cshard = CONV_W // N_CHIPS
    g_conv_w = lax.dynamic_slice_in_dim(sm["conv_w"][:CONV_K], chip * cshard, cshard, axis=1)

    grads = dict(
        norm_g=sm["norm_g"], mem_norm_g=sm["mem_norm_g"], w_in=g_w_in[None], b_f=sm["b_f"][:, :N_FOX],
        conv_w=g_conv_w[None], conv_b=sm["conv_b"], conv_ln_g=sm["conv_ln_g"], conv_ln_b=sm["conv_ln_b"],
        w_conv_pw=g_w_pw[None], w_mem_kv=g_w_mkv[None], w_out=g_w_out[None], final_g=sm["final_g"].reshape(D))
    weights = dict(norm_g=norm_g, mem_norm_g=mem_norm_g, w_in=w_in, b_f=b_f, conv_w=conv_w, conv_b=conv_b,
                   conv_ln_g=conv_ln_g, conv_ln_b=conv_ln_b, w_conv_pw=w_conv_pw, w_mem_kv=w_mem_kv, w_out=w_out,
                   final_g=final_g)
    ms = dict(norm_g=m_norm_g, mem_norm_g=m_mem_norm_g, w_in=m_w_in, b_f=m_b_f, conv_w=m_conv_w, conv_b=m_conv_b,
              conv_ln_g=m_conv_ln_g, conv_ln_b=m_conv_ln_b, w_conv_pw=m_w_conv_pw, w_mem_kv=m_w_mem_kv,
              w_out=m_w_out, final_g=m_final_g)
    vs = dict(norm_g=v_norm_g, mem_norm_g=v_mem_norm_g, w_in=v_w_in, b_f=v_b_f, conv_w=v_conv_w, conv_b=v_conv_b,
              conv_ln_g=v_conv_ln_g, conv_ln_b=v_conv_ln_b, w_conv_pw=v_w_conv_pw, w_mem_kv=v_w_mem_kv,
              w_out=v_w_out, final_g=v_final_g)

    names = ("norm_g", "mem_norm_g", "w_in", "b_f", "conv_w", "conv_b", "conv_ln_g", "conv_ln_b", "w_conv_pw",
             "w_mem_kv", "w_out", "final_g")
    deltas, new_m, new_v = {}, {}, {}
    for k in names:
        shape = weights[k].shape
        two_d = (lambda a: a.reshape(-1, shape[-1]))
        d, nm, nv = _adamw(two_d(weights[k]), two_d(grads[k]), two_d(ms[k]), two_d(vs[k]), name=f"adamw_{k}")
        deltas[k], new_m[k], new_v[k] = d.reshape(shape), nm.reshape(shape), nv.reshape(shape)

    return (loss, grad_x[None], *[grads[k] for k in names], *[deltas[k] for k in names],
            *[new_m[k] for k in names], *[new_v[k] for k in names])
```

```python
import functools

import jax
import jax.numpy as jnp
from jax import lax
from jax.experimental import pallas as pl
from jax.experimental.pallas import tpu as pltpu

F32 = jnp.float32
BF16 = jnp.bfloat16
MESH = pl.DeviceIdType.MESH

HEAD = 128
N_FOX = 8
N_MEMH = 4
CONV_W = 512
FOX_W = N_FOX * HEAD
MEM_W = N_MEMH * HEAD
MIX_W = CONV_W + FOX_W + MEM_W
CONV_K = 31
CONV_HALO = 32
EPS = 1e-6
NEG = -1e30
LANE = 128
N_CHIPS = 4
N_DEV = 8
VMEM_LIMIT = 56 * 1024 * 1024

ADAM_LR = 0.001
ADAM_B1 = 0.9
ADAM_B2 = 0.999
ADAM_EPS = 1e-08
ADAM_WD = 0.01
ADAM_STEP = 10

NT = (((1,), (1,)), ((), ()))
TN = (((0,), (0,)), ((), ()))
NN = (((1,), (0,)), ((), ()))


def _params(sem=None):
    kw = dict(vmem_limit_bytes=VMEM_LIMIT)
    if sem is not None:
        kw["dimension_semantics"] = sem
    return pltpu.CompilerParams(**kw)


def _sigmoid(v):
    return jax.nn.sigmoid(v)


def _dsilu(v, s):
    return s * (1.0 + v * (1.0 - s))


def _rms_fwd(x, g, *, tm, name):
    R, D = x.shape

    def body(x_ref, g_ref, h_ref):
        xv = x_ref[...]
        r = lax.rsqrt(jnp.mean(xv * xv, axis=-1, keepdims=True) + EPS)
        h_ref[...] = (xv * r * g_ref[...]).astype(BF16)

    return pl.pallas_call(
        body, name=name, grid=(R // tm,),
        in_specs=[pl.BlockSpec((tm, D), lambda i: (i, 0)), pl.BlockSpec((1, D), lambda i: (0, 0))],
        out_specs=pl.BlockSpec((tm, D), lambda i: (i, 0)),
        out_shape=jax.ShapeDtypeStruct((R, D), BF16),
        compiler_params=_params(("parallel",)),
    )(x, g)


def _rms_bwd(x, g, dh, dres, *, tm, name):
    R, D = x.shape
    has_res = dres is not None

    def body(*refs):
        if has_res:
            x_ref, g_ref, dh_ref, dres_ref, dx_ref, dg_ref = refs
        else:
            x_ref, g_ref, dh_ref, dx_ref, dg_ref = refs
        i = pl.program_id(0)

        @pl.when(i == 0)
        def _():
            dg_ref[...] = jnp.zeros_like(dg_ref)

        xv = x_ref[...]
        r = lax.rsqrt(jnp.mean(xv * xv, axis=-1, keepdims=True) + EPS)
        n = xv * r
        dh = dh_ref[...]
        dg_ref[...] += jnp.sum(dh * n, axis=0, keepdims=True)
        dn = dh * g_ref[...]
        dx = r * (dn - n * jnp.mean(dn * n, axis=-1, keepdims=True))
        if has_res:
            dx = dx + dres_ref[...]
        dx_ref[...] = dx

    row = pl.BlockSpec((tm, D), lambda i: (i, 0))
    vec = pl.BlockSpec((1, D), lambda i: (0, 0))
    ins = [row, vec, row] + ([row] if has_res else [])
    args = (x, g, dh) + ((dres,) if has_res else ())
    return pl.pallas_call(
        body, name=name, grid=(R // tm,),
        in_specs=ins, out_specs=[row, vec],
        out_shape=[jax.ShapeDtypeStruct((R, D), F32), jax.ShapeDtypeStruct((1, D), F32)],
        compiler_params=_params(("arbitrary",)),
    )(*args)


def _mm(a, b, *, ta=False, tb=False, out_dtype=F32, add=None, tm, tn, tk, n=None, b_off=0, name):
    M, K = (a.shape[1], a.shape[0]) if ta else a.shape
    nb = b.shape[0] if tb else b.shape[1]
    n = nb if n is None else n
    tm, tn, tk = min(tm, M), min(tn, n), min(tk, K)
    assert M % tm == 0 and n % tn == 0 and K % tk == 0, (name, M, n, K, tm, tn, tk)
    nk = K // tk
    has_add = add is not None

    def body(*refs):
        if has_add:
            a_ref, b_ref, add_ref, o_ref, acc_ref = refs
        else:
            a_ref, b_ref, o_ref, acc_ref = refs
        k = pl.program_id(2)

        @pl.when(k == 0)
        def _():
            acc_ref[...] = jnp.zeros_like(acc_ref)

        av = a_ref[...].astype(BF16)
        bv = b_ref[...].astype(BF16)
        dims = (((0 if ta else 1,), (1 if tb else 0,)), ((), ()))
        acc_ref[...] += lax.dot_general(av, bv, dims, preferred_element_type=F32)

        @pl.when(k == nk - 1)
        def _():
            r = acc_ref[...]
            if has_add:
                r = r + add_ref[...]
            o_ref[...] = r.astype(out_dtype)

    a_spec = pl.BlockSpec((tk, tm), lambda i, j, k: (k, i)) if ta else pl.BlockSpec((tm, tk), lambda i, j, k: (i, k))
    b_spec = (pl.BlockSpec((tn, tk), lambda i, j, k: (j + b_off, k)) if tb
              else pl.BlockSpec((tk, tn), lambda i, j, k: (k, j + b_off)))
    o_spec = pl.BlockSpec((tm, tn), lambda i, j, k: (i, j))
    ins = [a_spec, b_spec] + ([o_spec] if has_add else [])
    args = (a, b) + ((add,) if has_add else ())
    return pl.pallas_call(
        body, name=name, grid=(M // tm, n // tn, nk),
        in_specs=ins, out_specs=o_spec,
        out_shape=jax.ShapeDtypeStruct((M, n), out_dtype),
        scratch_shapes=[pltpu.VMEM((tm, tn), F32)],
        compiler_params=_params(("parallel", "parallel", "arbitrary")),
    )(*args)


def _conv_taps(w_ref, e_ref, first, tt):
    acc = w_ref[0:1, :] * e_ref[first:first + tt, :]
    for k in range(1, CONV_K):
        acc = acc + w_ref[k:k + 1, :] * e_ref[first + k:first + k + tt, :]
    return acc


def _conv_fwd(pc, cw, cb, lng, lnb, wpw, *, tt, name):
    S = pc.shape[0]
    C = CONV_W
    lead = CONV_HALO - (CONV_K - 1)

    def body(a_ref, b_ref, gc_ref, cw_ref, cb_ref, lng_ref, lnb_ref, wpw_ref, y_ref, e_ref):
        i = pl.program_id(0)

        @pl.when(i == 0)
        def _():
            e_ref[0:CONV_HALO, :] = jnp.zeros((CONV_HALO, C), F32)

        @pl.when(i > 0)
        def _():
            e_ref[0:CONV_HALO, :] = e_ref[tt:tt + CONV_HALO, :]

        e_ref[CONV_HALO:CONV_HALO + tt, :] = a_ref[...] * _sigmoid(b_ref[...])
        u1 = _conv_taps(cw_ref, e_ref, lead, tt) + cb_ref[...]
        mu = jnp.mean(u1, axis=-1, keepdims=True)
        xc = u1 - mu
        rstd = lax.rsqrt(jnp.mean(xc * xc, axis=-1, keepdims=True) + EPS)
        u2 = xc * rstd * lng_ref[...] + lnb_ref[...]
        u3 = u2 * _sigmoid(u2)
        z = jnp.dot(u3.astype(BF16), wpw_ref[...], preferred_element_type=F32)
        gc = gc_ref[...]
        y_ref[...] = (z * gc * _sigmoid(gc)).astype(BF16)

    col = lambda c: pl.BlockSpec((tt, C), lambda i, c=c: (i, c))
    vec = pl.BlockSpec((1, C), lambda i: (0, 0))
    return pl.pallas_call(
        body, name=name, grid=(S // tt,),
        in_specs=[col(0), col(1), col(2), pl.BlockSpec((CONV_HALO, C), lambda i: (0, 0)), vec, vec, vec,
                  pl.BlockSpec((C, C), lambda i: (0, 0))],
        out_specs=pl.BlockSpec((tt, C), lambda i: (i, 0)),
        out_shape=jax.ShapeDtypeStruct((S, C), BF16),
        scratch_shapes=[pltpu.VMEM((tt + CONV_HALO, C), F32)],
        compiler_params=_params(("arbitrary",)),
    )(pc, pc, pc, cw, cb, lng, lnb, wpw)


def _conv_bwd(pc, dy, cw, cb, lng, lnb, wpw, *, tt, name):
    S = pc.shape[0]
    C = CONV_W
    nt = S // tt
    hb = tt // CONV_HALO
    lead = CONV_HALO - (CONV_K - 1)

    def body(a_ref, b_ref, gc_ref, ah_ref, bh_ref, dy_ref, cw_ref, cb_ref, lng_ref, lnb_ref, wpw_ref,
             dpc_ref, dwpw_ref, dcw_ref, dsm_ref, eu_ref, ed_ref):
        i = pl.program_id(0)
        ti = nt - 1 - i

        @pl.when(i == 0)
        def _():
            ed_ref[tt:tt + CONV_HALO, :] = jnp.zeros((CONV_HALO, C), F32)
            dwpw_ref[...] = jnp.zeros_like(dwpw_ref)
            dcw_ref[...] = jnp.zeros_like(dcw_ref)
            dsm_ref[...] = jnp.zeros_like(dsm_ref)

        @pl.when(i > 0)
        def _():
            ed_ref[tt:tt + CONV_HALO, :] = ed_ref[0:CONV_HALO, :]

        keep = jnp.where(ti > 0, 1.0, 0.0).astype(F32)
        eu_ref[0:CONV_HALO, :] = ah_ref[...] * _sigmoid(bh_ref[...]) * keep
        a = a_ref[...]
        sb = _sigmoid(b_ref[...])
        eu_ref[CONV_HALO:CONV_HALO + tt, :] = a * sb

        u1 = _conv_taps(cw_ref, eu_ref, lead, tt) + cb_ref[...]
        mu = jnp.mean(u1, axis=-1, keepdims=True)
        xc = u1 - mu
        rstd = lax.rsqrt(jnp.mean(xc * xc, axis=-1, keepdims=True) + EPS)
        nhat = xc * rstd
        g = lng_ref[...]
        u2 = nhat * g + lnb_ref[...]
        s2 = _sigmoid(u2)
        u3 = (u2 * s2).astype(BF16)
        z = jnp.dot(u3, wpw_ref[...], preferred_element_type=F32)

        gc = gc_ref[...]
        sg = _sigmoid(gc)
        dyv = dy_ref[...]
        dz = (dyv * gc * sg).astype(BF16)
        dpc_ref[:, 2 * C:3 * C] = (dyv * z * _dsilu(gc, sg)).astype(BF16)

        du3 = lax.dot_general(dz, wpw_ref[...], NT, preferred_element_type=F32)
        dwpw_ref[...] += lax.dot_general(u3, dz, TN, preferred_element_type=F32)
        du2 = du3 * _dsilu(u2, s2)
        dsm_ref[1:2, :] += jnp.sum(du2 * nhat, axis=0, keepdims=True)
        dsm_ref[2:3, :] += jnp.sum(du2, axis=0, keepdims=True)
        dn = du2 * g
        du1 = rstd * (dn - jnp.mean(dn, axis=-1, keepdims=True)
                      - nhat * jnp.mean(dn * nhat, axis=-1, keepdims=True))
        dsm_ref[0:1, :] += jnp.sum(du1, axis=0, keepdims=True)
        ed_ref[0:tt, :] = du1

        du0 = cw_ref[0:1, :] * ed_ref[CONV_K - 1:CONV_K - 1 + tt, :]
        for k in range(1, CONV_K):
            o = CONV_K - 1 - k
            du0 = du0 + cw_ref[k:k + 1, :] * ed_ref[o:o + tt, :]
        for k in range(CONV_K):
            dcw_ref[k:k + 1, :] += jnp.sum(du1 * eu_ref[lead + k:lead + k + tt, :], axis=0, keepdims=True)

        dpc_ref[:, 0:C] = (du0 * sb).astype(BF16)
        dpc_ref[:, C:2 * C] = (du0 * a * sb * (1.0 - sb)).astype(BF16)

    col = lambda c: pl.BlockSpec((tt, C), lambda i, c=c: (nt - 1 - i, c))
    halo = lambda c: pl.BlockSpec((CONV_HALO, C), lambda i, c=c: (jnp.maximum((nt - 1 - i) * hb - 1, 0), c))
    vec = pl.BlockSpec((1, C), lambda i: (0, 0))
    fixed = lambda r: pl.BlockSpec((r, C), lambda i: (0, 0))
    return pl.pallas_call(
        body, name=name, grid=(nt,),
        in_specs=[col(0), col(1), col(2), halo(0), halo(1),
                  pl.BlockSpec((tt, C), lambda i: (nt - 1 - i, 0)),
                  fixed(CONV_HALO), vec, vec, vec, fixed(C)],
        out_specs=[pl.BlockSpec((tt, 3 * C), lambda i: (nt - 1 - i, 0)), fixed(C), fixed(CONV_HALO), fixed(8)],
        out_shape=[jax.ShapeDtypeStruct((S, 3 * C), BF16), jax.ShapeDtypeStruct((C, C), F32),
                   jax.ShapeDtypeStruct((CONV_HALO, C), F32), jax.ShapeDtypeStruct((8, C), F32)],
        scratch_shapes=[pltpu.VMEM((tt + CONV_HALO, C), F32), pltpu.VMEM((tt + CONV_HALO, C), F32)],
        compiler_params=_params(("arbitrary",)),
    )(pc, pc, pc, pc, pc, dy, cw, cb, lng, lnb, wpw)


def _tri(n, lower):
    r = lax.broadcasted_iota(jnp.int32, (n, n), 0)
    c = lax.broadcasted_iota(jnp.int32, (n, n), 1)
    return jnp.where((r >= c) if lower else (r <= c), 1.0, 0.0).astype(F32)


def _fox_cumsum(pfl, bf, *, tc, name):
    S = pfl.shape[0]

    def body(fl_ref, bf_ref, c_ref, carry_ref):
        i = pl.program_id(0)

        @pl.when(i == 0)
        def _():
            carry_ref[...] = jnp.zeros_like(carry_ref)

        z = fl_ref[...] + bf_ref[...]
        logf = jnp.minimum(z, 0.0) - jnp.log1p(jnp.exp(-jnp.abs(z)))
        c = jnp.dot(_tri(tc, True), logf, precision=lax.Precision.HIGHEST,
                    preferred_element_type=F32) + carry_ref[0:1, :]
        c_ref[...] = c
        carry_ref[0:1, :] = c[tc - 1:tc, :]

    return pl.pallas_call(
        body, name=name, grid=(S // tc,),
        in_specs=[pl.BlockSpec((tc, LANE), lambda i: (i, 0)), pl.BlockSpec((1, LANE), lambda i: (0, 0))],
        out_specs=pl.BlockSpec((tc, LANE), lambda i: (i, 0)),
        out_shape=jax.ShapeDtypeStruct((S, LANE), F32),
        scratch_shapes=[pltpu.VMEM((8, LANE), F32)],
        compiler_params=_params(("arbitrary",)),
    )(pfl, bf)


def _fox_dlogit(dc, pfl, bf, *, tc, name):
    S = pfl.shape[0]
    nt = S // tc

    def body(dc_ref, fl_ref, bf_ref, dfl_ref, dbf_ref, carry_ref):
        i = pl.program_id(0)

        @pl.when(i == 0)
        def _():
            carry_ref[...] = jnp.zeros_like(carry_ref)
            dbf_ref[...] = jnp.zeros_like(dbf_ref)

        dlogf = jnp.dot(_tri(tc, False), dc_ref[...], precision=lax.Precision.HIGHEST,
                        preferred_element_type=F32) + carry_ref[0:1, :]
        carry_ref[0:1, :] = dlogf[0:1, :]
        dz = dlogf * _sigmoid(-(fl_ref[...] + bf_ref[...]))
        dfl_ref[...] = dz.astype(BF16)
        dbf_ref[0:1, :] += jnp.sum(dz, axis=0, keepdims=True)

    rev = pl.BlockSpec((tc, LANE), lambda i: (nt - 1 - i, 0))
    return pl.pallas_call(
        body, name=name, grid=(nt,),
        in_specs=[rev, rev, pl.BlockSpec((1, LANE), lambda i: (0, 0))],
        out_specs=[rev, pl.BlockSpec((8, LANE), lambda i: (0, 0))],
        out_shape=[jax.ShapeDtypeStruct((S, LANE), BF16), jax.ShapeDtypeStruct((8, LANE), F32)],
        scratch_shapes=[pltpu.VMEM((8, LANE), F32)],
        compiler_params=_params(("arbitrary",)),
    )(dc, pfl, bf)


def _causal_part(rows, cols, row0, col0, q_is_row=True):
    r = lax.broadcasted_iota(jnp.int32, (rows, cols), 0) + row0
    c = lax.broadcasted_iota(jnp.int32, (rows, cols), 1) + col0
    return (r >= c) if q_is_row else (c >= r)


LOG2E = 1.4426950408889634
FOX_SCALE2 = (HEAD ** -0.5) * LOG2E


def _fox_bias_operands(c):
    S = c.shape[0]
    cs = (c * (HEAD ** 0.5)).T
    as_bf16 = lambda v: lax.reduce_precision(v, exponent_bits=8, mantissa_bits=7)
    hi = as_bf16(cs)
    r1 = cs - hi
    mid = as_bf16(r1)
    lo = r1 - mid
    pieces = jnp.stack([hi, mid, lo], axis=-1).astype(BF16)
    ones = jnp.ones_like(pieces)
    zeros = jnp.zeros((N_FOX, S, HEAD - 6), BF16)
    qa = jnp.concatenate([pieces, ones, zeros], axis=-1)
    ka = jnp.concatenate([ones, -pieces, zeros], axis=-1)
    return qa, ka


def _fox_fwd(qkv, pg, qa, ka, *, tq, nsub, name):
    S = qkv.shape[0]
    nq = S // tq
    rs = tq // nsub

    def body(q_ref, qa_ref, k_ref, ka_ref, v_ref, g_ref, y_ref, o_ref, lse_ref):
        i = pl.program_id(1)
        q = jnp.concatenate([q_ref[...], qa_ref[...]], axis=1)

        qs = [q[r * rs:(r + 1) * rs] for r in range(nsub)]

        def blk(j, carry, masked):
            off = pl.multiple_of(j * tq, tq)
            kj = jnp.concatenate([k_ref[pl.ds(off, tq), :], ka_ref[pl.ds(off, tq), :]], axis=1)
            vj = v_ref[pl.ds(off, tq), :]
            out = []
            for r in range(nsub):
                m, l, acc = carry[r]
                s = lax.dot_general(qs[r], kj, NT, preferred_element_type=F32) * FOX_SCALE2
                if masked:
                    s = jnp.where(_causal_part(rs, tq, r * rs, 0), s, NEG)
                m_new = jnp.maximum(m, jnp.max(s, axis=-1, keepdims=True))
                alpha = jnp.exp2(m - m_new)
                p = jnp.exp2(s - m_new)
                l = alpha * l + jnp.sum(p, axis=-1, keepdims=True)
                p_hi = p.astype(BF16)
                p_lo = (p - p_hi.astype(F32)).astype(BF16)
                acc = (alpha * acc + jnp.dot(p_hi, vj, preferred_element_type=F32)
                       + jnp.dot(p_lo, vj, preferred_element_type=F32))
                out.append((m_new, l, acc))
            return tuple(out)

        init = tuple((jnp.full((rs, 1), NEG, F32), jnp.zeros((rs, 1), F32), jnp.zeros((rs, HEAD), F32))
                     for _ in range(nsub))
        carry = lax.fori_loop(0, i, lambda j, c: blk(j, c, False), init)
        carry = blk(i, carry, True)
        m = jnp.concatenate([c[0] for c in carry], axis=0)
        l = jnp.concatenate([c[1] for c in carry], axis=0)
        acc = jnp.concatenate([c[2] for c in carry], axis=0)
        o = acc / l
        g = g_ref[...]
        y_ref[...] = (o * g * _sigmoid(g)).astype(BF16)
        o_ref[...] = o
        lse_ref[...] = m + jnp.log2(l)

    tile = lambda base: pl.BlockSpec((tq, HEAD), lambda h, i, base=base: (i, base + h))
    full = lambda base: pl.BlockSpec((S, HEAD), lambda h, i, base=base: (0, base + h))
    colv = pl.BlockSpec((None, tq, 1), lambda h, i: (h, i, 0))
    return pl.pallas_call(
        body, name=name, grid=(N_FOX, nq),
        in_specs=[tile(0), pl.BlockSpec((None, tq, HEAD), lambda h, i: (h, i, 0)),
                  full(N_FOX), pl.BlockSpec((None, S, HEAD), lambda h, i: (h, 0, 0)),
                  full(2 * N_FOX), tile(0)],
        out_specs=[tile(0), tile(0), colv],
        out_shape=[jax.ShapeDtypeStruct((S, FOX_W), BF16), jax.ShapeDtypeStruct((S, FOX_W), F32),
                   jax.ShapeDtypeStruct((N_FOX, S, 1), F32)],
        compiler_params=_params(("parallel", "arbitrary")),
    )(qkv, qa, qkv, ka, qkv, pg)


def _fox_bwd_prep(dy, o, pg, *, tq, name):
    S = o.shape[0]
    base = CONV_W // HEAD

    def body(dy_ref, o_ref, g_ref, do_ref, dg_ref, dl_ref):
        g = g_ref[...]
        sg = _sigmoid(g)
        dyv = dy_ref[...]
        ov = o_ref[...]
        dob = (dyv * g * sg).astype(BF16)
        do_ref[...] = dob
        dg_ref[...] = (dyv * ov * _dsilu(g, sg)).astype(BF16)
        dl_ref[...] = jnp.sum(dob.astype(F32) * ov, axis=-1, keepdims=True)

    tile = lambda b: pl.BlockSpec((tq, HEAD), lambda h, i, b=b: (i, b + h))
    return pl.pallas_call(
        body, name=name, grid=(N_FOX, S // tq),
        in_specs=[tile(base), tile(0), tile(0)],
        out_specs=[tile(0), tile(0), pl.BlockSpec((None, tq, 1), lambda h, i: (h, i, 0))],
        out_shape=[jax.ShapeDtypeStruct((S, FOX_W), BF16), jax.ShapeDtypeStruct((S, FOX_W), BF16),
                   jax.ShapeDtypeStruct((N_FOX, S, 1), F32)],
        compiler_params=_params(("parallel", "parallel")),
    )(dy, o, pg)


def _fox_bwd(qkv, do, qa, ka, lse_row, delta_row, *, tq, nsub, name):
    S = qkv.shape[0]
    nq = S // tq
    cs = tq // nsub
    scale = HEAD ** -0.5

    def body(k_ref, ka_ref, v_ref, q_ref, qa_ref, do_ref, lser_ref, dlr_ref, dq_ref, dk_ref, dv_ref, dc_ref):
        j = pl.program_id(1)

        @pl.when(j == 0)
        def _():
            dq_ref[...] = jnp.zeros_like(dq_ref)

        kj = k_ref[...]
        kja = jnp.concatenate([kj, ka_ref[...]], axis=1)
        vj = v_ref[...]

        def blk(i, carry, masked):
            dk, dv, dc = carry
            lse_i = lser_ref[i]
            dl_i = dlr_ref[i]
            for c in range(nsub):
                off = pl.multiple_of(i * tq + c * cs, cs)
                qi = q_ref[pl.ds(off, cs), :]
                qia = jnp.concatenate([qi, qa_ref[pl.ds(off, cs), :]], axis=1)
                doi = do_ref[pl.ds(off, cs), :]
                st = lax.dot_general(kja, qia, NT, preferred_element_type=F32) * FOX_SCALE2
                if masked:
                    st = jnp.where(_causal_part(tq, cs, 0, c * cs, q_is_row=False), st, NEG)
                pt = jnp.exp2(st - lse_i[:, c * cs:(c + 1) * cs])
                dv = dv + jnp.dot(pt.astype(BF16), doi, preferred_element_type=F32)
                dpt = lax.dot_general(vj, doi, NT, preferred_element_type=F32)
                dst = pt * (dpt - dl_i[:, c * cs:(c + 1) * cs])
                dsb = dst.astype(BF16)
                dk = dk + jnp.dot(dsb, qi, preferred_element_type=F32)
                dq_ref[pl.ds(off, cs), :] += lax.dot_general(dsb, kj, TN, preferred_element_type=F32) * scale
                dc = dc - jnp.sum(dst, axis=-1, keepdims=True)
            return dk, dv, dc

        init = (jnp.zeros((tq, HEAD), F32), jnp.zeros((tq, HEAD), F32), jnp.zeros((tq, 1), F32))
        carry = blk(j, init, True)
        dk, dv, dc = lax.fori_loop(j + 1, nq, lambda i, c: blk(i, c, False), carry)
        dk_ref[...] = (dk * scale).astype(BF16)
        dv_ref[...] = dv.astype(BF16)
        dc_ref[...] = dc

    tile = lambda base: pl.BlockSpec((tq, HEAD), lambda h, j, base=base: (j, base + h))
    full = lambda base: pl.BlockSpec((S, HEAD), lambda h, j, base=base: (0, base + h))
    atile = pl.BlockSpec((None, tq, HEAD), lambda h, j: (h, j, 0))
    afull = pl.BlockSpec((None, S, HEAD), lambda h, j: (h, 0, 0))
    colv = pl.BlockSpec((None, tq, 1), lambda h, j: (h, j, 0))
    rowv = pl.BlockSpec((None, nq, 1, tq), lambda h, j: (h, 0, 0, 0))
    return pl.pallas_call(
        body, name=name, grid=(N_FOX, nq),
        in_specs=[tile(N_FOX), atile, tile(2 * N_FOX), full(0), afull, full(0), rowv, rowv],
        out_specs=[full(0), tile(0), tile(0), colv],
        out_shape=[jax.ShapeDtypeStruct((S, FOX_W), F32), jax.ShapeDtypeStruct((S, FOX_W), BF16),
                   jax.ShapeDtypeStruct((S, FOX_W), BF16), jax.ShapeDtypeStruct((N_FOX, S, 1), F32)],
        compiler_params=_params(("arbitrary", "arbitrary")),
    )(qkv, ka, qkv, qkv, qa, do, lse_row, delta_row)


def _mem_heads(mq, mkv, h):
    lo = h * HEAD
    qh = mq[:, lo:lo + HEAD].astype(BF16)
    kh = mkv[:, lo:lo + HEAD].astype(BF16)
    vh = mkv[:, MEM_W + lo:MEM_W + lo + HEAD].astype(BF16)
    return qh, kh, vh


def _mem_softmax(qh, kh):
    s = lax.dot_general(qh, kh, NT, preferred_element_type=F32) * (HEAD ** -0.5)
    e = jnp.exp(s - jnp.max(s, axis=-1, keepdims=True))
    return e / jnp.sum(e, axis=-1, keepdims=True)


def _mem_fwd(pg, mkv, *, tq, name):
    S = pg.shape[0]
    M = mkv.shape[0]
    qb = FOX_W // MEM_W

    def body(mq_ref, g_ref, mkv_ref, y_ref):
        mq = mq_ref[...]
        mkvv = mkv_ref[...]
        for h in range(N_MEMH):
            qh, kh, vh = _mem_heads(mq, mkvv, h)
            p = _mem_softmax(qh, kh)
            o = jnp.dot(p.astype(BF16), vh, preferred_element_type=F32)
            g = g_ref[:, h * HEAD:(h + 1) * HEAD]
            y_ref[:, h * HEAD:(h + 1) * HEAD] = (o * g * _sigmoid(g)).astype(BF16)

    return pl.pallas_call(
        body, name=name, grid=(S // tq,),
        in_specs=[pl.BlockSpec((tq, MEM_W), lambda i: (i, qb)), pl.BlockSpec((tq, MEM_W), lambda i: (i, qb + 1)),
                  pl.BlockSpec((M, 2 * MEM_W), lambda i: (0, 0))],
        out_specs=pl.BlockSpec((tq, MEM_W), lambda i: (i, 0)),
        out_shape=jax.ShapeDtypeStruct((S, MEM_W), BF16),
        compiler_params=_params(("parallel",)),
    )(pg, pg, mkv)


def _mem_bwd(pg, mkv, dy, *, tq, name):
    S = pg.shape[0]
    M = mkv.shape[0]
    qb = FOX_W // MEM_W
    yb = (CONV_W + FOX_W) // MEM_W
    scale = HEAD ** -0.5

    def body(mq_ref, g_ref, mkv_ref, dy_ref, dmq_ref, dg_ref, dmkv_ref):
        i = pl.program_id(0)

        @pl.when(i == 0)
        def _():
            dmkv_ref[...] = jnp.zeros_like(dmkv_ref)

        mq = mq_ref[...]
        mkvv = mkv_ref[...]
        for h in range(N_MEMH):
            lo = h * HEAD
            qh, kh, vh = _mem_heads(mq, mkvv, h)
            p = _mem_softmax(qh, kh)
            o = jnp.dot(p.astype(BF16), vh, preferred_element_type=F32)
            g = g_ref[:, lo:lo + HEAD]
            sg = _sigmoid(g)
            dyh = dy_ref[:, lo:lo + HEAD]
            do = dyh * g * sg
            dg_ref[:, lo:lo + HEAD] = (dyh * o * _dsilu(g, sg)).astype(BF16)
            dob = do.astype(BF16)
            dp = lax.dot_general(dob, vh, NT, preferred_element_type=F32)
            ds = p * (dp - jnp.sum(do * o, axis=-1, keepdims=True))
            dsb = ds.astype(BF16)
            dmq_ref[:, lo:lo + HEAD] = (jnp.dot(dsb, kh, preferred_element_type=F32) * scale).astype(BF16)
            dmkv_ref[:, lo:lo + HEAD] += lax.dot_general(dsb, qh, TN, preferred_element_type=F32) * scale
            dmkv_ref[:, MEM_W + lo:MEM_W + lo + HEAD] += lax.dot_general(
                p.astype(BF16), dob, TN, preferred_element_type=F32)

    return pl.pallas_call(
        body, name=name, grid=(S // tq,),
        in_specs=[pl.BlockSpec((tq, MEM_W), lambda i: (i, qb)), pl.BlockSpec((tq, MEM_W), lambda i: (i, qb + 1)),
                  pl.BlockSpec((M, 2 * MEM_W), lambda i: (0, 0)), pl.BlockSpec((tq, MEM_W), lambda i: (i, yb))],
        out_specs=[pl.BlockSpec((tq, MEM_W), lambda i: (i, 0)), pl.BlockSpec((tq, MEM_W), lambda i: (i, 0)),
                   pl.BlockSpec((M, 2 * MEM_W), lambda i: (0, 0))],
        out_shape=[jax.ShapeDtypeStruct((S, MEM_W), BF16), jax.ShapeDtypeStruct((S, MEM_W), BF16),
                   jax.ShapeDtypeStruct((M, 2 * MEM_W), F32)],
        compiler_params=_params(("arbitrary",)),
    )(pg, pg, mkv, dy)


def _final(x2, target, fg, *, tm, name):
    S, D = x2.shape

    def body(x_ref, t_ref, g_ref, dx_ref, dg_ref, ls_ref):
        i = pl.program_id(0)

        @pl.when(i == 0)
        def _():
            dg_ref[...] = jnp.zeros_like(dg_ref)
            ls_ref[...] = jnp.zeros_like(ls_ref)

        xv = x_ref[...]
        r = lax.rsqrt(jnp.mean(xv * xv, axis=-1, keepdims=True) + EPS)
        n = xv * r
        g = g_ref[...]
        diff = n * g - t_ref[...]
        ls_ref[...] += jnp.sum(diff * diff)
        dout = diff * (1.0 / D)
        dg_ref[...] += jnp.sum(dout * n, axis=0, keepdims=True)
        dn = dout * g
        dx_ref[...] = r * (dn - n * jnp.mean(dn * n, axis=-1, keepdims=True))

    row = pl.BlockSpec((tm, D), lambda i: (i, 0))
    vec = pl.BlockSpec((1, D), lambda i: (0, 0))
    return pl.pallas_call(
        body, name=name, grid=(S // tm,),
        in_specs=[row, row, vec],
        out_specs=[row, vec, pl.BlockSpec((8, LANE), lambda i: (0, 0))],
        out_shape=[jax.ShapeDtypeStruct((S, D), F32), jax.ShapeDtypeStruct((1, D), F32),
                   jax.ShapeDtypeStruct((8, LANE), F32)],
        compiler_params=_params(("arbitrary",)),
    )(x2, target, fg)


def _adamw(w, g, m, v, *, name):
    R, C = w.shape
    tr = R
    for cand in (256, 128, 64, 32, 16, 8):
        if R % cand == 0 and R > cand:
            tr = cand
            break
    c1 = 1.0 - ADAM_B1 ** ADAM_STEP
    c2 = 1.0 - ADAM_B2 ** ADAM_STEP

    def body(w_ref, g_ref, m_ref, v_ref, d_ref, nm_ref, nv_ref):
        gv = g_ref[...]
        nm = ADAM_B1 * m_ref[...] + (1.0 - ADAM_B1) * gv
        nv = ADAM_B2 * v_ref[...] + (1.0 - ADAM_B2) * (gv * gv)
        nm_ref[...] = nm
        nv_ref[...] = nv
        d_ref[...] = -ADAM_LR * ((nm / c1) / (jnp.sqrt(nv / c2) + ADAM_EPS) + ADAM_WD * w_ref[...])

    spec = pl.BlockSpec((tr, C), lambda i: (i, 0))
    shp = jax.ShapeDtypeStruct((R, C), F32)
    return pl.pallas_call(
        body, name=name, grid=(R // tr,),
        in_specs=[spec] * 4, out_specs=[spec] * 3, out_shape=[shp] * 3,
        compiler_params=_params(("parallel",)),
    )(w, g, m, v)


def _sum4(q, *, name):
    _, R, C = q.shape
    tr = R
    for cand in (256, 128, 64, 32, 16, 8):
        if R % cand == 0 and R > cand:
            tr = cand
            break

    def body(q_ref, o_ref):
        o_ref[...] = ((q_ref[0] + q_ref[1]) + q_ref[2]) + q_ref[3]

    return pl.pallas_call(
        body, name=name, grid=(R // tr,),
        in_specs=[pl.BlockSpec((N_CHIPS, tr, C), lambda i: (0, i, 0))],
        out_specs=pl.BlockSpec((tr, C), lambda i: (i, 0)),
        out_shape=jax.ShapeDtypeStruct((R, C), F32),
        compiler_params=_params(("parallel",)),
    )(q)


def _add_sibling(g, got, c, *, name):
    K, _, R, C = g.shape
    tr = R
    for cand in (256, 128, 64, 32, 16, 8):
        if R % cand == 0 and R > cand:
            tr = cand
            break

    def body(c_ref, a_ref, b_ref, o_ref):
        o_ref[...] = a_ref[...] + b_ref[...]

    spec = pl.BlockSpec((None, tr, C), lambda k, i, c_ref: (k, i, 0))
    return pl.pallas_call(
        body, name=name,
        grid_spec=pltpu.PrefetchScalarGridSpec(
            num_scalar_prefetch=1, grid=(K, R // tr),
            in_specs=[pl.BlockSpec((None, None, tr, C), lambda k, i, c_ref: (k, c_ref[0], i, 0)), spec],
            out_specs=spec),
        out_shape=jax.ShapeDtypeStruct((K, R, C), F32),
        compiler_params=_params(("parallel", "parallel")),
    )(jnp.reshape(c, (1,)).astype(jnp.int32), g, got)


ANY = pl.BlockSpec(memory_space=pl.ANY)


def _other_chips(x, y):
    return [(d, 1 - x if d & 2 else x, 1 - y if d & 1 else y) for d in (1, 2, 3)]


def _weight_all_gather(shards, *, name):
    n = len(shards)

    def body(*refs):
        src_refs, out_refs = refs[:n], refs[n:2 * n]
        ici_send, ici_recv, d2d_send, d2d_recv, local_sems = refs[2 * n:]
        x, y, c = lax.axis_index("x"), lax.axis_index("y"), lax.axis_index("c")
        me = 2 * x + y
        sends = []
        local = []
        for t in range(n):
            cp = pltpu.make_async_copy(src_refs[t], out_refs[t].at[me], local_sems.at[t])
            cp.start()
            local.append(cp)
            for d, tx, ty in _other_chips(x, y):
                cp = pltpu.make_async_remote_copy(
                    src_ref=src_refs[t].at[c], dst_ref=out_refs[t].at[me, c],
                    send_sem=ici_send.at[3 * t + d - 1], recv_sem=ici_recv.at[3 * t + d - 1],
                    device_id=(tx, ty, c), device_id_type=MESH)
                cp.start()
                sends.append(cp)
        for t in range(n):
            for d, tx, ty in _other_chips(x, y):
                k = 2 * tx + ty
                landed = out_refs[t].at[k, c]
                pltpu.make_async_remote_copy(
                    src_ref=landed, dst_ref=landed,
                    send_sem=ici_send.at[3 * t + d - 1], recv_sem=ici_recv.at[3 * t + d - 1],
                    device_id=(tx, ty, c), device_id_type=MESH).wait_recv()
                cp = pltpu.make_async_remote_copy(
                    src_ref=landed, dst_ref=landed,
                    send_sem=d2d_send.at[3 * t + d - 1], recv_sem=d2d_recv.at[3 * t + d - 1],
                    device_id=(x, y, 1 - c), device_id_type=MESH)
                cp.start()
                sends.append(cp)
        for t in range(n):
            for d, tx, ty in _other_chips(x, y):
                theirs = out_refs[t].at[2 * tx + ty, 1 - c]
                pltpu.make_async_remote_copy(
                    src_ref=theirs, dst_ref=theirs,
                    send_sem=d2d_send.at[3 * t + d - 1], recv_sem=d2d_recv.at[3 * t + d - 1],
                    device_id=(x, y, 1 - c), device_id_type=MESH).wait_recv()
        for cp in sends:
            cp.wait_send()
        for cp in local:
            cp.wait()

    sem = pltpu.SemaphoreType.DMA((3 * n,))
    return pl.pallas_call(
        body, name=name,
        in_specs=[ANY] * n, out_specs=[ANY] * n,
        out_shape=[jax.ShapeDtypeStruct((N_CHIPS,) + s.shape, s.dtype) for s in shards],
        scratch_shapes=[sem, sem, sem, sem, pltpu.SemaphoreType.DMA((n,))],
    )(*shards)


def _scatter_copies(src_refs, out_refs, send_sems, recv_sems, local_sems):
    n = len(src_refs)
    x, y, c = lax.axis_index("x"), lax.axis_index("y"), lax.axis_index("c")
    me = 2 * x + y
    copies = []
    for t in range(n):
        copies.append(pltpu.make_async_copy(src_refs[t].at[me], out_refs[t].at[me], local_sems.at[t]))
        for d, tx, ty in _other_chips(x, y):
            copies.append(pltpu.make_async_remote_copy(
                src_ref=src_refs[t].at[2 * tx + ty], dst_ref=out_refs[t].at[me],
                send_sem=send_sems.at[3 * t + d - 1], recv_sem=recv_sems.at[3 * t + d - 1],
                device_id=(tx, ty, c), device_id_type=MESH))
    return copies


def _mm_scatter(a, b, add, srcs, *, tm, tn, tk, name):
    M, K = a.shape
    N = b.shape[1]
    tm, tn, tk = min(tm, M), min(tn, N), min(tk, K)
    assert M % tm == 0 and N % tn == 0 and K % tk == 0, (name, M, N, K, tm, tn, tk)
    gi, gj, gk = M // tm, N // tn, K // tk
    n = len(srcs)

    def body(*refs):
        a_ref, b_ref, add_ref = refs[:3]
        src_refs = refs[3:3 + n]
        o_ref = refs[3 + n]
        out_refs = refs[4 + n:4 + 2 * n]
        acc_ref, send_sems, recv_sems, local_sems = refs[4 + 2 * n:]
        i, j, k = pl.program_id(0), pl.program_id(1), pl.program_id(2)

        @pl.when((i == 0) & (j == 0) & (k == 0))
        def _():
            for cp in _scatter_copies(src_refs, out_refs, send_sems, recv_sems, local_sems):
                cp.start()

        @pl.when(k == 0)
        def _():
            acc_ref[...] = jnp.zeros_like(acc_ref)

        acc_ref[...] += jnp.dot(a_ref[...].astype(BF16), b_ref[...].astype(BF16), preferred_element_type=F32)

        @pl.when(k == gk - 1)
        def _():
            o_ref[...] = acc_ref[...] + add_ref[...]

        @pl.when((i == gi - 1) & (j == gj - 1) & (k == gk - 1))
        def _():
            for cp in _scatter_copies(src_refs, out_refs, send_sems, recv_sems, local_sems):
                cp.wait()

    o_spec = pl.BlockSpec((tm, tn), lambda i, j, k: (i, j))
    sem = pltpu.SemaphoreType.DMA((3 * n,))
    res = pl.pallas_call(
        body, name=name, grid=(gi, gj, gk),
        in_specs=[pl.BlockSpec((tm, tk), lambda i, j, k: (i, k)), pl.BlockSpec((tk, tn), lambda i, j, k: (k, j)),
                  o_spec] + [ANY] * n,
        out_specs=[o_spec] + [ANY] * n,
        out_shape=[jax.ShapeDtypeStruct((M, N), F32)] + [jax.ShapeDtypeStruct(s.shape, s.dtype) for s in srcs],
        scratch_shapes=[pltpu.VMEM((tm, tn), F32), sem, sem, pltpu.SemaphoreType.DMA((n,))],
        compiler_params=_params(("arbitrary", "arbitrary", "arbitrary")),
    )(a, b, add, *srcs)
    return res[0], res[1:]


def _swap_halves(grads, *, name):
    n = len(grads)

    def body(*refs):
        src_refs, out_refs = refs[:n], refs[n:2 * n]
        send_sems, recv_sems = refs[2 * n:]
        x, y, c = lax.axis_index("x"), lax.axis_index("y"), lax.axis_index("c")
        copies = []
        for t in range(n):
            for k in range(N_CHIPS):
                cp = pltpu.make_async_remote_copy(
                    src_ref=src_refs[t].at[k, 1 - c], dst_ref=out_refs[t].at[k],
                    send_sem=send_sems.at[N_CHIPS * t + k], recv_sem=recv_sems.at[N_CHIPS * t + k],
                    device_id=(x, y, 1 - c), device_id_type=MESH)
                cp.start()
                copies.append(cp)
        for cp in copies:
            cp.wait()

    sem = pltpu.SemaphoreType.DMA((N_CHIPS * n,))
    return pl.pallas_call(
        body, name=name,
        in_specs=[ANY] * n, out_specs=[ANY] * n,
        out_shape=[jax.ShapeDtypeStruct((N_CHIPS,) + g.shape[2:], g.dtype) for g in grads],
        scratch_shapes=[sem, sem],
    )(*grads)


def _join_halves(reds, *, name):
    n = len(reds)

    def body(*refs):
        src_refs, out_refs = refs[:n], refs[n:2 * n]
        send_sems, recv_sems, local_sems = refs[2 * n:]
        x, y, c = lax.axis_index("x"), lax.axis_index("y"), lax.axis_index("c")
        copies = []
        for t in range(n):
            cp = pltpu.make_async_copy(src_refs[t], out_refs[t].at[c], local_sems.at[t])
            cp.start()
            copies.append(cp)
            cp = pltpu.make_async_remote_copy(
                src_ref=src_refs[t], dst_ref=out_refs[t].at[c],
                send_sem=send_sems.at[t], recv_sem=recv_sems.at[t],
                device_id=(x, y, 1 - c), device_id_type=MESH)
            cp.start()
            copies.append(cp)
        for cp in copies:
            cp.wait()

    sem = pltpu.SemaphoreType.DMA((n,))
    return pl.pallas_call(
        body, name=name,
        in_specs=[ANY] * n, out_specs=[ANY] * n,
        out_shape=[jax.ShapeDtypeStruct((2,) + r.shape, r.dtype) for r in reds],
        scratch_shapes=[sem, sem, sem],
    )(*reds)


def _allreduce_small(v, *, name):
    R, C = v.shape

    def body(v_ref, o_ref, buf_ref, send_sems, recv_sems):
        x, y, c = lax.axis_index("x"), lax.axis_index("y"), lax.axis_index("c")
        me = 4 * x + 2 * y + c
        buf_ref[me] = v_ref[...]
        copies = []
        for d in range(1, N_DEV):
            tx = 1 - x if d & 4 else x
            ty = 1 - y if d & 2 else y
            tc = 1 - c if d & 1 else c
            cp = pltpu.make_async_remote_copy(
                src_ref=v_ref, dst_ref=buf_ref.at[me],
                send_sem=send_sems.at[d - 1], recv_sem=recv_sems.at[d - 1],
                device_id=(tx, ty, tc), device_id_type=MESH)
            cp.start()
            copies.append(cp)
        for cp in copies:
            cp.wait()
        acc = buf_ref[0]
        for k in range(1, N_DEV):
            acc = acc + buf_ref[k]
        o_ref[...] = acc

    return pl.pallas_call(
        body, name=name,
        in_specs=[pl.BlockSpec(memory_space=pltpu.VMEM)],
        out_specs=pl.BlockSpec(memory_space=pltpu.VMEM),
        out_shape=jax.ShapeDtypeStruct((R, C), F32),
        scratch_shapes=[pltpu.VMEM((N_DEV, R, C), F32), pltpu.SemaphoreType.DMA((N_DEV - 1,)),
                        pltpu.SemaphoreType.DMA((N_DEV - 1,))],
    )(v)


_A0, _B0, _GC0 = 0, CONV_W, 2 * CONV_W
_Q0 = 3 * CONV_W
_FL0 = _Q0 + 3 * FOX_W
_FG0 = _FL0 + N_FOX
_MQ0 = _FG0 + FOX_W
_MG0 = _MQ0 + MEM_W
_DIN = _MG0 + MEM_W
_WMAIN = _DIN - N_FOX
_PC_N = 3 * CONV_W
_QKV_N = 3 * FOX_W
_PG_N = FOX_W + 2 * MEM_W


def _rows_pad8(a):
    r = a.shape[0]
    p = (-r) % 8
    return jnp.pad(a, ((0, p), (0, 0))) if p else a


def _step(x, mem, target, norm_g, mem_norm_g, final_g, b_f, cw_pad, conv_b, ln_g, ln_b,
          w_in, w_pw, w_mkv, w_out, ci):
    S, D = x.shape
    M = mem.shape[0]
    tq = min(512, S)
    tf = min(1024, S)
    tt = min(256, S)
    tm = min(256, S)

    w_main = jnp.concatenate([w_in[:, :_FL0], w_in[:, _FG0:]], axis=1)
    w_fl = jnp.pad(w_in[:, _FL0:_FG0], ((0, 0), (0, LANE - N_FOX)))
    w_main_t, w_fl_t, w_out_t = w_main.T, w_fl.T, w_out.T
    bf_pad = jnp.pad(b_f, ((0, 0), (0, LANE - N_FOX)))

    h = _rms_fwd(x, norm_g, tm=tm, name="rms_fwd")
    h_t = h.T
    pc = _mm(h, w_main, tm=1024, tn=512, tk=2048, n=_PC_N, b_off=0, name="proj_conv")
    qkv = _mm(h, w_main, out_dtype=BF16, tm=1024, tn=512, tk=2048, n=_QKV_N, b_off=_PC_N // 512, name="proj_qkv")
    pg = _mm(h, w_main, tm=1024, tn=512, tk=2048, n=_PG_N, b_off=(_PC_N + _QKV_N) // 512, name="proj_gate")
    pfl = _mm(h, w_fl, tm=1024, tn=LANE, tk=2048, name="proj_logit")

    y_conv = _conv_fwd(pc, cw_pad, conv_b, ln_g, ln_b, w_pw, tt=tt, name="conv_fwd")

    cum = _fox_cumsum(pfl, bf_pad, tc=tt, name="fox_cumsum")
    qa, ka = _fox_bias_operands(cum[:, :N_FOX])
    y_fox, o_fox, lse = _fox_fwd(qkv, pg, qa, ka, tq=tf, nsub=2, name="fox_fwd")

    hm = _rms_fwd(mem, mem_norm_g, tm=min(256, M), name="rms_mem")
    mkv = _mm(hm, w_mkv, tm=256, tn=512, tk=2048, name="mem_kv")
    y_mem = _mem_fwd(pg, mkv, tq=tq, name="mem_fwd")

    y = jnp.concatenate([y_conv, y_fox, y_mem], axis=1)
    x2 = _mm(y, w_out, add=x, tm=1024, tn=512, tk=2048, name="out_proj")
    dx2, dfg, sq = _final(x2, target, final_g.reshape(1, D), tm=tm, name="final")

    dy = _mm(dx2, w_out_t, tm=512, tn=512, tk=2048, name="d_y")
    dw_out = _mm(y.T, dx2, tm=1024, tn=512, tk=1024, name="d_w_out")

    dpc, dw_pw, dcw, dsm = _conv_bwd(pc, dy, cw_pad, conv_b, ln_g, ln_b, w_pw, tt=tt, name="conv_bwd")

    do, dfgate, delta = _fox_bwd_prep(dy, o_fox, pg, tq=tq, name="fox_bwd_prep")
    rowv = lambda a: a.reshape(N_FOX, S // tf, 1, tf)
    dq, dk, dv, dc = _fox_bwd(qkv, do, qa, ka, rowv(lse), rowv(delta), tq=tf, nsub=2, name="fox_bwd")
    dc_pad = jnp.pad(dc.reshape(N_FOX, S).T, ((0, 0), (0, LANE - N_FOX)))
    dfl, dbf = _fox_dlogit(dc_pad, pfl, bf_pad, tc=tt, name="fox_dlogit")

    dmq, dmgate, dmkv = _mem_bwd(pg, mkv, dy, tq=tq, name="mem_bwd")
    dw_mkv = _mm(hm, dmkv, ta=True, tm=512, tn=512, tk=256, name="d_w_mkv")
    dhm = _mm(dmkv, w_mkv, tb=True, tm=256, tn=512, tk=1024, name="d_hm")
    _, dmg = _rms_bwd(mem, mem_norm_g, dhm, None, tm=min(256, M), name="rms_mem_bwd")

    dproj = jnp.concatenate([dpc, dq.astype(BF16), dk, dv, dfgate, dmq, dmgate], axis=1)
    dw_main = _mm(h_t, dproj, tm=1024, tn=_WMAIN // 4, tk=1024, name="d_w_main")
    dw_fl = _mm(h_t, dfl, tm=1024, tn=LANE, tk=1024, name="d_w_logit")
    dw_in = jnp.concatenate([dw_main[:, :_FL0], dw_fl[:, :N_FOX], dw_main[:, _FL0:]], axis=1)

    nin = _DIN // N_CHIPS
    big = [jnp.transpose(dw_in.reshape(D, N_CHIPS, nin), (1, 0, 2)),
           dw_pw.reshape(N_CHIPS, CONV_W // N_CHIPS, CONV_W),
           dw_mkv.reshape(N_CHIPS, D // N_CHIPS, 2 * MEM_W),
           dw_out.reshape(N_CHIPS, MIX_W // N_CHIPS, D)]
    big = [g.reshape(N_CHIPS, 2, g.shape[1] // 2, g.shape[2]) for g in big]
    got = _swap_halves(big, name="grad_swap_halves")
    chip = [_add_sibling(g, o, ci, name=f"grad_add_sibling_{t}") for t, (g, o) in enumerate(zip(big, got))]

    dh0 = _mm(dfl, w_fl_t, tm=512, tn=512, tk=LANE, name="d_h_logit")
    dh, parts = _mm_scatter(dproj, w_main_t, dh0, chip, tm=512, tn=512, tk=_WMAIN // 2, name="d_h_grad_scatter")
    grad_x, dng = _rms_bwd(x, norm_g, dh, dx2, tm=tm, name="rms_bwd")

    red = [_sum4(p, name=f"grad_sum_chips_{t}") for t, p in enumerate(parts)]
    full = _join_halves(red, name="grad_join_halves")
    full = [f.reshape(2 * f.shape[1], f.shape[2]) for f in full]

    small = dict(norm_g=dng, mem_norm_g=dmg, final_g=dfg, b_f=dbf[0:1, :], conv_w=dcw,
                 conv_b=dsm[0:1], conv_ln_g=dsm[1:2], conv_ln_b=dsm[2:3])
    return sq[0, 0], grad_x, full, small


_SMALL_ORDER = ("norm_g", "mem_norm_g", "final_g", "b_f", "conv_w", "conv_b", "conv_ln_g", "conv_ln_b")


def _pack_small(small):
    parts, layout = [], []
    row = 0
    for k in _SMALL_ORDER:
        p = _rows_pad8(small[k].reshape(-1, LANE))
        layout.append((k, row, small[k].shape))
        parts.append(p)
        row += p.shape[0]
    return jnp.concatenate(parts, axis=0), layout


def _unpack_small(packed, layout):
    out = {}
    for k, row, shape in layout:
        nrow = (shape[0] * shape[1]) // LANE
        out[k] = packed[row:row + nrow].reshape(shape)
    return out


def kernel(x, mem, norm_g, mem_norm_g, w_in, b_f, conv_w, conv_b, conv_ln_g, conv_ln_b, w_conv_pw, w_mem_kv, w_out, final_g, loss_target, m_norm_g, m_mem_norm_g, m_w_in, m_b_f, m_conv_w, m_conv_b, m_conv_ln_g, m_conv_ln_b, m_w_conv_pw, m_w_mem_kv, m_w_out, m_final_g, v_norm_g, v_mem_norm_g, v_w_in, v_b_f, v_conv_w, v_conv_b, v_conv_ln_g, v_conv_ln_b, v_w_conv_pw, v_w_mem_kv, v_w_out, v_final_g):
    S, D = x.shape[1], x.shape[2]
    xi, yi, ci = lax.axis_index("x"), lax.axis_index("y"), lax.axis_index("c")
    chip = 2 * xi + yi

    halves = lambda a: a.reshape(2, a.shape[0] // 2, a.shape[1])
    cw_shard = jnp.pad(conv_w[0], ((0, CONV_HALO - CONV_K), (0, 0)))
    g_in, g_pw, g_mkv, g_out, g_cw = _weight_all_gather(
        [halves(w_in[0].astype(BF16)), halves(w_conv_pw[0].astype(BF16)), halves(w_mem_kv[0].astype(BF16)),
         halves(w_out[0].astype(BF16)), halves(cw_shard)], name="weight_all_gather")
    nin = w_in.shape[2]
    w_in_full = jnp.transpose(g_in.reshape(N_CHIPS, D, nin), (1, 0, 2)).reshape(D, N_CHIPS * nin)
    w_pw_full = g_pw.reshape(CONV_W, CONV_W)
    w_mkv_full = g_mkv.reshape(D, 2 * MEM_W)
    w_out_full = g_out.reshape(MIX_W, D)
    cw_full = jnp.transpose(g_cw.reshape(N_CHIPS, CONV_HALO, CONV_W // N_CHIPS), (1, 0, 2)).reshape(CONV_HALO, CONV_W)

    sq, grad_x, (g_w_in, g_w_pw, g_w_mkv, g_w_out), small = _step(
        x[0], mem[0], loss_target[0], norm_g, mem_norm_g, final_g, b_f, cw_full, conv_b, conv_ln_g,
        conv_ln_b, w_in_full, w_pw_full, w_mkv_full, w_out_full, ci)

    loss = lax.psum(sq, ("x", "y", "c")) * (0.5 / D)

    packed, layout = _pack_small(small)
    sm = _unpack_small(_allreduce_small(packed, name="small_all_reduce"), layout)
    cshard = CONV_W // N_CHIPS
    g_conv_w = lax.dynamic_slice_in_dim(sm["conv_w"][:CONV_K], chip * cshard, cshard, axis=1)

    grads = dict(
        norm_g=sm["norm_g"], mem_norm_g=sm["mem_norm_g"], w_in=g_w_in[None], b_f=sm["b_f"][:, :N_FOX],
        conv_w=g_conv_w[None], conv_b=sm["conv_b"], conv_ln_g=sm["conv_ln_g"], conv_ln_b=sm["conv_ln_b"],
        w_conv_pw=g_w_pw[None], w_mem_kv=g_w_mkv[None], w_out=g_w_out[None], final_g=sm["final_g"].reshape(D))
    weights = dict(norm_g=norm_g, mem_norm_g=mem_norm_g, w_in=w_in, b_f=b_f, conv_w=conv_w, conv_b=conv_b,
                   conv_ln_g=conv_ln_g, conv_ln_b=conv_ln_b, w_conv_pw=w_conv_pw, w_mem_kv=w_mem_kv, w_out=w_out,
                   final_g=final_g)
    ms = dict(norm_g=m_norm_g, mem_norm_g=m_mem_norm_g, w_in=m_w_in, b_f=m_b_f, conv_w=m_conv_w, conv_b=m_conv_b,
              conv_ln_g=m_conv_ln_g, conv_ln_b=m_conv_ln_b, w_conv_pw=m_w_conv_pw, w_mem_kv=m_w_mem_kv,
              w_out=m_w_out, final_g=m_final_g)
    vs = dict(norm_g=v_norm_g, mem_norm_g=v_mem_norm_g, w_in=v_w_in, b_f=v_b_f, conv_w=v_conv_w, conv_b=v_conv_b,
              conv_ln_g=v_conv_ln_g, conv_ln_b=v_conv_ln_b, w_conv_pw=v_w_conv_pw, w_mem_kv=v_w_mem_kv,
              w_out=v_w_out, final_g=v_final_g)

    names = ("norm_g", "mem_norm_g", "w_in", "b_f", "conv_w", "conv_b", "conv_ln_g", "conv_ln_b", "w_conv_pw",
             "w_mem_kv", "w_out", "final_g")
    deltas, new_m, new_v = {}, {}, {}
    for k in names:
        shape = weights[k].shape
        two_d = (lambda a: a.reshape(-1, shape[-1]))
        d, nm, nv = _adamw(two_d(weights[k]), two_d(grads[k]), two_d(ms[k]), two_d(vs[k]), name=f"adamw_{k}")
        deltas[k], new_m[k], new_v[k] = d.reshape(shape), nm.reshape(shape), nv.reshape(shape)

    return (loss, grad_x[None], *[grads[k] for k in names], *[deltas[k] for k in names],
            *[new_m[k] for k in names], *[new_v[k] for k in names])
```

```python
import functools

import jax
import jax.numpy as jnp
from jax import lax
from jax.experimental import pallas as pl
from jax.experimental.pallas import tpu as pltpu

F32 = jnp.float32
BF16 = jnp.bfloat16
MESH = pl.DeviceIdType.MESH

HEAD = 128
N_FOX = 8
N_MEMH = 4
CONV_W = 512
FOX_W = N_FOX * HEAD
MEM_W = N_MEMH * HEAD
MIX_W = CONV_W + FOX_W + MEM_W
CONV_K = 31
CONV_HALO = 32
EPS = 1e-6
NEG = -1e30
LANE = 128
N_CHIPS = 4
N_DEV = 8
VMEM_LIMIT = 56 * 1024 * 1024

ADAM_LR = 0.001
ADAM_B1 = 0.9
ADAM_B2 = 0.999
ADAM_EPS = 1e-08
ADAM_WD = 0.01
ADAM_STEP = 10

NT = (((1,), (1,)), ((), ()))
TN = (((0,), (0,)), ((), ()))
NN = (((1,), (0,)), ((), ()))


def _params(sem=None):
    kw = dict(vmem_limit_bytes=VMEM_LIMIT)
    if sem is not None:
        kw["dimension_semantics"] = sem
    return pltpu.CompilerParams(**kw)


def _sigmoid(v):
    return jax.nn.sigmoid(v)


def _dsilu(v, s):
    return s * (1.0 + v * (1.0 - s))


def _rms_fwd(x, g, *, tm, name):
    R, D = x.shape

    def body(x_ref, g_ref, h_ref):
        xv = x_ref[...]
        r = lax.rsqrt(jnp.mean(xv * xv, axis=-1, keepdims=True) + EPS)
        h_ref[...] = (xv * r * g_ref[...]).astype(BF16)

    return pl.pallas_call(
        body, name=name, grid=(R // tm,),
        in_specs=[pl.BlockSpec((tm, D), lambda i: (i, 0)), pl.BlockSpec((1, D), lambda i: (0, 0))],
        out_specs=pl.BlockSpec((tm, D), lambda i: (i, 0)),
        out_shape=jax.ShapeDtypeStruct((R, D), BF16),
        compiler_params=_params(("parallel",)),
    )(x, g)


def _rms_bwd(x, g, dh, dres, *, tm, name):
    R, D = x.shape
    has_res = dres is not None

    def body(*refs):
        if has_res:
            x_ref, g_ref, dh_ref, dres_ref, dx_ref, dg_ref = refs
        else:
            x_ref, g_ref, dh_ref, dx_ref, dg_ref = refs
        i = pl.program_id(0)

        @pl.when(i == 0)
        def _():
            dg_ref[...] = jnp.zeros_like(dg_ref)

        xv = x_ref[...]
        r = lax.rsqrt(jnp.mean(xv * xv, axis=-1, keepdims=True) + EPS)
        n = xv * r
        dh = dh_ref[...]
        dg_ref[...] += jnp.sum(dh * n, axis=0, keepdims=True)
        dn = dh * g_ref[...]
        dx = r * (dn - n * jnp.mean(dn * n, axis=-1, keepdims=True))
        if has_res:
            dx = dx + dres_ref[...]
        dx_ref[...] = dx

    row = pl.BlockSpec((tm, D), lambda i: (i, 0))
    vec = pl.BlockSpec((1, D), lambda i: (0, 0))
    ins = [row, vec, row] + ([row] if has_res else [])
    args = (x, g, dh) + ((dres,) if has_res else ())
    return pl.pallas_call(
        body, name=name, grid=(R // tm,),
        in_specs=ins, out_specs=[row, vec],
        out_shape=[jax.ShapeDtypeStruct((R, D), F32), jax.ShapeDtypeStruct((1, D), F32)],
        compiler_params=_params(("arbitrary",)),
    )(*args)


def _mm(a, b, *, ta=False, tb=False, out_dtype=F32, add=None, tm, tn, tk, n=None, b_off=0, name):
    M, K = (a.shape[1], a.shape[0]) if ta else a.shape
    nb = b.shape[0] if tb else b.shape[1]
    n = nb if n is None else n
    tm, tn, tk = min(tm, M), min(tn, n), min(tk, K)
    assert M % tm == 0 and n % tn == 0 and K % tk == 0, (name, M, n, K, tm, tn, tk)
    nk = K // tk
    has_add = add is not None

    def body(*refs):
        if has_add:
            a_ref, b_ref, add_ref, o_ref, acc_ref = refs
        else:
            a_ref, b_ref, o_ref, acc_ref = refs
        k = pl.program_id(2)

        @pl.when(k == 0)
        def _():
            acc_ref[...] = jnp.zeros_like(acc_ref)

        av = a_ref[...].astype(BF16)
        bv = b_ref[...].astype(BF16)
        dims = (((0 if ta else 1,), (1 if tb else 0,)), ((), ()))
        acc_ref[...] += lax.dot_general(av, bv, dims, preferred_element_type=F32)

        @pl.when(k == nk - 1)
        def _():
            r = acc_ref[...]
            if has_add:
                r = r + add_ref[...]
            o_ref[...] = r.astype(out_dtype)

    a_spec = pl.BlockSpec((tk, tm), lambda i, j, k: (k, i)) if ta else pl.BlockSpec((tm, tk), lambda i, j, k: (i, k))
    b_spec = (pl.BlockSpec((tn, tk), lambda i, j, k: (j + b_off, k)) if tb
              else pl.BlockSpec((tk, tn), lambda i, j, k: (k, j + b_off)))
    o_spec = pl.BlockSpec((tm, tn), lambda i, j, k: (i, j))
    ins = [a_spec, b_spec] + ([o_spec] if has_add else [])
    args = (a, b) + ((add,) if has_add else ())
    return pl.pallas_call(
        body, name=name, grid=(M // tm, n // tn, nk),
        in_specs=ins, out_specs=o_spec,
        out_shape=jax.ShapeDtypeStruct((M, n), out_dtype),
        scratch_shapes=[pltpu.VMEM((tm, tn), F32)],
        compiler_params=_params(("parallel", "parallel", "arbitrary")),
    )(*args)


def _conv_taps(w_ref, e_ref, first, tt):
    acc = w_ref[0:1, :] * e_ref[first:first + tt, :]
    for k in range(1, CONV_K):
        acc = acc + w_ref[k:k + 1, :] * e_ref[first + k:first + k + tt, :]
    return acc


def _conv_fwd(pc, cw, cb, lng, lnb, wpw, *, tt, name):
    S = pc.shape[0]
    C = CONV_W
    lead = CONV_HALO - (CONV_K - 1)

    def body(a_ref, b_ref, gc_ref, cw_ref, cb_ref, lng_ref, lnb_ref, wpw_ref, y_ref, e_ref):
        i = pl.program_id(0)

        @pl.when(i == 0)
        def _():
            e_ref[0:CONV_HALO, :] = jnp.zeros((CONV_HALO, C), F32)

        @pl.when(i > 0)
        def _():
            e_ref[0:CONV_HALO, :] = e_ref[tt:tt + CONV_HALO, :]

        e_ref[CONV_HALO:CONV_HALO + tt, :] = a_ref[...] * _sigmoid(b_ref[...])
        u1 = _conv_taps(cw_ref, e_ref, lead, tt) + cb_ref[...]
        mu = jnp.mean(u1, axis=-1, keepdims=True)
        xc = u1 - mu
        rstd = lax.rsqrt(jnp.mean(xc * xc, axis=-1, keepdims=True) + EPS)
        u2 = xc * rstd * lng_ref[...] + lnb_ref[...]
        u3 = u2 * _sigmoid(u2)
        z = jnp.dot(u3.astype(BF16), wpw_ref[...], preferred_element_type=F32)
        gc = gc_ref[...]
        y_ref[...] = (z * gc * _sigmoid(gc)).astype(BF16)

    col = lambda c: pl.BlockSpec((tt, C), lambda i, c=c: (i, c))
    vec = pl.BlockSpec((1, C), lambda i: (0, 0))
    return pl.pallas_call(
        body, name=name, grid=(S // tt,),
        in_specs=[col(0), col(1), col(2), pl.BlockSpec((CONV_HALO, C), lambda i: (0, 0)), vec, vec, vec,
                  pl.BlockSpec((C, C), lambda i: (0, 0))],
        out_specs=pl.BlockSpec((tt, C), lambda i: (i, 0)),
        out_shape=jax.ShapeDtypeStruct((S, C), BF16),
        scratch_shapes=[pltpu.VMEM((tt + CONV_HALO, C), F32)],
        compiler_params=_params(("arbitrary",)),
    )(pc, pc, pc, cw, cb, lng, lnb, wpw)


def _conv_bwd(pc, dy, cw, cb, lng, lnb, wpw, *, tt, name):
    S = pc.shape[0]
    C = CONV_W
    nt = S // tt
    hb = tt // CONV_HALO
    lead = CONV_HALO - (CONV_K - 1)

    def body(a_ref, b_ref, gc_ref, ah_ref, bh_ref, dy_ref, cw_ref, cb_ref, lng_ref, lnb_ref, wpw_ref,
             dpc_ref, dwpw_ref, dcw_ref, dsm_ref, eu_ref, ed_ref):
        i = pl.program_id(0)
        ti = nt - 1 - i

        @pl.when(i == 0)
        def _():
            ed_ref[tt:tt + CONV_HALO, :] = jnp.zeros((CONV_HALO, C), F32)
            dwpw_ref[...] = jnp.zeros_like(dwpw_ref)
            dcw_ref[...] = jnp.zeros_like(dcw_ref)
            dsm_ref[...] = jnp.zeros_like(dsm_ref)

        @pl.when(i > 0)
        def _():
            ed_ref[tt:tt + CONV_HALO, :] = ed_ref[0:CONV_HALO, :]

        keep = jnp.where(ti > 0, 1.0, 0.0).astype(F32)
        eu_ref[0:CONV_HALO, :] = ah_ref[...] * _sigmoid(bh_ref[...]) * keep
        a = a_ref[...]
        sb = _sigmoid(b_ref[...])
        eu_ref[CONV_HALO:CONV_HALO + tt, :] = a * sb

        u1 = _conv_taps(cw_ref, eu_ref, lead, tt) + cb_ref[...]
        mu = jnp.mean(u1, axis=-1, keepdims=True)
        xc = u1 - mu
        rstd = lax.rsqrt(jnp.mean(xc * xc, axis=-1, keepdims=True) + EPS)
        nhat = xc * rstd
        g = lng_ref[...]
        u2 = nhat * g + lnb_ref[...]
        s2 = _sigmoid(u2)
        u3 = (u2 * s2).astype(BF16)
        z = jnp.dot(u3, wpw_ref[...], preferred_element_type=F32)

        gc = gc_ref[...]
        sg = _sigmoid(gc)
        dyv = dy_ref[...]
        dz = (dyv * gc * sg).astype(BF16)
        dpc_ref[:, 2 * C:3 * C] = (dyv * z * _dsilu(gc, sg)).astype(BF16)

        du3 = lax.dot_general(dz, wpw_ref[...], NT, preferred_element_type=F32)
        dwpw_ref[...] += lax.dot_general(u3, dz, TN, preferred_element_type=F32)
        du2 = du3 * _dsilu(u2, s2)
        dsm_ref[1:2, :] += jnp.sum(du2 * nhat, axis=0, keepdims=True)
        dsm_ref[2:3, :] += jnp.sum(du2, axis=0, keepdims=True)
        dn = du2 * g
        du1 = rstd * (dn - jnp.mean(dn, axis=-1, keepdims=True)
                      - nhat * jnp.mean(dn * nhat, axis=-1, keepdims=True))
        dsm_ref[0:1, :] += jnp.sum(du1, axis=0, keepdims=True)
        ed_ref[0:tt, :] = du1

        du0 = cw_ref[0:1, :] * ed_ref[CONV_K - 1:CONV_K - 1 + tt, :]
        for k in range(1, CONV_K):
            o = CONV_K - 1 - k
            du0 = du0 + cw_ref[k:k + 1, :] * ed_ref[o:o + tt, :]
        for k in range(CONV_K):
            dcw_ref[k:k + 1, :] += jnp.sum(du1 * eu_ref[lead + k:lead + k + tt, :], axis=0, keepdims=True)

        dpc_ref[:, 0:C] = (du0 * sb).astype(BF16)
        dpc_ref[:, C:2 * C] = (du0 * a * sb * (1.0 - sb)).astype(BF16)

    col = lambda c: pl.BlockSpec((tt, C), lambda i, c=c: (nt - 1 - i, c))
    halo = lambda c: pl.BlockSpec((CONV_HALO, C), lambda i, c=c: (jnp.maximum((nt - 1 - i) * hb - 1, 0), c))
    vec = pl.BlockSpec((1, C), lambda i: (0, 0))
    fixed = lambda r: pl.BlockSpec((r, C), lambda i: (0, 0))
    return pl.pallas_call(
        body, name=name, grid=(nt,),
        in_specs=[col(0), col(1), col(2), halo(0), halo(1),
                  pl.BlockSpec((tt, C), lambda i: (nt - 1 - i, 0)),
                  fixed(CONV_HALO), vec, vec, vec, fixed(C)],
        out_specs=[pl.BlockSpec((tt, 3 * C), lambda i: (nt - 1 - i, 0)), fixed(C), fixed(CONV_HALO), fixed(8)],
        out_shape=[jax.ShapeDtypeStruct((S, 3 * C), BF16), jax.ShapeDtypeStruct((C, C), F32),
                   jax.ShapeDtypeStruct((CONV_HALO, C), F32), jax.ShapeDtypeStruct((8, C), F32)],
        scratch_shapes=[pltpu.VMEM((tt + CONV_HALO, C), F32), pltpu.VMEM((tt + CONV_HALO, C), F32)],
        compiler_params=_params(("arbitrary",)),
    )(pc, pc, pc, pc, pc, dy, cw, cb, lng, lnb, wpw)


def _tri(n, lower):
    r = lax.broadcasted_iota(jnp.int32, (n, n), 0)
    c = lax.broadcasted_iota(jnp.int32, (n, n), 1)
    return jnp.where((r >= c) if lower else (r <= c), 1.0, 0.0).astype(F32)


def _fox_cumsum(pfl, bf, *, tc, name):
    S = pfl.shape[0]

    def body(fl_ref, bf_ref, qa_ref, ka_ref, carry_ref):
        i = pl.program_id(0)

        @pl.when(i == 0)
        def _():
            carry_ref[...] = jnp.zeros_like(carry_ref)

        z = fl_ref[...] + bf_ref[...]
        logf = jnp.minimum(z, 0.0) - jnp.log1p(jnp.exp(-jnp.abs(z)))
        c = jnp.dot(_tri(tc, True), logf, precision=lax.Precision.HIGHEST,
                    preferred_element_type=F32) + carry_ref[0:1, :]
        carry_ref[0:1, :] = c[tc - 1:tc, :]

        cs = c * (HEAD ** 0.5)
        hi = cs.astype(BF16).astype(F32)
        r1 = cs - hi
        mid = r1.astype(BF16).astype(F32)
        lo = r1 - mid
        lane = lax.broadcasted_iota(jnp.int32, (tc, LANE), 1)
        is_hi = (lane == 0) | (lane == 3)
        is_mid = (lane == 1) | (lane == 4)
        for h in range(N_FOX):
            col = lambda v: jnp.sum(jnp.where(lane == h, v, 0.0), axis=1, keepdims=True)
            pieces = jnp.where(is_hi, col(hi), jnp.where(is_mid, col(mid), col(lo)))
            qa_ref[h] = jnp.where(lane < 3, pieces, jnp.where(lane < 6, 1.0, 0.0)).astype(BF16)
            ka_ref[h] = jnp.where(lane < 3, 1.0, jnp.where(lane < 6, -pieces, 0.0)).astype(BF16)

    out = pl.BlockSpec((N_FOX, tc, HEAD), lambda i: (0, i, 0))
    return pl.pallas_call(
        body, name=name, grid=(S // tc,),
        in_specs=[pl.BlockSpec((tc, LANE), lambda i: (i, 0)), pl.BlockSpec((1, LANE), lambda i: (0, 0))],
        out_specs=[out, out],
        out_shape=[jax.ShapeDtypeStruct((N_FOX, S, HEAD), BF16)] * 2,
        scratch_shapes=[pltpu.VMEM((8, LANE), F32)],
        compiler_params=_params(("arbitrary",)),
    )(pfl, bf)


def _fox_dlogit(dc, pfl, bf, *, tc, name):
    S = pfl.shape[0]
    nt = S // tc

    def body(dc_ref, fl_ref, bf_ref, dfl_ref, dbf_ref, carry_ref):
        i = pl.program_id(0)

        @pl.when(i == 0)
        def _():
            carry_ref[...] = jnp.zeros_like(carry_ref)
            dbf_ref[...] = jnp.zeros_like(dbf_ref)

        dlogf = jnp.dot(_tri(tc, False), dc_ref[...], precision=lax.Precision.HIGHEST,
                        preferred_element_type=F32) + carry_ref[0:1, :]
        carry_ref[0:1, :] = dlogf[0:1, :]
        dz = dlogf * _sigmoid(-(fl_ref[...] + bf_ref[...]))
        dfl_ref[...] = dz.astype(BF16)
        dbf_ref[0:1, :] += jnp.sum(dz, axis=0, keepdims=True)

    rev = pl.BlockSpec((tc, LANE), lambda i: (nt - 1 - i, 0))
    return pl.pallas_call(
        body, name=name, grid=(nt,),
        in_specs=[rev, rev, pl.BlockSpec((1, LANE), lambda i: (0, 0))],
        out_specs=[rev, pl.BlockSpec((8, LANE), lambda i: (0, 0))],
        out_shape=[jax.ShapeDtypeStruct((S, LANE), BF16), jax.ShapeDtypeStruct((8, LANE), F32)],
        scratch_shapes=[pltpu.VMEM((8, LANE), F32)],
        compiler_params=_params(("arbitrary",)),
    )(dc, pfl, bf)


def _as_row(col):
    return jnp.transpose(jnp.broadcast_to(col, (col.shape[0], LANE)))[0:1, :]


def _causal_part(rows, cols, row0, col0, q_is_row=True):
    r = lax.broadcasted_iota(jnp.int32, (rows, cols), 0) + row0
    c = lax.broadcasted_iota(jnp.int32, (rows, cols), 1) + col0
    return (r >= c) if q_is_row else (c >= r)


LOG2E = 1.4426950408889634
FOX_SCALE2 = (HEAD ** -0.5) * LOG2E


def _fox_fwd(qkv, pg, qa, ka, *, tq, nsub, name):
    S = qkv.shape[0]
    nq = S // tq
    rs = tq // nsub

    def body(q_ref, qa_ref, k_ref, ka_ref, v_ref, g_ref, y_ref, o_ref, lse_ref):
        i = pl.program_id(1)
        q = jnp.concatenate([q_ref[...], qa_ref[...]], axis=1)

        qs = [q[r * rs:(r + 1) * rs] for r in range(nsub)]

        def blk(j, carry, masked):
            off = pl.multiple_of(j * tq, tq)
            kj = jnp.concatenate([k_ref[pl.ds(off, tq), :], ka_ref[pl.ds(off, tq), :]], axis=1)
            vj = v_ref[pl.ds(off, tq), :]
            out = []
            for r in range(nsub):
                m, l, acc = carry[r]
                s = lax.dot_general(qs[r], kj, NT, preferred_element_type=F32) * FOX_SCALE2
                if masked:
                    s = jnp.where(_causal_part(rs, tq, r * rs, 0), s, NEG)
                m_new = jnp.maximum(m, jnp.max(s, axis=-1, keepdims=True))
                alpha = jnp.exp2(m - m_new)
                p = jnp.exp2(s - m_new)
                l = alpha * l + jnp.sum(p, axis=-1, keepdims=True)
                p_hi = p.astype(BF16)
                p_lo = (p - p_hi.astype(F32)).astype(BF16)
                acc = (alpha * acc + jnp.dot(p_hi, vj, preferred_element_type=F32)
                       + jnp.dot(p_lo, vj, preferred_element_type=F32))
                out.append((m_new, l, acc))
            return tuple(out)

        init = tuple((jnp.full((rs, 1), NEG, F32), jnp.zeros((rs, 1), F32), jnp.zeros((rs, HEAD), F32))
                     for _ in range(nsub))
        carry = lax.fori_loop(0, i, lambda j, c: blk(j, c, False), init)
        carry = blk(i, carry, True)
        m = jnp.concatenate([c[0] for c in carry], axis=0)
        l = jnp.concatenate([c[1] for c in carry], axis=0)
        acc = jnp.concatenate([c[2] for c in carry], axis=0)
        o = acc / l
        g = g_ref[...]
        y_ref[...] = (o * g * _sigmoid(g)).astype(BF16)
        o_ref[...] = o
        lse_ref[...] = _as_row(m + jnp.log2(l))

    tile = lambda base: pl.BlockSpec((tq, HEAD), lambda h, i, base=base: (i, base + h))
    full = lambda base: pl.BlockSpec((S, HEAD), lambda h, i, base=base: (0, base + h))
    rowv = pl.BlockSpec((None, None, 1, tq), lambda h, i: (h, i, 0, 0))
    return pl.pallas_call(
        body, name=name, grid=(N_FOX, nq),
        in_specs=[tile(0), pl.BlockSpec((None, tq, HEAD), lambda h, i: (h, i, 0)),
                  full(N_FOX), pl.BlockSpec((None, S, HEAD), lambda h, i: (h, 0, 0)),
                  full(2 * N_FOX), tile(0)],
        out_specs=[tile(0), tile(0), rowv],
        out_shape=[jax.ShapeDtypeStruct((S, FOX_W), BF16), jax.ShapeDtypeStruct((S, FOX_W), F32),
                   jax.ShapeDtypeStruct((N_FOX, nq, 1, tq), F32)],
        compiler_params=_params(("parallel", "arbitrary")),
    )(qkv, qa, qkv, ka, qkv, pg)


def _fox_bwd_prep(dy, o, pg, *, tq, name):
    S = o.shape[0]
    base = CONV_W // HEAD

    def body(dy_ref, o_ref, g_ref, do_ref, dg_ref, dl_ref):
        g = g_ref[...]
        sg = _sigmoid(g)
        dyv = dy_ref[...]
        ov = o_ref[...]
        dob = (dyv * g * sg).astype(BF16)
        do_ref[...] = dob
        dg_ref[...] = (dyv * ov * _dsilu(g, sg)).astype(BF16)
        dl_ref[...] = _as_row(jnp.sum(dob.astype(F32) * ov, axis=-1, keepdims=True))

    tile = lambda b: pl.BlockSpec((tq, HEAD), lambda h, i, b=b: (i, b + h))
    return pl.pallas_call(
        body, name=name, grid=(N_FOX, S // tq),
        in_specs=[tile(base), tile(0), tile(0)],
        out_specs=[tile(0), tile(0), pl.BlockSpec((None, None, 1, tq), lambda h, i: (h, i, 0, 0))],
        out_shape=[jax.ShapeDtypeStruct((S, FOX_W), BF16), jax.ShapeDtypeStruct((S, FOX_W), BF16),
                   jax.ShapeDtypeStruct((N_FOX, S // tq, 1, tq), F32)],
        compiler_params=_params(("parallel", "parallel")),
    )(dy, o, pg)


def _fox_bwd(qkv, do, qa, ka, lse_row, delta_row, *, tq, nsub, name):
    S = qkv.shape[0]
    nq = S // tq
    cs = tq // nsub
    scale = HEAD ** -0.5

    def body(k_ref, ka_ref, v_ref, q_ref, qa_ref, do_ref, lser_ref, dlr_ref, dq_ref, dk_ref, dv_ref, dc_ref):
        j = pl.program_id(1)

        @pl.when(j == 0)
        def _():
            dq_ref[...] = jnp.zeros_like(dq_ref)

        kj = k_ref[...]
        kja = jnp.concatenate([kj, ka_ref[...]], axis=1)
        vj = v_ref[...]

        def blk(i, carry, masked):
            dk, dv, dc = carry
            lse_i = lser_ref[i]
            dl_i = dlr_ref[i]
            for c in range(nsub):
                off = pl.multiple_of(i * tq + c * cs, cs)
                qi = q_ref[pl.ds(off, cs), :]
                qia = jnp.concatenate([qi, qa_ref[pl.ds(off, cs), :]], axis=1)
                doi = do_ref[pl.ds(off, cs), :]
                st = lax.dot_general(kja, qia, NT, preferred_element_type=F32) * FOX_SCALE2
                if masked:
                    st = jnp.where(_causal_part(tq, cs, 0, c * cs, q_is_row=False), st, NEG)
                pt = jnp.exp2(st - lse_i[:, c * cs:(c + 1) * cs])
                dv = dv + jnp.dot(pt.astype(BF16), doi, preferred_element_type=F32)
                dpt = lax.dot_general(vj, doi, NT, preferred_element_type=F32)
                dst = pt * (dpt - dl_i[:, c * cs:(c + 1) * cs])
                dsb = dst.astype(BF16)
                dk = dk + jnp.dot(dsb, qi, preferred_element_type=F32)
                dq_ref[pl.ds(off, cs), :] += lax.dot_general(dsb, kj, TN, preferred_element_type=F32) * scale
                dc = dc - jnp.sum(dst, axis=-1, keepdims=True)
            return dk, dv, dc

        init = (jnp.zeros((tq, HEAD), F32), jnp.zeros((tq, HEAD), F32), jnp.zeros((tq, 1), F32))
        carry = blk(j, init, True)
        dk, dv, dc = lax.fori_loop(j + 1, nq, lambda i, c: blk(i, c, False), carry)
        dk_ref[...] = (dk * scale).astype(BF16)
        dv_ref[...] = dv.astype(BF16)
        dc_ref[...] = _as_row(dc)

    tile = lambda base: pl.BlockSpec((tq, HEAD), lambda h, j, base=base: (j, base + h))
    full = lambda base: pl.BlockSpec((S, HEAD), lambda h, j, base=base: (0, base + h))
    atile = pl.BlockSpec((None, tq, HEAD), lambda h, j: (h, j, 0))
    afull = pl.BlockSpec((None, S, HEAD), lambda h, j: (h, 0, 0))
    rowt = pl.BlockSpec((None, None, 1, tq), lambda h, j: (h, j, 0, 0))
    rowv = pl.BlockSpec((None, nq, 1, tq), lambda h, j: (h, 0, 0, 0))
    return pl.pallas_call(
        body, name=name, grid=(N_FOX, nq),
        in_specs=[tile(N_FOX), atile, tile(2 * N_FOX), full(0), afull, full(0), rowv, rowv],
        out_specs=[full(0), tile(0), tile(0), rowt],
        out_shape=[jax.ShapeDtypeStruct((S, FOX_W), F32), jax.ShapeDtypeStruct((S, FOX_W), BF16),
                   jax.ShapeDtypeStruct((S, FOX_W), BF16), jax.ShapeDtypeStruct((N_FOX, nq, 1, tq), F32)],
        compiler_params=_params(("arbitrary", "arbitrary")),
    )(qkv, ka, qkv, qkv, qa, do, lse_row, delta_row)


def _mem_heads(mq, mkv, h):
    lo = h * HEAD
    qh = mq[:, lo:lo + HEAD].astype(BF16)
    kh = mkv[:, lo:lo + HEAD].astype(BF16)
    vh = mkv[:, MEM_W + lo:MEM_W + lo + HEAD].astype(BF16)
    return qh, kh, vh


def _mem_softmax(qh, kh):
    s = lax.dot_general(qh, kh, NT, preferred_element_type=F32) * (HEAD ** -0.5)
    e = jnp.exp(s - jnp.max(s, axis=-1, keepdims=True))
    return e / jnp.sum(e, axis=-1, keepdims=True)


def _mem_fwd(pg, mkv, *, tq, name):
    S = pg.shape[0]
    M = mkv.shape[0]
    qb = FOX_W // MEM_W

    def body(mq_ref, g_ref, mkv_ref, y_ref):
        mq = mq_ref[...]
        mkvv = mkv_ref[...]
        for h in range(N_MEMH):
            qh, kh, vh = _mem_heads(mq, mkvv, h)
            p = _mem_softmax(qh, kh)
            o = jnp.dot(p.astype(BF16), vh, preferred_element_type=F32)
            g = g_ref[:, h * HEAD:(h + 1) * HEAD]
            y_ref[:, h * HEAD:(h + 1) * HEAD] = (o * g * _sigmoid(g)).astype(BF16)

    return pl.pallas_call(
        body, name=name, grid=(S // tq,),
        in_specs=[pl.BlockSpec((tq, MEM_W), lambda i: (i, qb)), pl.BlockSpec((tq, MEM_W), lambda i: (i, qb + 1)),
                  pl.BlockSpec((M, 2 * MEM_W), lambda i: (0, 0))],
        out_specs=pl.BlockSpec((tq, MEM_W), lambda i: (i, 0)),
        out_shape=jax.ShapeDtypeStruct((S, MEM_W), BF16),
        compiler_params=_params(("parallel",)),
    )(pg, pg, mkv)


def _mem_bwd(pg, mkv, dy, *, tq, name):
    S = pg.shape[0]
    M = mkv.shape[0]
    qb = FOX_W // MEM_W
    yb = (CONV_W + FOX_W) // MEM_W
    scale = HEAD ** -0.5

    def body(mq_ref, g_ref, mkv_ref, dy_ref, dmq_ref, dg_ref, dmkv_ref):
        i = pl.program_id(0)

        @pl.when(i == 0)
        def _():
            dmkv_ref[...] = jnp.zeros_like(dmkv_ref)

        mq = mq_ref[...]
        mkvv = mkv_ref[...]
        for h in range(N_MEMH):
            lo = h * HEAD
            qh, kh, vh = _mem_heads(mq, mkvv, h)
            p = _mem_softmax(qh, kh)
            o = jnp.dot(p.astype(BF16), vh, preferred_element_type=F32)
            g = g_ref[:, lo:lo + HEAD]
            sg = _sigmoid(g)
            dyh = dy_ref[:, lo:lo + HEAD]
            do = dyh * g * sg
            dg_ref[:, lo:lo + HEAD] = (dyh * o * _dsilu(g, sg)).astype(BF16)
            dob = do.astype(BF16)
            dp = lax.dot_general(dob, vh, NT, preferred_element_type=F32)
            ds = p * (dp - jnp.sum(do * o, axis=-1, keepdims=True))
            dsb = ds.astype(BF16)
            dmq_ref[:, lo:lo + HEAD] = (jnp.dot(dsb, kh, preferred_element_type=F32) * scale).astype(BF16)
            dmkv_ref[:, lo:lo + HEAD] += lax.dot_general(dsb, qh, TN, preferred_element_type=F32) * scale
            dmkv_ref[:, MEM_W + lo:MEM_W + lo + HEAD] += lax.dot_general(
                p.astype(BF16), dob, TN, preferred_element_type=F32)

    return pl.pallas_call(
        body, name=name, grid=(S // tq,),
        in_specs=[pl.BlockSpec((tq, MEM_W), lambda i: (i, qb)), pl.BlockSpec((tq, MEM_W), lambda i: (i, qb + 1)),
                  pl.BlockSpec((M, 2 * MEM_W), lambda i: (0, 0)), pl.BlockSpec((tq, MEM_W), lambda i: (i, yb))],
        out_specs=[pl.BlockSpec((tq, MEM_W), lambda i: (i, 0)), pl.BlockSpec((tq, MEM_W), lambda i: (i, 0)),
                   pl.BlockSpec((M, 2 * MEM_W), lambda i: (0, 0))],
        out_shape=[jax.ShapeDtypeStruct((S, MEM_W), BF16), jax.ShapeDtypeStruct((S, MEM_W), BF16),
                   jax.ShapeDtypeStruct((M, 2 * MEM_W), F32)],
        compiler_params=_params(("arbitrary",)),
    )(pg, pg, mkv, dy)


def _final(x2, target, fg, *, tm, name):
    S, D = x2.shape

    def body(x_ref, t_ref, g_ref, dx_ref, dg_ref, ls_ref):
        i = pl.program_id(0)

        @pl.when(i == 0)
        def _():
            dg_ref[...] = jnp.zeros_like(dg_ref)
            ls_ref[...] = jnp.zeros_like(ls_ref)

        xv = x_ref[...]
        r = lax.rsqrt(jnp.mean(xv * xv, axis=-1, keepdims=True) + EPS)
        n = xv * r
        g = g_ref[...]
        diff = n * g - t_ref[...]
        ls_ref[...] += jnp.sum(diff * diff)
        dout = diff * (1.0 / D)
        dg_ref[...] += jnp.sum(dout * n, axis=0, keepdims=True)
        dn = dout * g
        dx_ref[...] = r * (dn - n * jnp.mean(dn * n, axis=-1, keepdims=True))

    row = pl.BlockSpec((tm, D), lambda i: (i, 0))
    vec = pl.BlockSpec((1, D), lambda i: (0, 0))
    return pl.pallas_call(
        body, name=name, grid=(S // tm,),
        in_specs=[row, row, vec],
        out_specs=[row, vec, pl.BlockSpec((8, LANE), lambda i: (0, 0))],
        out_shape=[jax.ShapeDtypeStruct((S, D), F32), jax.ShapeDtypeStruct((1, D), F32),
                   jax.ShapeDtypeStruct((8, LANE), F32)],
        compiler_params=_params(("arbitrary",)),
    )(x2, target, fg)


def _adamw(w, g, m, v, *, name):
    R, C = w.shape
    tr, tc = R, C
    for cand in (256, 128, 64, 32, 16, 8):
        if R % cand == 0 and R > cand:
            tr = cand
            break
    if tr == R and R > 256 and C % 256 == 0:
        tc = 256
    c1 = 1.0 - ADAM_B1 ** ADAM_STEP
    c2 = 1.0 - ADAM_B2 ** ADAM_STEP

    def body(w_ref, g_ref, m_ref, v_ref, d_ref, nm_ref, nv_ref):
        gv = g_ref[...]
        nm = ADAM_B1 * m_ref[...] + (1.0 - ADAM_B1) * gv
        nv = ADAM_B2 * v_ref[...] + (1.0 - ADAM_B2) * (gv * gv)
        nm_ref[...] = nm
        nv_ref[...] = nv
        d_ref[...] = -ADAM_LR * ((nm / c1) / (jnp.sqrt(nv / c2) + ADAM_EPS) + ADAM_WD * w_ref[...])

    spec = pl.BlockSpec((tr, tc), lambda i, j: (i, j))
    shp = jax.ShapeDtypeStruct((R, C), F32)
    return pl.pallas_call(
        body, name=name, grid=(R // tr, C // tc),
        in_specs=[spec] * 4, out_specs=[spec] * 3, out_shape=[shp] * 3,
        compiler_params=_params(("parallel", "parallel")),
    )(w, g, m, v)


def _sum4(q, own, me, *, name):
    _, R, C = q.shape
    tr = R
    for cand in (256, 128, 64, 32, 16, 8):
        if R % cand == 0 and R > cand:
            tr = cand
            break

    def body(me_ref, own_ref, q1_ref, q2_ref, q3_ref, o_ref):
        f = lambda r: r[...].astype(F32)
        o_ref[...] = ((f(own_ref) + f(q1_ref)) + f(q2_ref)) + f(q3_ref)

    blk = lambda d: pl.BlockSpec((None, tr, C), lambda i, me_ref, d=d: (me_ref[0] ^ d, i, 0))
    return pl.pallas_call(
        body, name=name,
        grid_spec=pltpu.PrefetchScalarGridSpec(
            num_scalar_prefetch=1, grid=(R // tr,),
            in_specs=[blk(0), blk(1), blk(2), blk(3)],
            out_specs=pl.BlockSpec((tr, C), lambda i, me_ref: (i, 0))),
        out_shape=jax.ShapeDtypeStruct((R, C), F32),
        compiler_params=_params(("parallel",)),
    )(jnp.reshape(me, (1,)).astype(jnp.int32), own, q, q, q)


def _add_sibling(g, got, c, *, name):
    K, _, R, C = g.shape
    tr = R
    for cand in (256, 128, 64, 32, 16, 8):
        if R % cand == 0 and R > cand:
            tr = cand
            break

    def body(c_ref, a_ref, b_ref, o_ref):
        o_ref[...] = (a_ref[...] + b_ref[...]).astype(BF16)

    spec = pl.BlockSpec((None, tr, C), lambda k, i, c_ref: (k, i, 0))
    return pl.pallas_call(
        body, name=name,
        grid_spec=pltpu.PrefetchScalarGridSpec(
            num_scalar_prefetch=1, grid=(K, R // tr),
            in_specs=[pl.BlockSpec((None, None, tr, C), lambda k, i, c_ref: (k, c_ref[0], i, 0)), spec],
            out_specs=spec),
        out_shape=jax.ShapeDtypeStruct((K, R, C), BF16),
        compiler_params=_params(("parallel", "parallel")),
    )(jnp.reshape(c, (1,)).astype(jnp.int32), g, got)


ANY = pl.BlockSpec(memory_space=pl.ANY)


def _other_chips(x, y):
    return [(d, 1 - x if d & 2 else x, 1 - y if d & 1 else y) for d in (1, 2, 3)]


def _weight_all_gather(shards, *, name):
    n = len(shards)

    def body(*refs):
        src_refs, out_refs = refs[:n], refs[n:2 * n]
        ici_send, ici_recv, d2d_send, d2d_recv = refs[2 * n:]
        x, y, c = lax.axis_index("x"), lax.axis_index("y"), lax.axis_index("c")
        me = 2 * x + y
        sends = []
        for t in range(n):
            for d, tx, ty in _other_chips(x, y):
                cp = pltpu.make_async_remote_copy(
                    src_ref=src_refs[t].at[c], dst_ref=out_refs[t].at[me, c],
                    send_sem=ici_send.at[3 * t + d - 1], recv_sem=ici_recv.at[3 * t + d - 1],
                    device_id=(tx, ty, c), device_id_type=MESH)
                cp.start()
                sends.append(cp)
        for t in range(n):
            for d, tx, ty in _other_chips(x, y):
                k = 2 * tx + ty
                landed = out_refs[t].at[k, c]
                pltpu.make_async_remote_copy(
                    src_ref=landed, dst_ref=landed,
                    send_sem=ici_send.at[3 * t + d - 1], recv_sem=ici_recv.at[3 * t + d - 1],
                    device_id=(tx, ty, c), device_id_type=MESH).wait_recv()
                cp = pltpu.make_async_remote_copy(
                    src_ref=landed, dst_ref=landed,
                    send_sem=d2d_send.at[3 * t + d - 1], recv_sem=d2d_recv.at[3 * t + d - 1],
                    device_id=(x, y, 1 - c), device_id_type=MESH)
                cp.start()
                sends.append(cp)
        for t in range(n):
            for d, tx, ty in _other_chips(x, y):
                theirs = out_refs[t].at[2 * tx + ty, 1 - c]
                pltpu.make_async_remote_copy(
                    src_ref=theirs, dst_ref=theirs,
                    send_sem=d2d_send.at[3 * t + d - 1], recv_sem=d2d_recv.at[3 * t + d - 1],
                    device_id=(x, y, 1 - c), device_id_type=MESH).wait_recv()
        for cp in sends:
            cp.wait_send()

    sem = pltpu.SemaphoreType.DMA((3 * n,))
    return pl.pallas_call(
        body, name=name,
        in_specs=[ANY] * n, out_specs=[ANY] * n,
        out_shape=[jax.ShapeDtypeStruct((N_CHIPS,) + s.shape, s.dtype) for s in shards],
        scratch_shapes=[sem, sem, sem, sem],
    )(*shards)


def _scatter_copies(src_refs, out_refs, send_sems, recv_sems):
    n = len(src_refs)
    x, y, c = lax.axis_index("x"), lax.axis_index("y"), lax.axis_index("c")
    me = 2 * x + y
    copies = []
    for t in range(n):
        for d, tx, ty in _other_chips(x, y):
            copies.append(pltpu.make_async_remote_copy(
                src_ref=src_refs[t].at[2 * tx + ty], dst_ref=out_refs[t].at[me],
                send_sem=send_sems.at[3 * t + d - 1], recv_sem=recv_sems.at[3 * t + d - 1],
                device_id=(tx, ty, c), device_id_type=MESH))
    return copies


def _mm_scatter(a, b, add, srcs, *, tm, tn, tk, name):
    M, K = a.shape
    N = b.shape[1]
    tm, tn, tk = min(tm, M), min(tn, N), min(tk, K)
    assert M % tm == 0 and N % tn == 0 and K % tk == 0, (name, M, N, K, tm, tn, tk)
    gi, gj, gk = M // tm, N // tn, K // tk
    n = len(srcs)

    def body(*refs):
        a_ref, b_ref, add_ref = refs[:3]
        src_refs = refs[3:3 + n]
        o_ref = refs[3 + n]
        out_refs = refs[4 + n:4 + 2 * n]
        acc_ref, send_sems, recv_sems = refs[4 + 2 * n:]
        i, j, k = pl.program_id(0), pl.program_id(1), pl.program_id(2)

        @pl.when((i == 0) & (j == 0) & (k == 0))
        def _():
            for cp in _scatter_copies(src_refs, out_refs, send_sems, recv_sems):
                cp.start()

        @pl.when(k == 0)
        def _():
            acc_ref[...] = jnp.zeros_like(acc_ref)

        acc_ref[...] += jnp.dot(a_ref[...].astype(BF16), b_ref[...].astype(BF16), preferred_element_type=F32)

        @pl.when(k == gk - 1)
        def _():
            o_ref[...] = acc_ref[...] + add_ref[...]

        @pl.when((i == gi - 1) & (j == gj - 1) & (k == gk - 1))
        def _():
            for cp in _scatter_copies(src_refs, out_refs, send_sems, recv_sems):
                cp.wait()

    o_spec = pl.BlockSpec((tm, tn), lambda i, j, k: (i, j))
    sem = pltpu.SemaphoreType.DMA((3 * n,))
    res = pl.pallas_call(
        body, name=name, grid=(gi, gj, gk),
        in_specs=[pl.BlockSpec((tm, tk), lambda i, j, k: (i, k)), pl.BlockSpec((tk, tn), lambda i, j, k: (k, j)),
                  o_spec] + [ANY] * n,
        out_specs=[o_spec] + [ANY] * n,
        out_shape=[jax.ShapeDtypeStruct((M, N), F32)] + [jax.ShapeDtypeStruct(s.shape, s.dtype) for s in srcs],
        scratch_shapes=[pltpu.VMEM((tm, tn), F32), sem, sem],
        compiler_params=_params(("arbitrary", "arbitrary", "arbitrary")),
    )(a, b, add, *srcs)
    return res[0], res[1:]


def _swap_halves(grads, *, name):
    n = len(grads)

    def body(*refs):
        src_refs, out_refs = refs[:n], refs[n:2 * n]
        send_sems, recv_sems = refs[2 * n:]
        x, y, c = lax.axis_index("x"), lax.axis_index("y"), lax.axis_index("c")
        copies = []
        for t in range(n):
            for k in range(N_CHIPS):
                cp = pltpu.make_async_remote_copy(
                    src_ref=src_refs[t].at[k, 1 - c], dst_ref=out_refs[t].at[k],
                    send_sem=send_sems.at[N_CHIPS * t + k], recv_sem=recv_sems.at[N_CHIPS * t + k],
                    device_id=(x, y, 1 - c), device_id_type=MESH)
                cp.start()
                copies.append(cp)
        for cp in copies:
            cp.wait()

    sem = pltpu.SemaphoreType.DMA((N_CHIPS * n,))
    return pl.pallas_call(
        body, name=name,
        in_specs=[ANY] * n, out_specs=[ANY] * n,
        out_shape=[jax.ShapeDtypeStruct((N_CHIPS,) + g.shape[2:], g.dtype) for g in grads],
        scratch_shapes=[sem, sem],
    )(*grads)


def _sibling_swap(srcs, *, name):
    n = len(srcs)

    def body(*refs):
        src_refs, out_refs = refs[:n], refs[n:2 * n]
        send_sems, recv_sems = refs[2 * n:]
        x, y, c = lax.axis_index("x"), lax.axis_index("y"), lax.axis_index("c")
        copies = []
        for t in range(n):
            cp = pltpu.make_async_remote_copy(
                src_ref=src_refs[t], dst_ref=out_refs[t],
                send_sem=send_sems.at[t], recv_sem=recv_sems.at[t],
                device_id=(x, y, 1 - c), device_id_type=MESH)
            cp.start()
            copies.append(cp)
        for cp in copies:
            cp.wait()

    sem = pltpu.SemaphoreType.DMA((n,))
    return pl.pallas_call(
        body, name=name,
        in_specs=[ANY] * n, out_specs=[ANY] * n,
        out_shape=[jax.ShapeDtypeStruct(s.shape, s.dtype) for s in srcs],
        scratch_shapes=[sem, sem],
    )(*srcs)


def _allreduce_small(v, *, name):
    R, C = v.shape

    def body(v_ref, o_ref, buf_ref, send_sems, recv_sems):
        x, y, c = lax.axis_index("x"), lax.axis_index("y"), lax.axis_index("c")
        me = 4 * x + 2 * y + c
        buf_ref[me] = v_ref[...]
        copies = []
        for d in range(1, N_DEV):
            tx = 1 - x if d & 4 else x
            ty = 1 - y if d & 2 else y
            tc = 1 - c if d & 1 else c
            cp = pltpu.make_async_remote_copy(
                src_ref=v_ref, dst_ref=buf_ref.at[me],
                send_sem=send_sems.at[d - 1], recv_sem=recv_sems.at[d - 1],
                device_id=(tx, ty, tc), device_id_type=MESH)
            cp.start()
            copies.append(cp)
        for cp in copies:
            cp.wait()
        acc = buf_ref[0]
        for k in range(1, N_DEV):
            acc = acc + buf_ref[k]
        o_ref[...] = acc

    return pl.pallas_call(
        body, name=name,
        in_specs=[pl.BlockSpec(memory_space=pltpu.VMEM)],
        out_specs=pl.BlockSpec(memory_space=pltpu.VMEM),
        out_shape=jax.ShapeDtypeStruct((R, C), F32),
        scratch_shapes=[pltpu.VMEM((N_DEV, R, C), F32), pltpu.SemaphoreType.DMA((N_DEV - 1,)),
                        pltpu.SemaphoreType.DMA((N_DEV - 1,))],
    )(v)


_A0, _B0, _GC0 = 0, CONV_W, 2 * CONV_W
_Q0 = 3 * CONV_W
_FL0 = _Q0 + 3 * FOX_W
_FG0 = _FL0 + N_FOX
_MQ0 = _FG0 + FOX_W
_MG0 = _MQ0 + MEM_W
_DIN = _MG0 + MEM_W
_WMAIN = _DIN - N_FOX
_PC_N = 3 * CONV_W
_QKV_N = 3 * FOX_W
_PG_N = FOX_W + 2 * MEM_W


def _rows_pad8(a):
    r = a.shape[0]
    p = (-r) % 8
    return jnp.pad(a, ((0, p), (0, 0))) if p else a


def _step(x, mem, target, norm_g, mem_norm_g, final_g, b_f, cw_pad, conv_b, ln_g, ln_b,
          w_in, w_pw, w_mkv, w_out, ci, me):
    S, D = x.shape
    M = mem.shape[0]
    tq = min(512, S)
    tf = min(1024, S)
    tt = min(256, S)
    tm = min(256, S)

    w_main = jnp.concatenate([w_in[:, :_FL0], w_in[:, _FG0:]], axis=1)
    w_fl = jnp.pad(w_in[:, _FL0:_FG0], ((0, 0), (0, LANE - N_FOX)))
    w_main_t, w_fl_t, w_out_t = w_main.T, w_fl.T, w_out.T
    bf_pad = jnp.pad(b_f, ((0, 0), (0, LANE - N_FOX)))

    h = _rms_fwd(x, norm_g, tm=tm, name="rms_fwd")
    h_t = h.T
    pc = _mm(h, w_main, tm=1024, tn=512, tk=2048, n=_PC_N, b_off=0, name="proj_conv")
    qkv = _mm(h, w_main, out_dtype=BF16, tm=1024, tn=512, tk=2048, n=_QKV_N, b_off=_PC_N // 512, name="proj_qkv")
    pg = _mm(h, w_main, tm=1024, tn=512, tk=2048, n=_PG_N, b_off=(_PC_N + _QKV_N) // 512, name="proj_gate")
    pfl = _mm(h, w_fl, tm=1024, tn=LANE, tk=2048, name="proj_logit")

    y_conv = _conv_fwd(pc, cw_pad, conv_b, ln_g, ln_b, w_pw, tt=tt, name="conv_fwd")

    qa, ka = _fox_cumsum(pfl, bf_pad, tc=tt, name="fox_cumsum")
    y_fox, o_fox, lse = _fox_fwd(qkv, pg, qa, ka, tq=tf, nsub=2, name="fox_fwd")

    hm = _rms_fwd(mem, mem_norm_g, tm=min(256, M), name="rms_mem")
    mkv = _mm(hm, w_mkv, tm=256, tn=512, tk=2048, name="mem_kv")
    y_mem = _mem_fwd(pg, mkv, tq=tq, name="mem_fwd")

    y = jnp.concatenate([y_conv, y_fox, y_mem], axis=1)
    x2 = _mm(y, w_out, add=x, tm=1024, tn=512, tk=2048, name="out_proj")
    dx2, dfg, sq = _final(x2, target, final_g.reshape(1, D), tm=tm, name="final")

    dy = _mm(dx2, w_out_t, tm=512, tn=512, tk=2048, name="d_y")
    dw_out = _mm(y.T, dx2, tm=1024, tn=512, tk=1024, name="d_w_out")

    dpc, dw_pw, dcw, dsm = _conv_bwd(pc, dy, cw_pad, conv_b, ln_g, ln_b, w_pw, tt=tt, name="conv_bwd")

    do, dfgate, delta = _fox_bwd_prep(dy, o_fox, pg, tq=tf, name="fox_bwd_prep")
    dq, dk, dv, dc = _fox_bwd(qkv, do, qa, ka, lse, delta, tq=tf, nsub=2, name="fox_bwd")
    dc_pad = jnp.pad(dc.reshape(N_FOX, S).T, ((0, 0), (0, LANE - N_FOX)))
    dfl, dbf = _fox_dlogit(dc_pad, pfl, bf_pad, tc=tt, name="fox_dlogit")

    dmq, dmgate, dmkv = _mem_bwd(pg, mkv, dy, tq=tq, name="mem_bwd")
    dw_mkv = _mm(hm, dmkv, ta=True, tm=512, tn=512, tk=256, name="d_w_mkv")
    dhm = _mm(dmkv, w_mkv, tb=True, tm=256, tn=512, tk=1024, name="d_hm")
    _, dmg = _rms_bwd(mem, mem_norm_g, dhm, None, tm=min(256, M), name="rms_mem_bwd")

    dproj = jnp.concatenate([dpc, dq.astype(BF16), dk, dv, dfgate, dmq, dmgate], axis=1)
    dw_main = _mm(h_t, dproj, tm=1024, tn=_WMAIN // 4, tk=1024, name="d_w_main")
    dw_fl = _mm(h_t, dfl, tm=1024, tn=LANE, tk=1024, name="d_w_logit")
    dw_in = jnp.concatenate([dw_main[:, :_FL0], dw_fl[:, :N_FOX], dw_main[:, _FL0:]], axis=1)

    nin = _DIN // N_CHIPS
    big = [jnp.transpose(dw_in.reshape(D, N_CHIPS, nin), (1, 0, 2)),
           dw_pw.reshape(N_CHIPS, CONV_W // N_CHIPS, CONV_W),
           dw_mkv.reshape(N_CHIPS, D // N_CHIPS, 2 * MEM_W),
           dw_out.reshape(N_CHIPS, MIX_W // N_CHIPS, D)]
    big = [g.reshape(N_CHIPS, 2, g.shape[1] // 2, g.shape[2]) for g in big]
    got = _swap_halves(big, name="grad_swap_halves")
    chip = [_add_sibling(g, o, ci, name=f"grad_add_sibling_{t}") for t, (g, o) in enumerate(zip(big, got))]

    dh0 = _mm(dfl, w_fl_t, tm=512, tn=512, tk=LANE, name="d_h_logit")
    dh, parts = _mm_scatter(dproj, w_main_t, dh0, chip, tm=512, tn=512, tk=_WMAIN // 2, name="d_h_grad_scatter")
    grad_x, dng = _rms_bwd(x, norm_g, dh, dx2, tm=tm, name="rms_bwd")

    red = [_sum4(p, own, me, name=f"grad_sum_chips_{t}") for t, (p, own) in enumerate(zip(parts, chip))]
    other = _sibling_swap(red, name="grad_swap_result")
    full = [jnp.where(ci == 0, jnp.concatenate([r, o], axis=0), jnp.concatenate([o, r], axis=0))
            for r, o in zip(red, other)]

    small = dict(norm_g=dng, mem_norm_g=dmg, final_g=dfg, b_f=dbf[0:1, :], conv_w=dcw,
                 conv_b=dsm[0:1], conv_ln_g=dsm[1:2], conv_ln_b=dsm[2:3])
    return sq[0, 0], grad_x, full, small


_SMALL_ORDER = ("norm_g", "mem_norm_g", "final_g", "b_f", "conv_w", "conv_b", "conv_ln_g", "conv_ln_b")


def _pack_small(small):
    parts, layout = [], []
    row = 0
    for k in _SMALL_ORDER:
        p = _rows_pad8(small[k].reshape(-1, LANE))
        layout.append((k, row, small[k].shape))
        parts.append(p)
        row += p.shape[0]
    return jnp.concatenate(parts, axis=0), layout


def _unpack_small(packed, layout):
    out = {}
    for k, row, shape in layout:
        nrow = (shape[0] * shape[1]) // LANE
        out[k] = packed[row:row + nrow].reshape(shape)
    return out


def kernel(x, mem, norm_g, mem_norm_g, w_in, b_f, conv_w, conv_b, conv_ln_g, conv_ln_b, w_conv_pw, w_mem_kv, w_out, final_g, loss_target, m_norm_g, m_mem_norm_g, m_w_in, m_b_f, m_conv_w, m_conv_b, m_conv_ln_g, m_conv_ln_b, m_w_conv_pw, m_w_mem_kv, m_w_out, m_final_g, v_norm_g, v_mem_norm_g, v_w_in, v_b_f, v_conv_w, v_conv_b, v_conv_ln_g, v_conv_ln_b, v_w_conv_pw, v_w_mem_kv, v_w_out, v_final_g):
    S, D = x.shape[1], x.shape[2]
    xi, yi, ci = lax.axis_index("x"), lax.axis_index("y"), lax.axis_index("c")
    chip = 2 * xi + yi

    halves = lambda a: a.reshape(2, a.shape[0] // 2, a.shape[1])
    cw_shard = jnp.pad(conv_w[0], ((0, CONV_HALO - CONV_K), (0, 0)))
    shards = [halves(w_in[0].astype(BF16)), halves(w_conv_pw[0].astype(BF16)), halves(w_mem_kv[0].astype(BF16)),
              halves(w_out[0].astype(BF16)), halves(cw_shard)]
    gathered = _weight_all_gather(shards, name="weight_all_gather")
    g_in, g_pw, g_mkv, g_out, g_cw = [lax.dynamic_update_slice(g, s[None], (chip, 0, 0, 0))
                                      for g, s in zip(gathered, shards)]
    nin = w_in.shape[2]
    w_in_full = jnp.transpose(g_in.reshape(N_CHIPS, D, nin), (1, 0, 2)).reshape(D, N_CHIPS * nin)
    w_pw_full = g_pw.reshape(CONV_W, CONV_W)
    w_mkv_full = g_mkv.reshape(D, 2 * MEM_W)
    w_out_full = g_out.reshape(MIX_W, D)
    cw_full = jnp.transpose(g_cw.reshape(N_CHIPS, CONV_HALO, CONV_W // N_CHIPS), (1, 0, 2)).reshape(CONV_HALO, CONV_W)

    sq, grad_x, (g_w_in, g_w_pw, g_w_mkv, g_w_out), small = _step(
        x[0], mem[0], loss_target[0], norm_g, mem_norm_g, final_g, b_f, cw_full, conv_b, conv_ln_g,
        conv_ln_b, w_in_full, w_pw_full, w_mkv_full, w_out_full, ci, chip)

    loss = lax.psum(sq, ("x", "y", "c")) * (0.5 / D)

    packed, layout = _pack_small(small)
    sm = _unpack_small(_allreduce_small(packed, name="small_all_reduce"), layout)
    cshard = CONV_W // N_CHIPS
    g_conv_w = lax.dynamic_slice_in_dim(sm["conv_w"][:CONV_K], chip * cshard, cshard, axis=1)

    grads = dict(
        norm_g=sm["norm_g"], mem_norm_g=sm["mem_norm_g"], w_in=g_w_in[None], b_f=sm["b_f"][:, :N_FOX],
        conv_w=g_conv_w[None], conv_b=sm["conv_b"], conv_ln_g=sm["conv_ln_g"], conv_ln_b=sm["conv_ln_b"],
        w_conv_pw=g_w_pw[None], w_mem_kv=g_w_mkv[None], w_out=g_w_out[None], final_g=sm["final_g"].reshape(D))
    weights = dict(norm_g=norm_g, mem_norm_g=mem_norm_g, w_in=w_in, b_f=b_f, conv_w=conv_w, conv_b=conv_b,
                   conv_ln_g=conv_ln_g, conv_ln_b=conv_ln_b, w_conv_pw=w_conv_pw, w_mem_kv=w_mem_kv, w_out=w_out,
                   final_g=final_g)
    ms = dict(norm_g=m_norm_g, mem_norm_g=m_mem_norm_g, w_in=m_w_in, b_f=m_b_f, conv_w=m_conv_w, conv_b=m_conv_b,
              conv_ln_g=m_conv_ln_g, conv_ln_b=m_conv_ln_b, w_conv_pw=m_w_conv_pw, w_mem_kv=m_w_mem_kv,
              w_out=m_w_out, final_g=m_final_g)
    vs = dict(norm_g=v_norm_g, mem_norm_g=v_mem_norm_g, w_in=v_w_in, b_f=v_b_f, conv_w=v_conv_w, conv_b=v_conv_b,
              conv_ln_g=v_conv_ln_g, conv_ln_b=v_conv_ln_b, w_conv_pw=v_w_conv_pw, w_mem_kv=v_w_mem_kv,
              w_out=v_w_out, final_g=v_final_g)

    names = ("norm_g", "mem_norm_g", "w_in", "b_f", "conv_w", "conv_b", "conv_ln_g", "conv_ln_b", "w_conv_pw",
             "w_mem_kv", "w_out", "final_g")
    deltas, new_m, new_v = {}, {}, {}
    for k in names:
        shape = weights[k].shape
        if k == "w_in":
            two_d = lambda a: a.reshape(shape[-2], shape[-1]).T
            back = lambda a: a.T.reshape(shape)
        else:
            two_d = lambda a: a.reshape(-1, shape[-1])
            back = lambda a: a.reshape(shape)
        d, nm, nv = _adamw(two_d(weights[k]), two_d(grads[k]), two_d(ms[k]), two_d(vs[k]), name=f"adamw_{k}")
        deltas[k], new_m[k], new_v[k] = back(d), back(nm), back(nv)

    return (loss, grad_x[None], *[grads[k] for k in names], *[deltas[k] for k in names],
            *[new_m[k] for k in names], *[new_v[k] for k in names])
```

```python
import functools

import jax
import jax.numpy as jnp
from jax import lax
from jax.experimental import pallas as pl
from jax.experimental.pallas import tpu as pltpu

F32 = jnp.float32
BF16 = jnp.bfloat16
MESH = pl.DeviceIdType.MESH

HEAD = 128
N_FOX = 8
N_MEMH = 4
CONV_W = 512
FOX_W = N_FOX * HEAD
MEM_W = N_MEMH * HEAD
MIX_W = CONV_W + FOX_W + MEM_W
CONV_K = 31
CONV_HALO = 32
EPS = 1e-6
NEG = -1e30
LANE = 128
N_CHIPS = 4
N_DEV = 8
VMEM_LIMIT = 56 * 1024 * 1024

ADAM_LR = 0.001
ADAM_B1 = 0.9
ADAM_B2 = 0.999
ADAM_EPS = 1e-08
ADAM_WD = 0.01
ADAM_STEP = 10

NT = (((1,), (1,)), ((), ()))
TN = (((0,), (0,)), ((), ()))
NN = (((1,), (0,)), ((), ()))


def _params(sem=None):
    kw = dict(vmem_limit_bytes=VMEM_LIMIT)
    if sem is not None:
        kw["dimension_semantics"] = sem
    return pltpu.CompilerParams(**kw)


def _sigmoid(v):
    return jax.nn.sigmoid(v)


def _dsilu(v, s):
    return s * (1.0 + v * (1.0 - s))


def _rms_fwd(x, g, *, tm, name):
    R, D = x.shape

    def body(x_ref, g_ref, h_ref):
        xv = x_ref[...]
        r = lax.rsqrt(jnp.mean(xv * xv, axis=-1, keepdims=True) + EPS)
        h_ref[...] = (xv * r * g_ref[...]).astype(BF16)

    return pl.pallas_call(
        body, name=name, grid=(R // tm,),
        in_specs=[pl.BlockSpec((tm, D), lambda i: (i, 0)), pl.BlockSpec((1, D), lambda i: (0, 0))],
        out_specs=pl.BlockSpec((tm, D), lambda i: (i, 0)),
        out_shape=jax.ShapeDtypeStruct((R, D), BF16),
        compiler_params=_params(("parallel",)),
    )(x, g)


def _rms_bwd(x, g, dh, dres, *, tm, name):
    R, D = x.shape
    has_res = dres is not None

    def body(*refs):
        if has_res:
            x_ref, g_ref, dh_ref, dres_ref, dx_ref, dg_ref = refs
        else:
            x_ref, g_ref, dh_ref, dx_ref, dg_ref = refs
        i = pl.program_id(0)

        @pl.when(i == 0)
        def _():
            dg_ref[...] = jnp.zeros_like(dg_ref)

        xv = x_ref[...]
        r = lax.rsqrt(jnp.mean(xv * xv, axis=-1, keepdims=True) + EPS)
        n = xv * r
        dh = dh_ref[...]
        dg_ref[...] += jnp.sum(dh * n, axis=0, keepdims=True)
        dn = dh * g_ref[...]
        dx = r * (dn - n * jnp.mean(dn * n, axis=-1, keepdims=True))
        if has_res:
            dx = dx + dres_ref[...]
        dx_ref[...] = dx

    row = pl.BlockSpec((tm, D), lambda i: (i, 0))
    vec = pl.BlockSpec((1, D), lambda i: (0, 0))
    ins = [row, vec, row] + ([row] if has_res else [])
    args = (x, g, dh) + ((dres,) if has_res else ())
    return pl.pallas_call(
        body, name=name, grid=(R // tm,),
        in_specs=ins, out_specs=[row, vec],
        out_shape=[jax.ShapeDtypeStruct((R, D), F32), jax.ShapeDtypeStruct((1, D), F32)],
        compiler_params=_params(("arbitrary",)),
    )(*args)


def _mm(a, b, *, ta=False, tb=False, out_dtype=F32, add=None, tm, tn, tk, n=None, b_off=0, name):
    M, K = (a.shape[1], a.shape[0]) if ta else a.shape
    nb = b.shape[0] if tb else b.shape[1]
    n = nb if n is None else n
    tm, tn, tk = min(tm, M), min(tn, n), min(tk, K)
    assert M % tm == 0 and n % tn == 0 and K % tk == 0, (name, M, n, K, tm, tn, tk)
    nk = K // tk
    has_add = add is not None

    def body(*refs):
        if has_add:
            a_ref, b_ref, add_ref, o_ref, acc_ref = refs
        else:
            a_ref, b_ref, o_ref, acc_ref = refs
        k = pl.program_id(2)

        @pl.when(k == 0)
        def _():
            acc_ref[...] = jnp.zeros_like(acc_ref)

        av = a_ref[...].astype(BF16)
        bv = b_ref[...].astype(BF16)
        dims = (((0 if ta else 1,), (1 if tb else 0,)), ((), ()))
        acc_ref[...] += lax.dot_general(av, bv, dims, preferred_element_type=F32)

        @pl.when(k == nk - 1)
        def _():
            r = acc_ref[...]
            if has_add:
                r = r + add_ref[...]
            o_ref[...] = r.astype(out_dtype)

    a_spec = pl.BlockSpec((tk, tm), lambda i, j, k: (k, i)) if ta else pl.BlockSpec((tm, tk), lambda i, j, k: (i, k))
    b_spec = (pl.BlockSpec((tn, tk), lambda i, j, k: (j + b_off, k)) if tb
              else pl.BlockSpec((tk, tn), lambda i, j, k: (k, j + b_off)))
    o_spec = pl.BlockSpec((tm, tn), lambda i, j, k: (i, j))
    ins = [a_spec, b_spec] + ([o_spec] if has_add else [])
    args = (a, b) + ((add,) if has_add else ())
    return pl.pallas_call(
        body, name=name, grid=(M // tm, n // tn, nk),
        in_specs=ins, out_specs=o_spec,
        out_shape=jax.ShapeDtypeStruct((M, n), out_dtype),
        scratch_shapes=[pltpu.VMEM((tm, tn), F32)],
        compiler_params=_params(("parallel", "parallel", "arbitrary")),
    )(*args)


SUB = 8


def _shifted_copies(es_ref, rows):
    e = es_ref[0]
    for b in range(1, SUB):
        es_ref[b] = pltpu.roll(e, shift=rows - b, axis=0)


def _window(es_ref, offset, tt):
    b = offset % SUB
    return es_ref[b, offset - b:offset - b + tt, :]


def _conv_taps(w_ref, es_ref, offsets, tt):
    acc = w_ref[0:1, :] * _window(es_ref, offsets[0], tt)
    for k in range(1, CONV_K):
        acc = acc + w_ref[k:k + 1, :] * _window(es_ref, offsets[k], tt)
    return acc


def _conv_fwd(pc, cw, cb, lng, lnb, wpw, *, tt, name):
    S = pc.shape[0]
    C = CONV_W
    lead = CONV_HALO - (CONV_K - 1)
    rows = tt + CONV_HALO

    def body(a_ref, b_ref, gc_ref, cw_ref, cb_ref, lng_ref, lnb_ref, wpw_ref, y_ref, u1_ref, es_ref):
        i = pl.program_id(0)

        @pl.when(i == 0)
        def _():
            es_ref[0, 0:CONV_HALO, :] = jnp.zeros((CONV_HALO, C), F32)

        @pl.when(i > 0)
        def _():
            es_ref[0, 0:CONV_HALO, :] = es_ref[0, tt:tt + CONV_HALO, :]

        es_ref[0, CONV_HALO:CONV_HALO + tt, :] = a_ref[...] * _sigmoid(b_ref[...])
        _shifted_copies(es_ref, rows)
        u1 = _conv_taps(cw_ref, es_ref, [lead + k for k in range(CONV_K)], tt) + cb_ref[...]
        u1_ref[...] = u1
        mu = jnp.mean(u1, axis=-1, keepdims=True)
        xc = u1 - mu
        rstd = lax.rsqrt(jnp.mean(xc * xc, axis=-1, keepdims=True) + EPS)
        u2 = xc * rstd * lng_ref[...] + lnb_ref[...]
        u3 = u2 * _sigmoid(u2)
        z = jnp.dot(u3.astype(BF16), wpw_ref[...], preferred_element_type=F32)
        gc = gc_ref[...]
        y_ref[...] = (z * gc * _sigmoid(gc)).astype(BF16)

    col = lambda c: pl.BlockSpec((tt, C), lambda i, c=c: (i, c))
    vec = pl.BlockSpec((1, C), lambda i: (0, 0))
    return pl.pallas_call(
        body, name=name, grid=(S // tt,),
        in_specs=[col(0), col(1), col(2), pl.BlockSpec((CONV_HALO, C), lambda i: (0, 0)), vec, vec, vec,
                  pl.BlockSpec((C, C), lambda i: (0, 0))],
        out_specs=[pl.BlockSpec((tt, C), lambda i: (i, 0))] * 2,
        out_shape=[jax.ShapeDtypeStruct((S, C), BF16), jax.ShapeDtypeStruct((S, C), F32)],
        scratch_shapes=[pltpu.VMEM((SUB, rows, C), F32)],
        compiler_params=_params(("arbitrary",)),
    )(pc, pc, pc, cw, cb, lng, lnb, wpw)


def _conv_bwd(pc, u1s, dy, cw, lng, lnb, wpw, *, tt, name):
    S = pc.shape[0]
    C = CONV_W
    nt = S // tt
    hb = tt // CONV_HALO
    lead = CONV_HALO - (CONV_K - 1)
    rows = tt + CONV_HALO

    def body(a_ref, b_ref, gc_ref, ah_ref, bh_ref, u1_ref, dy_ref, cw_ref, lng_ref, lnb_ref, wpw_ref,
             dpc_ref, dwpw_ref, dcw_ref, dsm_ref, eu_ref, ed_ref, dcw8_ref):
        i = pl.program_id(0)
        ti = nt - 1 - i

        @pl.when(i == 0)
        def _():
            ed_ref[0, tt:tt + CONV_HALO, :] = jnp.zeros((CONV_HALO, C), F32)
            dwpw_ref[...] = jnp.zeros_like(dwpw_ref)
            dcw8_ref[...] = jnp.zeros_like(dcw8_ref)
            dsm_ref[...] = jnp.zeros_like(dsm_ref)

        @pl.when(i > 0)
        def _():
            ed_ref[0, tt:tt + CONV_HALO, :] = ed_ref[0, 0:CONV_HALO, :]

        keep = jnp.where(ti > 0, 1.0, 0.0).astype(F32)
        eu_ref[0, 0:CONV_HALO, :] = ah_ref[...] * _sigmoid(bh_ref[...]) * keep
        a = a_ref[...]
        sb = _sigmoid(b_ref[...])
        eu_ref[0, CONV_HALO:CONV_HALO + tt, :] = a * sb
        _shifted_copies(eu_ref, rows)

        u1 = u1_ref[...]
        mu = jnp.mean(u1, axis=-1, keepdims=True)
        xc = u1 - mu
        rstd = lax.rsqrt(jnp.mean(xc * xc, axis=-1, keepdims=True) + EPS)
        nhat = xc * rstd
        g = lng_ref[...]
        u2 = nhat * g + lnb_ref[...]
        s2 = _sigmoid(u2)
        u3 = (u2 * s2).astype(BF16)
        z = jnp.dot(u3, wpw_ref[...], preferred_element_type=F32)

        gc = gc_ref[...]
        sg = _sigmoid(gc)
        dyv = dy_ref[...]
        dz = (dyv * gc * sg).astype(BF16)
        dpc_ref[:, 2 * C:3 * C] = (dyv * z * _dsilu(gc, sg)).astype(BF16)

        du3 = lax.dot_general(dz, wpw_ref[...], NT, preferred_element_type=F32)
        dwpw_ref[...] += lax.dot_general(u3, dz, TN, preferred_element_type=F32)
        du2 = du3 * _dsilu(u2, s2)
        dsm_ref[1:2, :] += jnp.sum(du2 * nhat, axis=0, keepdims=True)
        dsm_ref[2:3, :] += jnp.sum(du2, axis=0, keepdims=True)
        dn = du2 * g
        du1 = rstd * (dn - jnp.mean(dn, axis=-1, keepdims=True)
                      - nhat * jnp.mean(dn * nhat, axis=-1, keepdims=True))
        dsm_ref[0:1, :] += jnp.sum(du1, axis=0, keepdims=True)
        ed_ref[0, 0:tt, :] = du1
        _shifted_copies(ed_ref, rows)

        du0 = _conv_taps(cw_ref, ed_ref, [CONV_K - 1 - k for k in range(CONV_K)], tt)
        for k in range(CONV_K):
            prod = du1 * _window(eu_ref, lead + k, tt)
            part = prod[0:SUB]
            for r in range(1, tt // SUB):
                part = part + prod[r * SUB:(r + 1) * SUB]
            dcw8_ref[k * SUB:(k + 1) * SUB, :] += part

        dpc_ref[:, 0:C] = (du0 * sb).astype(BF16)
        dpc_ref[:, C:2 * C] = (du0 * a * sb * (1.0 - sb)).astype(BF16)

        @pl.when(i == nt - 1)
        def _():
            dcw_ref[...] = jnp.zeros_like(dcw_ref)
            for k in range(CONV_K):
                dcw_ref[k:k + 1, :] = jnp.sum(dcw8_ref[k * SUB:(k + 1) * SUB, :], axis=0, keepdims=True)

    col = lambda c: pl.BlockSpec((tt, C), lambda i, c=c: (nt - 1 - i, c))
    halo = lambda c: pl.BlockSpec((CONV_HALO, C), lambda i, c=c: (jnp.maximum((nt - 1 - i) * hb - 1, 0), c))
    vec = pl.BlockSpec((1, C), lambda i: (0, 0))
    fixed = lambda r: pl.BlockSpec((r, C), lambda i: (0, 0))
    return pl.pallas_call(
        body, name=name, grid=(nt,),
        in_specs=[col(0), col(1), col(2), halo(0), halo(1),
                  pl.BlockSpec((tt, C), lambda i: (nt - 1 - i, 0)),
                  pl.BlockSpec((tt, C), lambda i: (nt - 1 - i, 0)),
                  fixed(CONV_HALO), vec, vec, fixed(C)],
        out_specs=[pl.BlockSpec((tt, 3 * C), lambda i: (nt - 1 - i, 0)), fixed(C), fixed(CONV_HALO), fixed(8)],
        out_shape=[jax.ShapeDtypeStruct((S, 3 * C), BF16), jax.ShapeDtypeStruct((C, C), F32),
                   jax.ShapeDtypeStruct((CONV_HALO, C), F32), jax.ShapeDtypeStruct((8, C), F32)],
        scratch_shapes=[pltpu.VMEM((SUB, rows, C), F32), pltpu.VMEM((SUB, rows, C), F32),
                        pltpu.VMEM((CONV_HALO * SUB, C), F32)],
        compiler_params=_params(("arbitrary",)),
    )(pc, pc, pc, pc, pc, u1s, dy, cw, lng, lnb, wpw)


def _tri(n, lower):
    r = lax.broadcasted_iota(jnp.int32, (n, n), 0)
    c = lax.broadcasted_iota(jnp.int32, (n, n), 1)
    return jnp.where((r >= c) if lower else (r <= c), 1.0, 0.0).astype(F32)


def _fox_cumsum(pfl, bf, *, tc, name):
    S = pfl.shape[0]

    def body(fl_ref, bf_ref, qa_ref, ka_ref, carry_ref):
        i = pl.program_id(0)

        @pl.when(i == 0)
        def _():
            carry_ref[...] = jnp.zeros_like(carry_ref)

        z = fl_ref[...] + bf_ref[...]
        logf = jnp.minimum(z, 0.0) - jnp.log1p(jnp.exp(-jnp.abs(z)))
        c = jnp.dot(_tri(tc, True), logf, precision=lax.Precision.HIGHEST,
                    preferred_element_type=F32) + carry_ref[0:1, :]
        carry_ref[0:1, :] = c[tc - 1:tc, :]

        cs = c * (HEAD ** 0.5)
        hi = cs.astype(BF16).astype(F32)
        r1 = cs - hi
        mid = r1.astype(BF16).astype(F32)
        lo = r1 - mid
        lane = lax.broadcasted_iota(jnp.int32, (tc, LANE), 1)
        is_hi = (lane == 0) | (lane == 3)
        is_mid = (lane == 1) | (lane == 4)
        for h in range(N_FOX):
            col = lambda v: jnp.sum(jnp.where(lane == h, v, 0.0), axis=1, keepdims=True)
            pieces = jnp.where(is_hi, col(hi), jnp.where(is_mid, col(mid), col(lo)))
            qa_ref[h] = jnp.where(lane < 3, pieces, jnp.where(lane < 6, 1.0, 0.0)).astype(BF16)
            ka_ref[h] = jnp.where(lane < 3, 1.0, jnp.where(lane < 6, -pieces, 0.0)).astype(BF16)

    out = pl.BlockSpec((N_FOX, tc, HEAD), lambda i: (0, i, 0))
    return pl.pallas_call(
        body, name=name, grid=(S // tc,),
        in_specs=[pl.BlockSpec((tc, LANE), lambda i: (i, 0)), pl.BlockSpec((1, LANE), lambda i: (0, 0))],
        out_specs=[out, out],
        out_shape=[jax.ShapeDtypeStruct((N_FOX, S, HEAD), BF16)] * 2,
        scratch_shapes=[pltpu.VMEM((8, LANE), F32)],
        compiler_params=_params(("arbitrary",)),
    )(pfl, bf)


def _fox_dlogit(dc, pfl, bf, *, tc, name):
    S = pfl.shape[0]
    nt = S // tc

    def body(dc_ref, fl_ref, bf_ref, dfl_ref, dbf_ref, carry_ref):
        i = pl.program_id(0)

        @pl.when(i == 0)
        def _():
            carry_ref[...] = jnp.zeros_like(carry_ref)
            dbf_ref[...] = jnp.zeros_like(dbf_ref)

        dlogf = jnp.dot(_tri(tc, False), dc_ref[...], precision=lax.Precision.HIGHEST,
                        preferred_element_type=F32) + carry_ref[0:1, :]
        carry_ref[0:1, :] = dlogf[0:1, :]
        dz = dlogf * _sigmoid(-(fl_ref[...] + bf_ref[...]))
        dfl_ref[...] = dz.astype(BF16)
        dbf_ref[0:1, :] += jnp.sum(dz, axis=0, keepdims=True)

    rev = pl.BlockSpec((tc, LANE), lambda i: (nt - 1 - i, 0))
    return pl.pallas_call(
        body, name=name, grid=(nt,),
        in_specs=[rev, rev, pl.BlockSpec((1, LANE), lambda i: (0, 0))],
        out_specs=[rev, pl.BlockSpec((8, LANE), lambda i: (0, 0))],
        out_shape=[jax.ShapeDtypeStruct((S, LANE), BF16), jax.ShapeDtypeStruct((8, LANE), F32)],
        scratch_shapes=[pltpu.VMEM((8, LANE), F32)],
        compiler_params=_params(("arbitrary",)),
    )(dc, pfl, bf)


def _as_row(col):
    return jnp.transpose(jnp.broadcast_to(col, (col.shape[0], LANE)))[0:1, :]


def _causal_part(rows, cols, row0, col0, q_is_row=True):
    r = lax.broadcasted_iota(jnp.int32, (rows, cols), 0) + row0
    c = lax.broadcasted_iota(jnp.int32, (rows, cols), 1) + col0
    return (r >= c) if q_is_row else (c >= r)


LOG2E = 1.4426950408889634
FOX_SCALE2 = (HEAD ** -0.5) * LOG2E


def _fox_fwd(qkv, pg, qa, ka, *, tq, nsub, name):
    S = qkv.shape[0]
    nq = S // tq
    rs = tq // nsub

    def body(q_ref, qa_ref, k_ref, ka_ref, v_ref, g_ref, y_ref, o_ref, m_ref, linv_ref):
        i = pl.program_id(1)
        q = jnp.concatenate([q_ref[...], qa_ref[...]], axis=1)
        qs = [q[r * rs:(r + 1) * rs] for r in range(nsub)]
        ones = jnp.where(lax.broadcasted_iota(jnp.int32, (tq, HEAD), 1) == 0, 1.0, 0.0).astype(BF16)

        def blk(j, carry, masked):
            off = pl.multiple_of(j * tq, tq)
            kj = jnp.concatenate([k_ref[pl.ds(off, tq), :], ka_ref[pl.ds(off, tq), :]], axis=1)
            vj = jnp.concatenate([v_ref[pl.ds(off, tq), :], ones], axis=1)
            out = []
            for r in range(nsub):
                m, acc = carry[r]
                s = lax.dot_general(qs[r], kj, NT, preferred_element_type=F32) * FOX_SCALE2
                if masked:
                    s = jnp.where(_causal_part(rs, tq, r * rs, 0), s, NEG)
                m_new = jnp.maximum(m, jnp.ceil(jnp.max(s, axis=-1, keepdims=True)))
                pb = jnp.exp2(s - m_new).astype(BF16)
                acc = jnp.exp2(m - m_new) * acc + jnp.dot(pb, vj, preferred_element_type=F32)
                out.append((m_new, acc))
            return tuple(out)

        init = tuple((jnp.full((rs, 1), NEG, F32), jnp.zeros((rs, 2 * HEAD), F32)) for _ in range(nsub))
        carry = lax.fori_loop(0, i, lambda j, c: blk(j, c, False), init)
        carry = blk(i, carry, True)
        m = jnp.concatenate([c[0] for c in carry], axis=0)
        acc = jnp.concatenate([c[1] for c in carry], axis=0)
        linv = 1.0 / acc[:, HEAD:HEAD + 1]
        o = acc[:, :HEAD] * linv
        g = g_ref[...]
        y_ref[...] = (o * g * _sigmoid(g)).astype(BF16)
        o_ref[...] = o
        m_ref[...] = _as_row(m)
        linv_ref[...] = _as_row(linv)

    tile = lambda base: pl.BlockSpec((tq, HEAD), lambda h, i, base=base: (i, base + h))
    full = lambda base: pl.BlockSpec((S, HEAD), lambda h, i, base=base: (0, base + h))
    rowv = pl.BlockSpec((None, None, 1, tq), lambda h, i: (h, i, 0, 0))
    stat = jax.ShapeDtypeStruct((N_FOX, nq, 1, tq), F32)
    return pl.pallas_call(
        body, name=name, grid=(N_FOX, nq),
        in_specs=[tile(0), pl.BlockSpec((None, tq, HEAD), lambda h, i: (h, i, 0)),
                  full(N_FOX), pl.BlockSpec((None, S, HEAD), lambda h, i: (h, 0, 0)),
                  full(2 * N_FOX), tile(0)],
        out_specs=[tile(0), tile(0), rowv, rowv],
        out_shape=[jax.ShapeDtypeStruct((S, FOX_W), BF16), jax.ShapeDtypeStruct((S, FOX_W), F32), stat, stat],
        compiler_params=_params(("parallel", "arbitrary")),
    )(qkv, qa, qkv, ka, qkv, pg)


def _fox_bwd_prep(dy, o, pg, *, tq, name):
    S = o.shape[0]
    base = CONV_W // HEAD

    def body(dy_ref, o_ref, g_ref, do_ref, dg_ref, dl_ref):
        g = g_ref[...]
        sg = _sigmoid(g)
        dyv = dy_ref[...]
        ov = o_ref[...]
        dob = (dyv * g * sg).astype(BF16)
        do_ref[...] = dob
        dg_ref[...] = (dyv * ov * _dsilu(g, sg)).astype(BF16)
        dl_ref[...] = _as_row(jnp.sum(dob.astype(F32) * ov, axis=-1, keepdims=True))

    tile = lambda b: pl.BlockSpec((tq, HEAD), lambda h, i, b=b: (i, b + h))
    return pl.pallas_call(
        body, name=name, grid=(N_FOX, S // tq),
        in_specs=[tile(base), tile(0), tile(0)],
        out_specs=[tile(0), tile(0), pl.BlockSpec((None, None, 1, tq), lambda h, i: (h, i, 0, 0))],
        out_shape=[jax.ShapeDtypeStruct((S, FOX_W), BF16), jax.ShapeDtypeStruct((S, FOX_W), BF16),
                   jax.ShapeDtypeStruct((N_FOX, S // tq, 1, tq), F32)],
        compiler_params=_params(("parallel", "parallel")),
    )(dy, o, pg)


def _fox_bwd(qkv, do, qa, ka, m_row, linv_row, delta_row, *, tq, nsub, name):
    S = qkv.shape[0]
    nq = S // tq
    cs = tq // nsub
    scale = HEAD ** -0.5

    def body(k_ref, ka_ref, v_ref, q_ref, qa_ref, do_ref, mr_ref, lir_ref, dlr_ref, dq_ref, dk_ref, dv_ref, dc_ref):
        j = pl.program_id(1)

        @pl.when(j == 0)
        def _():
            dq_ref[...] = jnp.zeros_like(dq_ref)

        kj = k_ref[...]
        kja = jnp.concatenate([kj, ka_ref[...]], axis=1)
        vj = v_ref[...]

        def blk(i, carry, masked):
            dk, dv, dc = carry
            m_i = mr_ref[i]
            linv_i = lir_ref[i]
            dl_i = dlr_ref[i]
            for c in range(nsub):
                off = pl.multiple_of(i * tq + c * cs, cs)
                cols = slice(c * cs, (c + 1) * cs)
                qi = q_ref[pl.ds(off, cs), :]
                qia = jnp.concatenate([qi, qa_ref[pl.ds(off, cs), :]], axis=1)
                doi = do_ref[pl.ds(off, cs), :]
                st = lax.dot_general(kja, qia, NT, preferred_element_type=F32) * FOX_SCALE2
                if masked:
                    st = jnp.where(_causal_part(tq, cs, 0, c * cs, q_is_row=False), st, NEG)
                pt = jnp.exp2(st - m_i[:, cols]).astype(BF16).astype(F32) * linv_i[:, cols]
                dv = dv + jnp.dot(pt.astype(BF16), doi, preferred_element_type=F32)
                dpt = lax.dot_general(vj, doi, NT, preferred_element_type=F32)
                dst = pt * (dpt - dl_i[:, cols])
                dsb = dst.astype(BF16)
                dk = dk + jnp.dot(dsb, qi, preferred_element_type=F32)
                dq_ref[pl.ds(off, cs), :] += lax.dot_general(dsb, kj, TN, preferred_element_type=F32) * scale
                dc = dc - jnp.sum(dst, axis=-1, keepdims=True)
            return dk, dv, dc

        init = (jnp.zeros((tq, HEAD), F32), jnp.zeros((tq, HEAD), F32), jnp.zeros((tq, 1), F32))
        carry = blk(j, init, True)
        dk, dv, dc = lax.fori_loop(j + 1, nq, lambda i, c: blk(i, c, False), carry)
        dk_ref[...] = (dk * scale).astype(BF16)
        dv_ref[...] = dv.astype(BF16)
        dc_ref[...] = _as_row(dc)

    tile = lambda base: pl.BlockSpec((tq, HEAD), lambda h, j, base=base: (j, base + h))
    full = lambda base: pl.BlockSpec((S, HEAD), lambda h, j, base=base: (0, base + h))
    atile = pl.BlockSpec((None, tq, HEAD), lambda h, j: (h, j, 0))
    afull = pl.BlockSpec((None, S, HEAD), lambda h, j: (h, 0, 0))
    rowt = pl.BlockSpec((None, None, 1, tq), lambda h, j: (h, j, 0, 0))
    rowv = pl.BlockSpec((None, nq, 1, tq), lambda h, j: (h, 0, 0, 0))
    return pl.pallas_call(
        body, name=name, grid=(N_FOX, nq),
        in_specs=[tile(N_FOX), atile, tile(2 * N_FOX), full(0), afull, full(0), rowv, rowv, rowv],
        out_specs=[full(0), tile(0), tile(0), rowt],
        out_shape=[jax.ShapeDtypeStruct((S, FOX_W), F32), jax.ShapeDtypeStruct((S, FOX_W), BF16),
                   jax.ShapeDtypeStruct((S, FOX_W), BF16), jax.ShapeDtypeStruct((N_FOX, nq, 1, tq), F32)],
        compiler_params=_params(("arbitrary", "arbitrary")),
    )(qkv, ka, qkv, qkv, qa, do, m_row, linv_row, delta_row)


def _mem_heads(mq, mkv, h):
    lo = h * HEAD
    qh = mq[:, lo:lo + HEAD].astype(BF16)
    kh = mkv[:, lo:lo + HEAD].astype(BF16)
    vh = mkv[:, MEM_W + lo:MEM_W + lo + HEAD].astype(BF16)
    return qh, kh, vh


def _mem_softmax(qh, kh):
    s = lax.dot_general(qh, kh, NT, preferred_element_type=F32) * (HEAD ** -0.5)
    e = jnp.exp(s - jnp.max(s, axis=-1, keepdims=True))
    return e / jnp.sum(e, axis=-1, keepdims=True)


def _mem_fwd(pg, mkv, *, tq, name):
    S = pg.shape[0]
    M = mkv.shape[0]
    qb = FOX_W // MEM_W

    def body(mq_ref, g_ref, mkv_ref, y_ref):
        mq = mq_ref[...]
        mkvv = mkv_ref[...]
        for h in range(N_MEMH):
            qh, kh, vh = _mem_heads(mq, mkvv, h)
            p = _mem_softmax(qh, kh)
            o = jnp.dot(p.astype(BF16), vh, preferred_element_type=F32)
            g = g_ref[:, h * HEAD:(h + 1) * HEAD]
            y_ref[:, h * HEAD:(h + 1) * HEAD] = (o * g * _sigmoid(g)).astype(BF16)

    return pl.pallas_call(
        body, name=name, grid=(S // tq,),
        in_specs=[pl.BlockSpec((tq, MEM_W), lambda i: (i, qb)), pl.BlockSpec((tq, MEM_W), lambda i: (i, qb + 1)),
                  pl.BlockSpec((M, 2 * MEM_W), lambda i: (0, 0))],
        out_specs=pl.BlockSpec((tq, MEM_W), lambda i: (i, 0)),
        out_shape=jax.ShapeDtypeStruct((S, MEM_W), BF16),
        compiler_params=_params(("parallel",)),
    )(pg, pg, mkv)


def _mem_bwd(pg, mkv, dy, *, tq, name):
    S = pg.shape[0]
    M = mkv.shape[0]
    qb = FOX_W // MEM_W
    yb = (CONV_W + FOX_W) // MEM_W
    scale = HEAD ** -0.5

    def body(mq_ref, g_ref, mkv_ref, dy_ref, dmq_ref, dg_ref, dmkv_ref):
        i = pl.program_id(0)

        @pl.when(i == 0)
        def _():
            dmkv_ref[...] = jnp.zeros_like(dmkv_ref)

        mq = mq_ref[...]
        mkvv = mkv_ref[...]
        for h in range(N_MEMH):
            lo = h * HEAD
            qh, kh, vh = _mem_heads(mq, mkvv, h)
            p = _mem_softmax(qh, kh)
            o = jnp.dot(p.astype(BF16), vh, preferred_element_type=F32)
            g = g_ref[:, lo:lo + HEAD]
            sg = _sigmoid(g)
            dyh = dy_ref[:, lo:lo + HEAD]
            do = dyh * g * sg
            dg_ref[:, lo:lo + HEAD] = (dyh * o * _dsilu(g, sg)).astype(BF16)
            dob = do.astype(BF16)
            dp = lax.dot_general(dob, vh, NT, preferred_element_type=F32)
            ds = p * (dp - jnp.sum(do * o, axis=-1, keepdims=True))
            dsb = ds.astype(BF16)
            dmq_ref[:, lo:lo + HEAD] = (jnp.dot(dsb, kh, preferred_element_type=F32) * scale).astype(BF16)
            dmkv_ref[:, lo:lo + HEAD] += lax.dot_general(dsb, qh, TN, preferred_element_type=F32) * scale
            dmkv_ref[:, MEM_W + lo:MEM_W + lo + HEAD] += lax.dot_general(
                p.astype(BF16), dob, TN, preferred_element_type=F32)

    return pl.pallas_call(
        body, name=name, grid=(S // tq,),
        in_specs=[pl.BlockSpec((tq, MEM_W), lambda i: (i, qb)), pl.BlockSpec((tq, MEM_W), lambda i: (i, qb + 1)),
                  pl.BlockSpec((M, 2 * MEM_W), lambda i: (0, 0)), pl.BlockSpec((tq, MEM_W), lambda i: (i, yb))],
        out_specs=[pl.BlockSpec((tq, MEM_W), lambda i: (i, 0)), pl.BlockSpec((tq, MEM_W), lambda i: (i, 0)),
                   pl.BlockSpec((M, 2 * MEM_W), lambda i: (0, 0))],
        out_shape=[jax.ShapeDtypeStruct((S, MEM_W), BF16), jax.ShapeDtypeStruct((S, MEM_W), BF16),
                   jax.ShapeDtypeStruct((M, 2 * MEM_W), F32)],
        compiler_params=_params(("arbitrary",)),
    )(pg, pg, mkv, dy)


def _final(x2, target, fg, *, tm, name):
    S, D = x2.shape

    def body(x_ref, t_ref, g_ref, dx_ref, dxb_ref, dg_ref, ls_ref):
        i = pl.program_id(0)

        @pl.when(i == 0)
        def _():
            dg_ref[...] = jnp.zeros_like(dg_ref)
            ls_ref[...] = jnp.zeros_like(ls_ref)

        xv = x_ref[...]
        r = lax.rsqrt(jnp.mean(xv * xv, axis=-1, keepdims=True) + EPS)
        n = xv * r
        g = g_ref[...]
        diff = n * g - t_ref[...]
        ls_ref[...] += jnp.sum(diff * diff)
        dout = diff * (1.0 / D)
        dg_ref[...] += jnp.sum(dout * n, axis=0, keepdims=True)
        dn = dout * g
        dx = r * (dn - n * jnp.mean(dn * n, axis=-1, keepdims=True))
        dx_ref[...] = dx
        dxb_ref[...] = dx.astype(BF16)

    row = pl.BlockSpec((tm, D), lambda i: (i, 0))
    vec = pl.BlockSpec((1, D), lambda i: (0, 0))
    return pl.pallas_call(
        body, name=name, grid=(S // tm,),
        in_specs=[row, row, vec],
        out_specs=[row, row, vec, pl.BlockSpec((8, LANE), lambda i: (0, 0))],
        out_shape=[jax.ShapeDtypeStruct((S, D), F32), jax.ShapeDtypeStruct((S, D), BF16),
                   jax.ShapeDtypeStruct((1, D), F32), jax.ShapeDtypeStruct((8, LANE), F32)],
        compiler_params=_params(("arbitrary",)),
    )(x2, target, fg)


def _adamw(w, g, m, v, *, name):
    R, C = w.shape
    tr, tc = R, C
    for cand in (256, 128, 64, 32, 16, 8):
        if R % cand == 0 and R > cand:
            tr = cand
            break
    if tr == R and R > 256 and C % 256 == 0:
        tc = 256
    c1 = 1.0 - ADAM_B1 ** ADAM_STEP
    c2 = 1.0 - ADAM_B2 ** ADAM_STEP

    def body(w_ref, g_ref, m_ref, v_ref, d_ref, nm_ref, nv_ref):
        gv = g_ref[...]
        nm = ADAM_B1 * m_ref[...] + (1.0 - ADAM_B1) * gv
        nv = ADAM_B2 * v_ref[...] + (1.0 - ADAM_B2) * (gv * gv)
        nm_ref[...] = nm
        nv_ref[...] = nv
        d_ref[...] = -ADAM_LR * ((nm / c1) / (jnp.sqrt(nv / c2) + ADAM_EPS) + ADAM_WD * w_ref[...])

    spec = pl.BlockSpec((tr, tc), lambda i, j: (i, j))
    shp = jax.ShapeDtypeStruct((R, C), F32)
    return pl.pallas_call(
        body, name=name, grid=(R // tr, C // tc),
        in_specs=[spec] * 4, out_specs=[spec] * 3, out_shape=[shp] * 3,
        compiler_params=_params(("parallel", "parallel")),
    )(w, g, m, v)


def _sum4(q, own, me, *, name):
    _, R, C = q.shape
    tr = R
    for cand in (256, 128, 64, 32, 16, 8):
        if R % cand == 0 and R > cand:
            tr = cand
            break

    def body(me_ref, own_ref, q1_ref, q2_ref, q3_ref, o_ref):
        f = lambda r: r[...].astype(F32)
        o_ref[...] = ((f(own_ref) + f(q1_ref)) + f(q2_ref)) + f(q3_ref)

    blk = lambda d: pl.BlockSpec((None, tr, C), lambda i, me_ref, d=d: (me_ref[0] ^ d, i, 0))
    return pl.pallas_call(
        body, name=name,
        grid_spec=pltpu.PrefetchScalarGridSpec(
            num_scalar_prefetch=1, grid=(R // tr,),
            in_specs=[blk(0), blk(1), blk(2), blk(3)],
            out_specs=pl.BlockSpec((tr, C), lambda i, me_ref: (i, 0))),
        out_shape=jax.ShapeDtypeStruct((R, C), F32),
        compiler_params=_params(("parallel",)),
    )(jnp.reshape(me, (1,)).astype(jnp.int32), own, q, q, q)


def _add_sibling(g, got, c, *, name):
    K, _, R, C = g.shape
    tr = R
    for cand in (256, 128, 64, 32, 16, 8):
        if R % cand == 0 and R > cand:
            tr = cand
            break

    def body(c_ref, a_ref, b_ref, o_ref):
        o_ref[...] = (a_ref[...] + b_ref[...]).astype(BF16)

    spec = pl.BlockSpec((None, tr, C), lambda k, i, c_ref: (k, i, 0))
    return pl.pallas_call(
        body, name=name,
        grid_spec=pltpu.PrefetchScalarGridSpec(
            num_scalar_prefetch=1, grid=(K, R // tr),
            in_specs=[pl.BlockSpec((None, None, tr, C), lambda k, i, c_ref: (k, c_ref[0], i, 0)), spec],
            out_specs=spec),
        out_shape=jax.ShapeDtypeStruct((K, R, C), BF16),
        compiler_params=_params(("parallel", "parallel")),
    )(jnp.reshape(c, (1,)).astype(jnp.int32), g, got)


ANY = pl.BlockSpec(memory_space=pl.ANY)


def _other_chips(x, y):
    return [(d, 1 - x if d & 2 else x, 1 - y if d & 1 else y) for d in (1, 2, 3)]


def _weight_all_gather(shards, *, name):
    n = len(shards)

    def body(*refs):
        src_refs, out_refs = refs[:n], refs[n:2 * n]
        ici_send, ici_recv, d2d_send, d2d_recv = refs[2 * n:]
        x, y, c = lax.axis_index("x"), lax.axis_index("y"), lax.axis_index("c")
        me = 2 * x + y
        sends = []
        for t in range(n):
            for d, tx, ty in _other_chips(x, y):
                cp = pltpu.make_async_remote_copy(
                    src_ref=src_refs[t].at[c], dst_ref=out_refs[t].at[me, c],
                    send_sem=ici_send.at[3 * t + d - 1], recv_sem=ici_recv.at[3 * t + d - 1],
                    device_id=(tx, ty, c), device_id_type=MESH)
                cp.start()
                sends.append(cp)
        for t in range(n):
            for d, tx, ty in _other_chips(x, y):
                k = 2 * tx + ty
                landed = out_refs[t].at[k, c]
                pltpu.make_async_remote_copy(
                    src_ref=landed, dst_ref=landed,
                    send_sem=ici_send.at[3 * t + d - 1], recv_sem=ici_recv.at[3 * t + d - 1],
                    device_id=(tx, ty, c), device_id_type=MESH).wait_recv()
                cp = pltpu.make_async_remote_copy(
                    src_ref=landed, dst_ref=landed,
                    send_sem=d2d_send.at[3 * t + d - 1], recv_sem=d2d_recv.at[3 * t + d - 1],
                    device_id=(x, y, 1 - c), device_id_type=MESH)
                cp.start()
                sends.append(cp)
        for t in range(n):
            for d, tx, ty in _other_chips(x, y):
                theirs = out_refs[t].at[2 * tx + ty, 1 - c]
                pltpu.make_async_remote_copy(
                    src_ref=theirs, dst_ref=theirs,
                    send_sem=d2d_send.at[3 * t + d - 1], recv_sem=d2d_recv.at[3 * t + d - 1],
                    device_id=(x, y, 1 - c), device_id_type=MESH).wait_recv()
        for cp in sends:
            cp.wait_send()

    sem = pltpu.SemaphoreType.DMA((3 * n,))
    return pl.pallas_call(
        body, name=name,
        in_specs=[ANY] * n, out_specs=[ANY] * n,
        out_shape=[jax.ShapeDtypeStruct((N_CHIPS,) + s.shape, s.dtype) for s in shards],
        scratch_shapes=[sem, sem, sem, sem],
    )(*shards)


def _scatter_copies(src_refs, out_refs, send_sems, recv_sems):
    n = len(src_refs)
    x, y, c = lax.axis_index("x"), lax.axis_index("y"), lax.axis_index("c")
    me = 2 * x + y
    copies = []
    for t in range(n):
        for d, tx, ty in _other_chips(x, y):
            copies.append(pltpu.make_async_remote_copy(
                src_ref=src_refs[t].at[2 * tx + ty], dst_ref=out_refs[t].at[me],
                send_sem=send_sems.at[3 * t + d - 1], recv_sem=recv_sems.at[3 * t + d - 1],
                device_id=(tx, ty, c), device_id_type=MESH))
    return copies


def _mm_scatter(a, b, add, srcs, *, tm, tn, tk, name):
    M, K = a.shape
    N = b.shape[1]
    tm, tn, tk = min(tm, M), min(tn, N), min(tk, K)
    assert M % tm == 0 and N % tn == 0 and K % tk == 0, (name, M, N, K, tm, tn, tk)
    gi, gj, gk = M // tm, N // tn, K // tk
    n = len(srcs)

    def body(*refs):
        a_ref, b_ref, add_ref = refs[:3]
        src_refs = refs[3:3 + n]
        o_ref = refs[3 + n]
        out_refs = refs[4 + n:4 + 2 * n]
        acc_ref, send_sems, recv_sems = refs[4 + 2 * n:]
        i, j, k = pl.program_id(0), pl.program_id(1), pl.program_id(2)

        @pl.when((i == 0) & (j == 0) & (k == 0))
        def _():
            for cp in _scatter_copies(src_refs, out_refs, send_sems, recv_sems):
                cp.start()

        @pl.when(k == 0)
        def _():
            acc_ref[...] = jnp.zeros_like(acc_ref)

        acc_ref[...] += jnp.dot(a_ref[...].astype(BF16), b_ref[...].astype(BF16), preferred_element_type=F32)

        @pl.when(k == gk - 1)
        def _():
            o_ref[...] = acc_ref[...] + add_ref[...]

        @pl.when((i == gi - 1) & (j == gj - 1) & (k == gk - 1))
        def _():
            for cp in _scatter_copies(src_refs, out_refs, send_sems, recv_sems):
                cp.wait()

    o_spec = pl.BlockSpec((tm, tn), lambda i, j, k: (i, j))
    sem = pltpu.SemaphoreType.DMA((3 * n,))
    res = pl.pallas_call(
        body, name=name, grid=(gi, gj, gk),
        in_specs=[pl.BlockSpec((tm, tk), lambda i, j, k: (i, k)), pl.BlockSpec((tk, tn), lambda i, j, k: (k, j)),
                  o_spec] + [ANY] * n,
        out_specs=[o_spec] + [ANY] * n,
        out_shape=[jax.ShapeDtypeStruct((M, N), F32)] + [jax.ShapeDtypeStruct(s.shape, s.dtype) for s in srcs],
        scratch_shapes=[pltpu.VMEM((tm, tn), F32), sem, sem],
        compiler_params=_params(("arbitrary", "arbitrary", "arbitrary")),
    )(a, b, add, *srcs)
    return res[0], res[1:]


def _swap_halves(grads, *, name):
    n = len(grads)

    def body(*refs):
        src_refs, out_refs = refs[:n], refs[n:2 * n]
        send_sems, recv_sems = refs[2 * n:]
        x, y, c = lax.axis_index("x"), lax.axis_index("y"), lax.axis_index("c")
        copies = []
        for t in range(n):
            for k in range(N_CHIPS):
                cp = pltpu.make_async_remote_copy(
                    src_ref=src_refs[t].at[k, 1 - c], dst_ref=out_refs[t].at[k],
                    send_sem=send_sems.at[N_CHIPS * t + k], recv_sem=recv_sems.at[N_CHIPS * t + k],
                    device_id=(x, y, 1 - c), device_id_type=MESH)
                cp.start()
                copies.append(cp)
        for cp in copies:
            cp.wait()

    sem = pltpu.SemaphoreType.DMA((N_CHIPS * n,))
    return pl.pallas_call(
        body, name=name,
        in_specs=[ANY] * n, out_specs=[ANY] * n,
        out_shape=[jax.ShapeDtypeStruct((N_CHIPS,) + g.shape[2:], g.dtype) for g in grads],
        scratch_shapes=[sem, sem],
    )(*grads)


def _sibling_swap(srcs, *, name):
    n = len(srcs)

    def body(*refs):
        src_refs, out_refs = refs[:n], refs[n:2 * n]
        send_sems, recv_sems = refs[2 * n:]
        x, y, c = lax.axis_index("x"), lax.axis_index("y"), lax.axis_index("c")
        copies = []
        for t in range(n):
            cp = pltpu.make_async_remote_copy(
                src_ref=src_refs[t], dst_ref=out_refs[t],
                send_sem=send_sems.at[t], recv_sem=recv_sems.at[t],
                device_id=(x, y, 1 - c), device_id_type=MESH)
            cp.start()
            copies.append(cp)
        for cp in copies:
            cp.wait()

    sem = pltpu.SemaphoreType.DMA((n,))
    return pl.pallas_call(
        body, name=name,
        in_specs=[ANY] * n, out_specs=[ANY] * n,
        out_shape=[jax.ShapeDtypeStruct(s.shape, s.dtype) for s in srcs],
        scratch_shapes=[sem, sem],
    )(*srcs)


def _allreduce_small(v, *, name):
    R, C = v.shape

    def body(v_ref, o_ref, buf_ref, send_sems, recv_sems):
        x, y, c = lax.axis_index("x"), lax.axis_index("y"), lax.axis_index("c")
        me = 4 * x + 2 * y + c
        buf_ref[me] = v_ref[...]
        copies = []
        for d in range(1, N_DEV):
            tx = 1 - x if d & 4 else x
            ty = 1 - y if d & 2 else y
            tc = 1 - c if d & 1 else c
            cp = pltpu.make_async_remote_copy(
                src_ref=v_ref, dst_ref=buf_ref.at[me],
                send_sem=send_sems.at[d - 1], recv_sem=recv_sems.at[d - 1],
                device_id=(tx, ty, tc), device_id_type=MESH)
            cp.start()
            copies.append(cp)
        for cp in copies:
            cp.wait()
        acc = buf_ref[0]
        for k in range(1, N_DEV):
            acc = acc + buf_ref[k]
        o_ref[...] = acc

    return pl.pallas_call(
        body, name=name,
        in_specs=[pl.BlockSpec(memory_space=pltpu.VMEM)],
        out_specs=pl.BlockSpec(memory_space=pltpu.VMEM),
        out_shape=jax.ShapeDtypeStruct((R, C), F32),
        scratch_shapes=[pltpu.VMEM((N_DEV, R, C), F32), pltpu.SemaphoreType.DMA((N_DEV - 1,)),
                        pltpu.SemaphoreType.DMA((N_DEV - 1,))],
    )(v)


_A0, _B0, _GC0 = 0, CONV_W, 2 * CONV_W
_Q0 = 3 * CONV_W
_FL0 = _Q0 + 3 * FOX_W
_FG0 = _FL0 + N_FOX
_MQ0 = _FG0 + FOX_W
_MG0 = _MQ0 + MEM_W
_DIN = _MG0 + MEM_W
_WMAIN = _DIN - N_FOX
_PC_N = 3 * CONV_W
_QKV_N = 3 * FOX_W
_PG_N = FOX_W + 2 * MEM_W


def _rows_pad8(a):
    r = a.shape[0]
    p = (-r) % 8
    return jnp.pad(a, ((0, p), (0, 0))) if p else a


def _step(x, mem, target, norm_g, mem_norm_g, final_g, b_f, cw_pad, conv_b, ln_g, ln_b,
          w_in, w_pw, w_mkv, w_out, ci, me):
    S, D = x.shape
    M = mem.shape[0]
    tq = min(512, S)
    tf = min(1024, S)
    tt = min(256, S)
    tm = min(256, S)

    w_main = jnp.concatenate([w_in[:, :_FL0], w_in[:, _FG0:]], axis=1)
    w_fl = jnp.pad(w_in[:, _FL0:_FG0], ((0, 0), (0, LANE - N_FOX)))
    w_main_t, w_fl_t, w_out_t = w_main.T, w_fl.T, w_out.T
    bf_pad = jnp.pad(b_f, ((0, 0), (0, LANE - N_FOX)))

    h = _rms_fwd(x, norm_g, tm=tm, name="rms_fwd")
    h_t = h.T
    pc = _mm(h, w_main, tm=1024, tn=512, tk=2048, n=_PC_N, b_off=0, name="proj_conv")
    qkv = _mm(h, w_main, out_dtype=BF16, tm=1024, tn=512, tk=2048, n=_QKV_N, b_off=_PC_N // 512, name="proj_qkv")
    pg = _mm(h, w_main, tm=1024, tn=512, tk=2048, n=_PG_N, b_off=(_PC_N + _QKV_N) // 512, name="proj_gate")
    pfl = _mm(h, w_fl, tm=1024, tn=LANE, tk=2048, name="proj_logit")

    y_conv, u1s = _conv_fwd(pc, cw_pad, conv_b, ln_g, ln_b, w_pw, tt=tt, name="conv_fwd")

    qa, ka = _fox_cumsum(pfl, bf_pad, tc=tt, name="fox_cumsum")
    y_fox, o_fox, fox_m, fox_linv = _fox_fwd(qkv, pg, qa, ka, tq=tf, nsub=2, name="fox_fwd")

    hm = _rms_fwd(mem, mem_norm_g, tm=min(256, M), name="rms_mem")
    mkv = _mm(hm, w_mkv, tm=256, tn=512, tk=2048, name="mem_kv")
    y_mem = _mem_fwd(pg, mkv, tq=tq, name="mem_fwd")

    y = jnp.concatenate([y_conv, y_fox, y_mem], axis=1)
    x2 = _mm(y, w_out, add=x, tm=1024, tn=512, tk=2048, name="out_proj")
    dx2, dx2_b, dfg, sq = _final(x2, target, final_g.reshape(1, D), tm=tm, name="final")

    dy = _mm(dx2_b, w_out_t, tm=1024, tn=512, tk=2048, name="d_y")
    dw_out = _mm(y.T, dx2_b, tm=1024, tn=512, tk=2048, name="d_w_out")

    dpc, dw_pw, dcw, dsm = _conv_bwd(pc, u1s, dy, cw_pad, ln_g, ln_b, w_pw, tt=tt, name="conv_bwd")

    do, dfgate, delta = _fox_bwd_prep(dy, o_fox, pg, tq=tf, name="fox_bwd_prep")
    dq, dk, dv, dc = _fox_bwd(qkv, do, qa, ka, fox_m, fox_linv, delta, tq=tf, nsub=2, name="fox_bwd")
    dc_pad = jnp.pad(dc.reshape(N_FOX, S).T, ((0, 0), (0, LANE - N_FOX)))
    dfl, dbf = _fox_dlogit(dc_pad, pfl, bf_pad, tc=tt, name="fox_dlogit")

    dmq, dmgate, dmkv = _mem_bwd(pg, mkv, dy, tq=tq, name="mem_bwd")
    dw_mkv = _mm(hm, dmkv, ta=True, tm=512, tn=512, tk=256, name="d_w_mkv")
    dhm = _mm(dmkv, w_mkv, tb=True, tm=256, tn=512, tk=1024, name="d_hm")
    _, dmg = _rms_bwd(mem, mem_norm_g, dhm, None, tm=min(256, M), name="rms_mem_bwd")

    dproj = jnp.concatenate([dpc, dq.astype(BF16), dk, dv, dfgate, dmq, dmgate], axis=1)
    dw_main = _mm(h_t, dproj, tm=1024, tn=_WMAIN // 4, tk=1024, name="d_w_main")
    dw_fl = _mm(h_t, dfl, tm=1024, tn=LANE, tk=1024, name="d_w_logit")
    dw_in = jnp.concatenate([dw_main[:, :_FL0], dw_fl[:, :N_FOX], dw_main[:, _FL0:]], axis=1)

    nin = _DIN // N_CHIPS
    big = [jnp.transpose(dw_in.reshape(D, N_CHIPS, nin), (1, 0, 2)),
           dw_pw.reshape(N_CHIPS, CONV_W // N_CHIPS, CONV_W),
           dw_mkv.reshape(N_CHIPS, D // N_CHIPS, 2 * MEM_W),
           dw_out.reshape(N_CHIPS, MIX_W // N_CHIPS, D)]
    big = [g.reshape(N_CHIPS, 2, g.shape[1] // 2, g.shape[2]) for g in big]
    got = _swap_halves(big, name="grad_swap_halves")
    chip = [_add_sibling(g, o, ci, name=f"grad_add_sibling_{t}") for t, (g, o) in enumerate(zip(big, got))]

    dh0 = _mm(dfl, w_fl_t, tm=512, tn=512, tk=LANE, name="d_h_logit")
    dh, parts = _mm_scatter(dproj, w_main_t, dh0, chip, tm=512, tn=512, tk=_WMAIN // 2, name="d_h_grad_scatter")
    grad_x, dng = _rms_bwd(x, norm_g, dh, dx2, tm=tm, name="rms_bwd")

    red = [_sum4(p, own, me, name=f"grad_sum_chips_{t}") for t, (p, own) in enumerate(zip(parts, chip))]
    other = _sibling_swap(red, name="grad_swap_result")
    full = [jnp.where(ci == 0, jnp.concatenate([r, o], axis=0), jnp.concatenate([o, r], axis=0))
            for r, o in zip(red, other)]

    small = dict(norm_g=dng, mem_norm_g=dmg, final_g=dfg, b_f=dbf[0:1, :], conv_w=dcw,
                 conv_b=dsm[0:1], conv_ln_g=dsm[1:2], conv_ln_b=dsm[2:3])
    return sq[0, 0], grad_x, full, small


_SMALL_ORDER = ("norm_g", "mem_norm_g", "final_g", "b_f", "conv_w", "conv_b", "conv_ln_g", "conv_ln_b")


def _pack_small(small):
    parts, layout = [], []
    row = 0
    for k in _SMALL_ORDER:
        p = _rows_pad8(small[k].reshape(-1, LANE))
        layout.append((k, row, small[k].shape))
        parts.append(p)
        row += p.shape[0]
    return jnp.concatenate(parts, axis=0), layout


def _unpack_small(packed, layout):
    out = {}
    for k, row, shape in layout:
        nrow = (shape[0] * shape[1]) // LANE
        out[k] = packed[row:row + nrow].reshape(shape)
    return out


def kernel(x, mem, norm_g, mem_norm_g, w_in, b_f, conv_w, conv_b, conv_ln_g, conv_ln_b, w_conv_pw, w_mem_kv, w_out, final_g, loss_target, m_norm_g, m_mem_norm_g, m_w_in, m_b_f, m_conv_w, m_conv_b, m_conv_ln_g, m_conv_ln_b, m_w_conv_pw, m_w_mem_kv, m_w_out, m_final_g, v_norm_g, v_mem_norm_g, v_w_in, v_b_f, v_conv_w, v_conv_b, v_conv_ln_g, v_conv_ln_b, v_w_conv_pw, v_w_mem_kv, v_w_out, v_final_g):
    S, D = x.shape[1], x.shape[2]
    xi, yi, ci = lax.axis_index("x"), lax.axis_index("y"), lax.axis_index("c")
    chip = 2 * xi + yi

    halves = lambda a: a.reshape(2, a.shape[0] // 2, a.shape[1])
    cw_shard = jnp.pad(conv_w[0], ((0, CONV_HALO - CONV_K), (0, 0)))
    shards = [halves(w_in[0].astype(BF16)), halves(w_conv_pw[0].astype(BF16)), halves(w_mem_kv[0].astype(BF16)),
              halves(w_out[0].astype(BF16)), halves(cw_shard)]
    gathered = _weight_all_gather(shards, name="weight_all_gather")
    g_in, g_pw, g_mkv, g_out, g_cw = [lax.dynamic_update_slice(g, s[None], (chip, 0, 0, 0))
                                      for g, s in zip(gathered, shards)]
    nin = w_in.shape[2]
    w_in_full = jnp.transpose(g_in.reshape(N_CHIPS, D, nin), (1, 0, 2)).reshape(D, N_CHIPS * nin)
    w_pw_full = g_pw.reshape(CONV_W, CONV_W)
    w_mkv_full = g_mkv.reshape(D, 2 * MEM_W)
    w_out_full = g_out.reshape(MIX_W, D)
    cw_full = jnp.transpose(g_cw.reshape(N_CHIPS, CONV_HALO, CONV_W // N_CHIPS), (1, 0, 2)).reshape(CONV_HALO, CONV_W)

    sq, grad_x, (g_w_in, g_w_pw, g_w_mkv, g_w_out), small = _step(
        x[0], mem[0], loss_target[0], norm_g, mem_norm_g, final_g, b_f, cw_full, conv_b, conv_ln_g,
        conv_ln_b, w_in_full, w_pw_full, w_mkv_full, w_out_full, ci, chip)

    loss = lax.psum(sq, ("x", "y", "c")) * (0.5 / D)

    packed, layout = _pack_small(small)
    sm = _unpack_small(_allreduce_small(packed, name="small_all_reduce"), layout)
    cshard = CONV_W // N_CHIPS
    g_conv_w = lax.dynamic_slice_in_dim(sm["conv_w"][:CONV_K], chip * cshard, cshard, axis=1)

    grads = dict(
        norm_g=sm["norm_g"], mem_norm_g=sm["mem_norm_g"], w_in=g_w_in[None], b_f=sm["b_f"][:, :N_FOX],
        conv_w=g_conv_w[None], conv_b=sm["conv_b"], conv_ln_g=sm["conv_ln_g"], conv_ln_b=sm["conv_ln_b"],
        w_conv_pw=g_w_pw[None], w_mem_kv=g_w_mkv[None], w_out=g_w_out[None], final_g=sm["final_g"].reshape(D))
    weights = dict(norm_g=norm_g, mem_norm_g=mem_norm_g, w_in=w_in, b_f=b_f, conv_w=conv_w, conv_b=conv_b,
                   conv_ln_g=conv_ln_g, conv_ln_b=conv_ln_b, w_conv_pw=w_conv_pw, w_mem_kv=w_mem_kv, w_out=w_out,
                   final_g=final_g)
    ms = dict(norm_g=m_norm_g, mem_norm_g=m_mem_norm_g, w_in=m_w_in, b_f=m_b_f, conv_w=m_conv_w, conv_b=m_conv_b,
              conv_ln_g=m_conv_ln_g, conv_ln_b=m_conv_ln_b, w_conv_pw=m_w_conv_pw, w_mem_kv=m_w_mem_kv,
              w_out=m_w_out, final_g=m_final_g)
    vs = dict(norm_g=v_norm_g, mem_norm_g=v_mem_norm_g, w_in=v_w_in, b_f=v_b_f, conv_w=v_conv_w, conv_b=v_conv_b,
              conv_ln_g=v_conv_ln_g, conv_ln_b=v_conv_ln_b, w_conv_pw=v_w_conv_pw, w_mem_kv=v_w_mem_kv,
              w_out=v_w_out, final_g=v_final_g)

    names = ("norm_g", "mem_norm_g", "w_in", "b_f", "conv_w", "conv_b", "conv_ln_g", "conv_ln_b", "w_conv_pw",
             "w_mem_kv", "w_out", "final_g")
    deltas, new_m, new_v = {}, {}, {}
    for k in names:
        shape = weights[k].shape
        if k == "w_in":
            two_d = lambda a: a.reshape(shape[-2], shape[-1]).T
            back = lambda a: a.T.reshape(shape)
        else:
            two_d = lambda a: a.reshape(-1, shape[-1])
            back = lambda a: a.reshape(shape)
        d, nm, nv = _adamw(two_d(weights[k]), two_d(grads[k]), two_d(ms[k]), two_d(vs[k]), name=f"adamw_{k}")
        deltas[k], new_m[k], new_v[k] = back(d), back(nm), back(nv)

    return (loss, grad_x[None], *[grads[k] for k in names], *[deltas[k] for k in names],
            *[new_m[k] for k in names], *[new_v[k] for k in names])
```

```python
import functools

import jax
import jax.numpy as jnp
from jax import lax
from jax.experimental import pallas as pl
from jax.experimental.pallas import tpu as pltpu

F32 = jnp.float32
BF16 = jnp.bfloat16
MESH = pl.DeviceIdType.MESH

HEAD = 128
N_FOX = 8
N_MEMH = 4
CONV_W = 512
FOX_W = N_FOX * HEAD
MEM_W = N_MEMH * HEAD
MIX_W = CONV_W + FOX_W + MEM_W
CONV_K = 31
CONV_HALO = 32
EPS = 1e-6
NEG = -1e30
LANE = 128
N_CHIPS = 4
N_DEV = 8
VMEM_LIMIT = 56 * 1024 * 1024

ADAM_LR = 0.001
ADAM_B1 = 0.9
ADAM_B2 = 0.999
ADAM_EPS = 1e-08
ADAM_WD = 0.01
ADAM_STEP = 10

NT = (((1,), (1,)), ((), ()))
TN = (((0,), (0,)), ((), ()))
NN = (((1,), (0,)), ((), ()))


def _params(sem=None):
    kw = dict(vmem_limit_bytes=VMEM_LIMIT)
    if sem is not None:
        kw["dimension_semantics"] = sem
    return pltpu.CompilerParams(**kw)


def _sigmoid(v):
    return jax.nn.sigmoid(v)


def _dsilu(v, s):
    return s * (1.0 + v * (1.0 - s))


def _rms_fwd(x, g, *, tm, name, gather=None):
    R, D = x.shape
    steps = R // tm

    def body(x_ref, g_ref, h_ref):
        xv = x_ref[...]
        r = lax.rsqrt(jnp.mean(xv * xv, axis=-1, keepdims=True) + EPS)
        h_ref[...] = (xv * r * g_ref[...]).astype(BF16)

    in_specs = [pl.BlockSpec((tm, D), lambda i: (i, 0)), pl.BlockSpec((1, D), lambda i: (0, 0))]
    out_spec = pl.BlockSpec((tm, D), lambda i: (i, 0))
    out_shape = jax.ShapeDtypeStruct((R, D), BF16)
    if gather is None:
        return pl.pallas_call(
            body, name=name, grid=(steps,), in_specs=in_specs, out_specs=out_spec, out_shape=out_shape,
            compiler_params=_params(("parallel",)),
        )(x, g)
    res = _call_with_gather(
        body, gather, lambda: pl.program_id(0) == 0, lambda: pl.program_id(0) == steps - 1,
        n_in=2, n_out=1, in_specs=in_specs, out_specs=[out_spec], out_shape=[out_shape], scratch_shapes=[],
        name=name, grid=(steps,), compiler_params=_params(("arbitrary",)),
    )(x, g, *gather)
    return res[0], res[1:]


def _rms_bwd(x, g, dh, dres, *, tm, name):
    R, D = x.shape
    has_res = dres is not None

    def body(*refs):
        if has_res:
            x_ref, g_ref, dh_ref, dres_ref, dx_ref, dg_ref = refs
        else:
            x_ref, g_ref, dh_ref, dx_ref, dg_ref = refs
        i = pl.program_id(0)

        @pl.when(i == 0)
        def _():
            dg_ref[...] = jnp.zeros_like(dg_ref)

        xv = x_ref[...]
        r = lax.rsqrt(jnp.mean(xv * xv, axis=-1, keepdims=True) + EPS)
        n = xv * r
        dh = dh_ref[...]
        dg_ref[...] += jnp.sum(dh * n, axis=0, keepdims=True)
        dn = dh * g_ref[...]
        dx = r * (dn - n * jnp.mean(dn * n, axis=-1, keepdims=True))
        if has_res:
            dx = dx + dres_ref[...]
        dx_ref[...] = dx

    row = pl.BlockSpec((tm, D), lambda i: (i, 0))
    vec = pl.BlockSpec((1, D), lambda i: (0, 0))
    ins = [row, vec, row] + ([row] if has_res else [])
    args = (x, g, dh) + ((dres,) if has_res else ())
    return pl.pallas_call(
        body, name=name, grid=(R // tm,),
        in_specs=ins, out_specs=[row, vec],
        out_shape=[jax.ShapeDtypeStruct((R, D), F32), jax.ShapeDtypeStruct((1, D), F32)],
        compiler_params=_params(("arbitrary",)),
    )(*args)


def _mm(a, b, *, ta=False, tb=False, out_dtype=F32, add=None, tm, tn, tk, n=None, b_off=0, name, gather=None):
    M, K = (a.shape[1], a.shape[0]) if ta else a.shape
    nb = b.shape[0] if tb else b.shape[1]
    n = nb if n is None else n
    tm, tn, tk = min(tm, M), min(tn, n), min(tk, K)
    assert M % tm == 0 and n % tn == 0 and K % tk == 0, (name, M, n, K, tm, tn, tk)
    nk = K // tk
    has_add = add is not None

    def body(*refs):
        if has_add:
            a_ref, b_ref, add_ref, o_ref, acc_ref = refs
        else:
            a_ref, b_ref, o_ref, acc_ref = refs
        k = pl.program_id(2)

        @pl.when(k == 0)
        def _():
            acc_ref[...] = jnp.zeros_like(acc_ref)

        av = a_ref[...].astype(BF16)
        bv = b_ref[...].astype(BF16)
        dims = (((0 if ta else 1,), (1 if tb else 0,)), ((), ()))
        acc_ref[...] += lax.dot_general(av, bv, dims, preferred_element_type=F32)

        @pl.when(k == nk - 1)
        def _():
            r = acc_ref[...]
            if has_add:
                r = r + add_ref[...]
            o_ref[...] = r.astype(out_dtype)

    a_spec = pl.BlockSpec((tk, tm), lambda i, j, k: (k, i)) if ta else pl.BlockSpec((tm, tk), lambda i, j, k: (i, k))
    b_spec = (pl.BlockSpec((tn, tk), lambda i, j, k: (j + b_off, k)) if tb
              else pl.BlockSpec((tk, tn), lambda i, j, k: (k, j + b_off)))
    o_spec = pl.BlockSpec((tm, tn), lambda i, j, k: (i, j))
    ins = [a_spec, b_spec] + ([o_spec] if has_add else [])
    args = (a, b) + ((add,) if has_add else ())
    gi, gj = M // tm, n // tn
    out_shape = jax.ShapeDtypeStruct((M, n), out_dtype)
    if gather is None:
        return pl.pallas_call(
            body, name=name, grid=(gi, gj, nk),
            in_specs=ins, out_specs=o_spec, out_shape=out_shape,
            scratch_shapes=[pltpu.VMEM((tm, tn), F32)],
            compiler_params=_params(("parallel", "parallel", "arbitrary")),
        )(*args)
    at = lambda i, j, k: (pl.program_id(0) == i) & (pl.program_id(1) == j) & (pl.program_id(2) == k)
    res = _call_with_gather(
        body, gather, lambda: at(0, 0, 0), lambda: at(gi - 1, gj - 1, nk - 1),
        n_in=len(ins), n_out=1, in_specs=ins, out_specs=[o_spec], out_shape=[out_shape],
        scratch_shapes=[pltpu.VMEM((tm, tn), F32)],
        name=name, grid=(gi, gj, nk), compiler_params=_params(("arbitrary", "arbitrary", "arbitrary")),
    )(*args, *gather)
    return res[0], res[1:]


SUB = 8


def _shifted_copies(es_ref, rows):
    e = es_ref[0]
    for b in range(1, SUB):
        es_ref[b] = pltpu.roll(e, shift=rows - b, axis=0)


def _window(es_ref, offset, tt):
    b = offset % SUB
    return es_ref[b, offset - b:offset - b + tt, :]


def _conv_taps(w_ref, es_ref, offsets, tt):
    acc = w_ref[0:1, :] * _window(es_ref, offsets[0], tt)
    for k in range(1, CONV_K):
        acc = acc + w_ref[k:k + 1, :] * _window(es_ref, offsets[k], tt)
    return acc


def _conv_fwd(pc, cw, cb, lng, lnb, wpw, *, tt, name):
    S = pc.shape[0]
    C = CONV_W
    lead = CONV_HALO - (CONV_K - 1)
    rows = tt + CONV_HALO

    def body(a_ref, b_ref, gc_ref, cw_ref, cb_ref, lng_ref, lnb_ref, wpw_ref, y_ref, u1_ref, es_ref):
        i = pl.program_id(0)

        @pl.when(i == 0)
        def _():
            es_ref[0, 0:CONV_HALO, :] = jnp.zeros((CONV_HALO, C), F32)

        @pl.when(i > 0)
        def _():
            es_ref[0, 0:CONV_HALO, :] = es_ref[0, tt:tt + CONV_HALO, :]

        es_ref[0, CONV_HALO:CONV_HALO + tt, :] = a_ref[...] * _sigmoid(b_ref[...])
        _shifted_copies(es_ref, rows)
        u1 = _conv_taps(cw_ref, es_ref, [lead + k for k in range(CONV_K)], tt) + cb_ref[...]
        u1_ref[...] = u1
        mu = jnp.mean(u1, axis=-1, keepdims=True)
        xc = u1 - mu
        rstd = lax.rsqrt(jnp.mean(xc * xc, axis=-1, keepdims=True) + EPS)
        u2 = xc * rstd * lng_ref[...] + lnb_ref[...]
        u3 = u2 * _sigmoid(u2)
        z = jnp.dot(u3.astype(BF16), wpw_ref[...], preferred_element_type=F32)
        gc = gc_ref[...]
        y_ref[...] = (z * gc * _sigmoid(gc)).astype(BF16)

    col = lambda c: pl.BlockSpec((tt, C), lambda i, c=c: (i, c))
    vec = pl.BlockSpec((1, C), lambda i: (0, 0))
    return pl.pallas_call(
        body, name=name, grid=(S // tt,),
        in_specs=[col(0), col(1), col(2), pl.BlockSpec((CONV_HALO, C), lambda i: (0, 0)), vec, vec, vec,
                  pl.BlockSpec((C, C), lambda i: (0, 0))],
        out_specs=[pl.BlockSpec((tt, C), lambda i: (i, 0))] * 2,
        out_shape=[jax.ShapeDtypeStruct((S, MIX_W), BF16), jax.ShapeDtypeStruct((S, C), F32)],
        scratch_shapes=[pltpu.VMEM((SUB, rows, C), F32)],
        compiler_params=_params(("arbitrary",)),
    )(pc, pc, pc, cw, cb, lng, lnb, wpw)


def _conv_bwd(pc, u1s, dy, cw, lng, lnb, wpw, *, tt, name):
    S = pc.shape[0]
    C = CONV_W
    nt = S // tt
    hb = tt // CONV_HALO
    lead = CONV_HALO - (CONV_K - 1)
    rows = tt + CONV_HALO

    def body(a_ref, b_ref, gc_ref, ah_ref, bh_ref, u1_ref, dy_ref, cw_ref, lng_ref, lnb_ref, wpw_ref,
             dpc_ref, dwpw_ref, dcw_ref, dsm_ref, eu_ref, ed_ref, dcw8_ref):
        i = pl.program_id(0)
        ti = nt - 1 - i

        @pl.when(i == 0)
        def _():
            ed_ref[0, tt:tt + CONV_HALO, :] = jnp.zeros((CONV_HALO, C), F32)
            dwpw_ref[...] = jnp.zeros_like(dwpw_ref)
            dcw8_ref[...] = jnp.zeros_like(dcw8_ref)
            dsm_ref[...] = jnp.zeros_like(dsm_ref)

        @pl.when(i > 0)
        def _():
            ed_ref[0, tt:tt + CONV_HALO, :] = ed_ref[0, 0:CONV_HALO, :]

        keep = jnp.where(ti > 0, 1.0, 0.0).astype(F32)
        eu_ref[0, 0:CONV_HALO, :] = ah_ref[...] * _sigmoid(bh_ref[...]) * keep
        a = a_ref[...]
        sb = _sigmoid(b_ref[...])
        eu_ref[0, CONV_HALO:CONV_HALO + tt, :] = a * sb
        _shifted_copies(eu_ref, rows)

        u1 = u1_ref[...]
        mu = jnp.mean(u1, axis=-1, keepdims=True)
        xc = u1 - mu
        rstd = lax.rsqrt(jnp.mean(xc * xc, axis=-1, keepdims=True) + EPS)
        nhat = xc * rstd
        g = lng_ref[...]
        u2 = nhat * g + lnb_ref[...]
        s2 = _sigmoid(u2)
        u3 = (u2 * s2).astype(BF16)
        z = jnp.dot(u3, wpw_ref[...], preferred_element_type=F32)

        gc = gc_ref[...]
        sg = _sigmoid(gc)
        dyv = dy_ref[...]
        dz = (dyv * gc * sg).astype(BF16)
        dpc_ref[:, 2 * C:3 * C] = (dyv * z * _dsilu(gc, sg)).astype(BF16)

        du3 = lax.dot_general(dz, wpw_ref[...], NT, preferred_element_type=F32)
        dwpw_ref[...] += lax.dot_general(u3, dz, TN, preferred_element_type=F32)
        du2 = du3 * _dsilu(u2, s2)
        dsm_ref[1:2, :] += jnp.sum(du2 * nhat, axis=0, keepdims=True)
        dsm_ref[2:3, :] += jnp.sum(du2, axis=0, keepdims=True)
        dn = du2 * g
        du1 = rstd * (dn - jnp.mean(dn, axis=-1, keepdims=True)
                      - nhat * jnp.mean(dn * nhat, axis=-1, keepdims=True))
        dsm_ref[0:1, :] += jnp.sum(du1, axis=0, keepdims=True)
        ed_ref[0, 0:tt, :] = du1
        _shifted_copies(ed_ref, rows)

        du0 = _conv_taps(cw_ref, ed_ref, [CONV_K - 1 - k for k in range(CONV_K)], tt)
        for k in range(CONV_K):
            prod = du1 * _window(eu_ref, lead + k, tt)
            part = prod[0:SUB]
            for r in range(1, tt // SUB):
                part = part + prod[r * SUB:(r + 1) * SUB]
            dcw8_ref[k * SUB:(k + 1) * SUB, :] += part

        dpc_ref[:, 0:C] = (du0 * sb).astype(BF16)
        dpc_ref[:, C:2 * C] = (du0 * a * sb * (1.0 - sb)).astype(BF16)

        @pl.when(i == nt - 1)
        def _():
            dcw_ref[...] = jnp.zeros_like(dcw_ref)
            for k in range(CONV_K):
                dcw_ref[k:k + 1, :] = jnp.sum(dcw8_ref[k * SUB:(k + 1) * SUB, :], axis=0, keepdims=True)

    col = lambda c: pl.BlockSpec((tt, C), lambda i, c=c: (nt - 1 - i, c))
    halo = lambda c: pl.BlockSpec((CONV_HALO, C), lambda i, c=c: (jnp.maximum((nt - 1 - i) * hb - 1, 0), c))
    vec = pl.BlockSpec((1, C), lambda i: (0, 0))
    fixed = lambda r: pl.BlockSpec((r, C), lambda i: (0, 0))
    return pl.pallas_call(
        body, name=name, grid=(nt,),
        in_specs=[col(0), col(1), col(2), halo(0), halo(1),
                  pl.BlockSpec((tt, C), lambda i: (nt - 1 - i, 0)),
                  pl.BlockSpec((tt, C), lambda i: (nt - 1 - i, 0)),
                  fixed(CONV_HALO), vec, vec, fixed(C)],
        out_specs=[pl.BlockSpec((tt, 3 * C), lambda i: (nt - 1 - i, 0)), fixed(C), fixed(CONV_HALO), fixed(8)],
        out_shape=[jax.ShapeDtypeStruct((S, 3 * C), BF16), jax.ShapeDtypeStruct((C, C), F32),
                   jax.ShapeDtypeStruct((CONV_HALO, C), F32), jax.ShapeDtypeStruct((8, C), F32)],
        scratch_shapes=[pltpu.VMEM((SUB, rows, C), F32), pltpu.VMEM((SUB, rows, C), F32),
                        pltpu.VMEM((CONV_HALO * SUB, C), F32)],
        compiler_params=_params(("arbitrary",)),
    )(pc, pc, pc, pc, pc, u1s, dy, cw, lng, lnb, wpw)


def _tri(n, lower):
    r = lax.broadcasted_iota(jnp.int32, (n, n), 0)
    c = lax.broadcasted_iota(jnp.int32, (n, n), 1)
    return jnp.where((r >= c) if lower else (r <= c), 1.0, 0.0).astype(F32)


def _fox_cumsum(pfl, bf, *, tc, name):
    S = pfl.shape[0]

    def body(fl_ref, bf_ref, qa_ref, ka_ref, carry_ref):
        i = pl.program_id(0)

        @pl.when(i == 0)
        def _():
            carry_ref[...] = jnp.zeros_like(carry_ref)

        z = fl_ref[...] + bf_ref[...]
        logf = jnp.minimum(z, 0.0) - jnp.log1p(jnp.exp(-jnp.abs(z)))
        c = jnp.dot(_tri(tc, True), logf, precision=lax.Precision.HIGHEST,
                    preferred_element_type=F32) + carry_ref[0:1, :]
        carry_ref[0:1, :] = c[tc - 1:tc, :]

        cs = c * (HEAD ** 0.5)
        hi = cs.astype(BF16).astype(F32)
        r1 = cs - hi
        mid = r1.astype(BF16).astype(F32)
        lo = r1 - mid
        lane = lax.broadcasted_iota(jnp.int32, (tc, LANE), 1)
        is_hi = (lane == 0) | (lane == 3)
        is_mid = (lane == 1) | (lane == 4)
        for h in range(N_FOX):
            col = lambda v: jnp.sum(jnp.where(lane == h, v, 0.0), axis=1, keepdims=True)
            pieces = jnp.where(is_hi, col(hi), jnp.where(is_mid, col(mid), col(lo)))
            qa_ref[h] = jnp.where(lane < 3, pieces, jnp.where(lane < 6, 1.0, 0.0)).astype(BF16)
            ka_ref[h] = jnp.where(lane < 3, 1.0, jnp.where(lane < 6, -pieces, 0.0)).astype(BF16)

    out = pl.BlockSpec((N_FOX, tc, HEAD), lambda i: (0, i, 0))
    return pl.pallas_call(
        body, name=name, grid=(S // tc,),
        in_specs=[pl.BlockSpec((tc, LANE), lambda i: (i, 0)), pl.BlockSpec((1, LANE), lambda i: (0, 0))],
        out_specs=[out, out],
        out_shape=[jax.ShapeDtypeStruct((N_FOX, S, HEAD), BF16)] * 2,
        scratch_shapes=[pltpu.VMEM((8, LANE), F32)],
        compiler_params=_params(("arbitrary",)),
    )(pfl, bf)


def _fox_dlogit(dc, pfl, bf, *, tc, name):
    S = pfl.shape[0]
    nt = S // tc

    def body(dc_ref, fl_ref, bf_ref, dfl_ref, dbf_ref, carry_ref):
        i = pl.program_id(0)

        @pl.when(i == 0)
        def _():
            carry_ref[...] = jnp.zeros_like(carry_ref)
            dbf_ref[...] = jnp.zeros_like(dbf_ref)

        dlogf = jnp.dot(_tri(tc, False), dc_ref[...], precision=lax.Precision.HIGHEST,
                        preferred_element_type=F32) + carry_ref[0:1, :]
        carry_ref[0:1, :] = dlogf[0:1, :]
        dz = dlogf * _sigmoid(-(fl_ref[...] + bf_ref[...]))
        dfl_ref[...] = dz.astype(BF16)
        dbf_ref[0:1, :] += jnp.sum(dz, axis=0, keepdims=True)

    rev = pl.BlockSpec((tc, LANE), lambda i: (nt - 1 - i, 0))
    return pl.pallas_call(
        body, name=name, grid=(nt,),
        in_specs=[rev, rev, pl.BlockSpec((1, LANE), lambda i: (0, 0))],
        out_specs=[rev, pl.BlockSpec((8, LANE), lambda i: (0, 0))],
        out_shape=[jax.ShapeDtypeStruct((S, LANE), BF16), jax.ShapeDtypeStruct((8, LANE), F32)],
        scratch_shapes=[pltpu.VMEM((8, LANE), F32)],
        compiler_params=_params(("arbitrary",)),
    )(dc, pfl, bf)


def _as_row(col):
    return jnp.transpose(jnp.broadcast_to(col, (col.shape[0], LANE)))[0:1, :]


def _causal_part(rows, cols, row0, col0, q_is_row=True):
    r = lax.broadcasted_iota(jnp.int32, (rows, cols), 0) + row0
    c = lax.broadcasted_iota(jnp.int32, (rows, cols), 1) + col0
    return (r >= c) if q_is_row else (c >= r)


LOG2E = 1.4426950408889634
FOX_SCALE2 = (HEAD ** -0.5) * LOG2E


def _fox_fwd(qkv, pg, qa, ka, y_all, *, tq, nsub, name):
    S = qkv.shape[0]
    nq = S // tq
    rs = tq // nsub
    ybase = CONV_W // HEAD

    def body(q_ref, qa_ref, k_ref, ka_ref, v_ref, g_ref, yin_ref, y_ref, o_ref, m_ref, linv_ref):
        i = pl.program_id(1)
        q = jnp.concatenate([q_ref[...], qa_ref[...]], axis=1)
        qs = [q[r * rs:(r + 1) * rs] for r in range(nsub)]
        ones = jnp.where(lax.broadcasted_iota(jnp.int32, (tq, HEAD), 1) == 0, 1.0, 0.0).astype(BF16)

        def blk(j, carry, masked):
            off = pl.multiple_of(j * tq, tq)
            kj = jnp.concatenate([k_ref[pl.ds(off, tq), :], ka_ref[pl.ds(off, tq), :]], axis=1)
            vj = jnp.concatenate([v_ref[pl.ds(off, tq), :], ones], axis=1)
            out = []
            for r in range(nsub):
                m, acc = carry[r]
                s = lax.dot_general(qs[r], kj, NT, preferred_element_type=F32) * FOX_SCALE2
                if masked:
                    s = jnp.where(_causal_part(rs, tq, r * rs, 0), s, NEG)
                m_new = jnp.maximum(m, jnp.ceil(jnp.max(s, axis=-1, keepdims=True)))
                pb = jnp.exp2(s - m_new).astype(BF16)
                acc = jnp.exp2(m - m_new) * acc + jnp.dot(pb, vj, preferred_element_type=F32)
                out.append((m_new, acc))
            return tuple(out)

        init = tuple((jnp.full((rs, 1), NEG, F32), jnp.zeros((rs, 2 * HEAD), F32)) for _ in range(nsub))
        carry = lax.fori_loop(0, i, lambda j, c: blk(j, c, False), init)
        carry = blk(i, carry, True)
        m = jnp.concatenate([c[0] for c in carry], axis=0)
        acc = jnp.concatenate([c[1] for c in carry], axis=0)
        linv = 1.0 / acc[:, HEAD:HEAD + 1]
        o = acc[:, :HEAD] * linv
        g = g_ref[...]
        y_ref[...] = (o * g * _sigmoid(g)).astype(BF16)
        o_ref[...] = o
        m_ref[...] = _as_row(m)
        linv_ref[...] = _as_row(linv)

    tile = lambda base: pl.BlockSpec((tq, HEAD), lambda h, i, base=base: (i, base + h))
    full = lambda base: pl.BlockSpec((S, HEAD), lambda h, i, base=base: (0, base + h))
    rowv = pl.BlockSpec((None, None, 1, tq), lambda h, i: (h, i, 0, 0))
    stat = jax.ShapeDtypeStruct((N_FOX, nq, 1, tq), F32)
    return pl.pallas_call(
        body, name=name, grid=(N_FOX, nq),
        in_specs=[tile(0), pl.BlockSpec((None, tq, HEAD), lambda h, i: (h, i, 0)),
                  full(N_FOX), pl.BlockSpec((None, S, HEAD), lambda h, i: (h, 0, 0)),
                  full(2 * N_FOX), tile(0), ANY],
        out_specs=[tile(ybase), tile(0), rowv, rowv],
        out_shape=[jax.ShapeDtypeStruct(y_all.shape, BF16), jax.ShapeDtypeStruct((S, FOX_W), F32), stat, stat],
        input_output_aliases={6: 0},
        compiler_params=_params(("parallel", "arbitrary")),
    )(qkv, qa, qkv, ka, qkv, pg, y_all)


def _fox_bwd_prep(dy, o, pg, *, tq, name):
    S = o.shape[0]
    base = CONV_W // HEAD

    def body(dy_ref, o_ref, g_ref, do_ref, dg_ref, dl_ref):
        g = g_ref[...]
        sg = _sigmoid(g)
        dyv = dy_ref[...]
        ov = o_ref[...]
        dob = (dyv * g * sg).astype(BF16)
        do_ref[...] = dob
        dg_ref[...] = (dyv * ov * _dsilu(g, sg)).astype(BF16)
        dl_ref[...] = _as_row(jnp.sum(dob.astype(F32) * ov, axis=-1, keepdims=True))

    tile = lambda b: pl.BlockSpec((tq, HEAD), lambda h, i, b=b: (i, b + h))
    return pl.pallas_call(
        body, name=name, grid=(N_FOX, S // tq),
        in_specs=[tile(base), tile(0), tile(0)],
        out_specs=[tile(0), tile(0), pl.BlockSpec((None, None, 1, tq), lambda h, i: (h, i, 0, 0))],
        out_shape=[jax.ShapeDtypeStruct((S, FOX_W), BF16), jax.ShapeDtypeStruct((S, FOX_W), BF16),
                   jax.ShapeDtypeStruct((N_FOX, S // tq, 1, tq), F32)],
        compiler_params=_params(("parallel", "parallel")),
    )(dy, o, pg)


def _fox_bwd(qkv, do, qa, ka, m_row, linv_row, delta_row, *, tq, nsub, name):
    S = qkv.shape[0]
    nq = S // tq
    cs = tq // nsub
    scale = HEAD ** -0.5

    def body(k_ref, ka_ref, v_ref, q_ref, qa_ref, do_ref, mr_ref, lir_ref, dlr_ref, dq_ref, dk_ref, dv_ref, dc_ref):
        j = pl.program_id(1)

        @pl.when(j == 0)
        def _():
            dq_ref[...] = jnp.zeros_like(dq_ref)

        kj = k_ref[...]
        kja = jnp.concatenate([kj, ka_ref[...]], axis=1)
        vj = v_ref[...]

        def blk(i, carry, masked):
            dk, dv, dc = carry
            m_i = mr_ref[i]
            linv_i = lir_ref[i]
            dl_i = dlr_ref[i]
            for c in range(nsub):
                off = pl.multiple_of(i * tq + c * cs, cs)
                cols = slice(c * cs, (c + 1) * cs)
                qi = q_ref[pl.ds(off, cs), :]
                qia = jnp.concatenate([qi, qa_ref[pl.ds(off, cs), :]], axis=1)
                doi = do_ref[pl.ds(off, cs), :]
                st = lax.dot_general(kja, qia, NT, preferred_element_type=F32) * FOX_SCALE2
                if masked:
                    st = jnp.where(_causal_part(tq, cs, 0, c * cs, q_is_row=False), st, NEG)
                pt = jnp.exp2(st - m_i[:, cols]).astype(BF16).astype(F32) * linv_i[:, cols]
                dv = dv + jnp.dot(pt.astype(BF16), doi, preferred_element_type=F32)
                dpt = lax.dot_general(vj, doi, NT, preferred_element_type=F32)
                dst = pt * (dpt - dl_i[:, cols])
                dsb = dst.astype(BF16)
                dk = dk + jnp.dot(dsb, qi, preferred_element_type=F32)
                dq_ref[pl.ds(off, cs), :] += lax.dot_general(dsb, kj, TN, preferred_element_type=F32) * scale
                dc = dc - jnp.sum(dst, axis=-1, keepdims=True)
            return dk, dv, dc

        init = (jnp.zeros((tq, HEAD), F32), jnp.zeros((tq, HEAD), F32), jnp.zeros((tq, 1), F32))
        carry = blk(j, init, True)
        dk, dv, dc = lax.fori_loop(j + 1, nq, lambda i, c: blk(i, c, False), carry)
        dk_ref[...] = (dk * scale).astype(BF16)
        dv_ref[...] = dv.astype(BF16)
        dc_ref[...] = _as_row(dc)

    tile = lambda base: pl.BlockSpec((tq, HEAD), lambda h, j, base=base: (j, base + h))
    full = lambda base: pl.BlockSpec((S, HEAD), lambda h, j, base=base: (0, base + h))
    atile = pl.BlockSpec((None, tq, HEAD), lambda h, j: (h, j, 0))
    afull = pl.BlockSpec((None, S, HEAD), lambda h, j: (h, 0, 0))
    rowt = pl.BlockSpec((None, None, 1, tq), lambda h, j: (h, j, 0, 0))
    rowv = pl.BlockSpec((None, nq, 1, tq), lambda h, j: (h, 0, 0, 0))
    return pl.pallas_call(
        body, name=name, grid=(N_FOX, nq),
        in_specs=[tile(N_FOX), atile, tile(2 * N_FOX), full(0), afull, full(0), rowv, rowv, rowv],
        out_specs=[full(0), tile(0), tile(0), rowt],
        out_shape=[jax.ShapeDtypeStruct((S, FOX_W), F32), jax.ShapeDtypeStruct((S, FOX_W), BF16),
                   jax.ShapeDtypeStruct((S, FOX_W), BF16), jax.ShapeDtypeStruct((N_FOX, nq, 1, tq), F32)],
        compiler_params=_params(("arbitrary", "arbitrary")),
    )(qkv, ka, qkv, qkv, qa, do, m_row, linv_row, delta_row)


def _mem_heads(mq, mkv, h):
    lo = h * HEAD
    qh = mq[:, lo:lo + HEAD].astype(BF16)
    kh = mkv[:, lo:lo + HEAD].astype(BF16)
    vh = mkv[:, MEM_W + lo:MEM_W + lo + HEAD].astype(BF16)
    return qh, kh, vh


def _mem_softmax(qh, kh):
    s = lax.dot_general(qh, kh, NT, preferred_element_type=F32) * (HEAD ** -0.5)
    e = jnp.exp(s - jnp.max(s, axis=-1, keepdims=True))
    return e / jnp.sum(e, axis=-1, keepdims=True)


def _mem_fwd(pg, mkv, y_all, *, tq, name):
    S = pg.shape[0]
    M = mkv.shape[0]
    qb = FOX_W // MEM_W
    yb = (CONV_W + FOX_W) // MEM_W

    def body(mq_ref, g_ref, mkv_ref, yin_ref, y_ref):
        mq = mq_ref[...]
        mkvv = mkv_ref[...]
        for h in range(N_MEMH):
            qh, kh, vh = _mem_heads(mq, mkvv, h)
            p = _mem_softmax(qh, kh)
            o = jnp.dot(p.astype(BF16), vh, preferred_element_type=F32)
            g = g_ref[:, h * HEAD:(h + 1) * HEAD]
            y_ref[:, h * HEAD:(h + 1) * HEAD] = (o * g * _sigmoid(g)).astype(BF16)

    return pl.pallas_call(
        body, name=name, grid=(S // tq,),
        in_specs=[pl.BlockSpec((tq, MEM_W), lambda i: (i, qb)), pl.BlockSpec((tq, MEM_W), lambda i: (i, qb + 1)),
                  pl.BlockSpec((M, 2 * MEM_W), lambda i: (0, 0)), ANY],
        out_specs=pl.BlockSpec((tq, MEM_W), lambda i: (i, yb)),
        out_shape=jax.ShapeDtypeStruct(y_all.shape, BF16),
        input_output_aliases={3: 0},
        compiler_params=_params(("parallel",)),
    )(pg, pg, mkv, y_all)


def _mem_bwd(pg, mkv, dy, *, tq, name):
    S = pg.shape[0]
    M = mkv.shape[0]
    qb = FOX_W // MEM_W
    yb = (CONV_W + FOX_W) // MEM_W
    scale = HEAD ** -0.5

    def body(mq_ref, g_ref, mkv_ref, dy_ref, dmq_ref, dg_ref, dmkv_ref):
        i = pl.program_id(0)

        @pl.when(i == 0)
        def _():
            dmkv_ref[...] = jnp.zeros_like(dmkv_ref)

        mq = mq_ref[...]
        mkvv = mkv_ref[...]
        for h in range(N_MEMH):
            lo = h * HEAD
            qh, kh, vh = _mem_heads(mq, mkvv, h)
            p = _mem_softmax(qh, kh)
            o = jnp.dot(p.astype(BF16), vh, preferred_element_type=F32)
            g = g_ref[:, lo:lo + HEAD]
            sg = _sigmoid(g)
            dyh = dy_ref[:, lo:lo + HEAD]
            do = dyh * g * sg
            dg_ref[:, lo:lo + HEAD] = (dyh * o * _dsilu(g, sg)).astype(BF16)
            dob = do.astype(BF16)
            dp = lax.dot_general(dob, vh, NT, preferred_element_type=F32)
            ds = p * (dp - jnp.sum(do * o, axis=-1, keepdims=True))
            dsb = ds.astype(BF16)
            dmq_ref[:, lo:lo + HEAD] = (jnp.dot(dsb, kh, preferred_element_type=F32) * scale).astype(BF16)
            dmkv_ref[:, lo:lo + HEAD] += lax.dot_general(dsb, qh, TN, preferred_element_type=F32) * scale
            dmkv_ref[:, MEM_W + lo:MEM_W + lo + HEAD] += lax.dot_general(
                p.astype(BF16), dob, TN, preferred_element_type=F32)

    return pl.pallas_call(
        body, name=name, grid=(S // tq,),
        in_specs=[pl.BlockSpec((tq, MEM_W), lambda i: (i, qb)), pl.BlockSpec((tq, MEM_W), lambda i: (i, qb + 1)),
                  pl.BlockSpec((M, 2 * MEM_W), lambda i: (0, 0)), pl.BlockSpec((tq, MEM_W), lambda i: (i, yb))],
        out_specs=[pl.BlockSpec((tq, MEM_W), lambda i: (i, 0)), pl.BlockSpec((tq, MEM_W), lambda i: (i, 0)),
                   pl.BlockSpec((M, 2 * MEM_W), lambda i: (0, 0))],
        out_shape=[jax.ShapeDtypeStruct((S, MEM_W), BF16), jax.ShapeDtypeStruct((S, MEM_W), BF16),
                   jax.ShapeDtypeStruct((M, 2 * MEM_W), F32)],
        compiler_params=_params(("arbitrary",)),
    )(pg, pg, mkv, dy)


def _final(x2, target, fg, *, tm, name):
    S, D = x2.shape

    def body(x_ref, t_ref, g_ref, dx_ref, dxb_ref, dg_ref, ls_ref):
        i = pl.program_id(0)

        @pl.when(i == 0)
        def _():
            dg_ref[...] = jnp.zeros_like(dg_ref)
            ls_ref[...] = jnp.zeros_like(ls_ref)

        xv = x_ref[...]
        r = lax.rsqrt(jnp.mean(xv * xv, axis=-1, keepdims=True) + EPS)
        n = xv * r
        g = g_ref[...]
        diff = n * g - t_ref[...]
        ls_ref[...] += jnp.sum(diff * diff)
        dout = diff * (1.0 / D)
        dg_ref[...] += jnp.sum(dout * n, axis=0, keepdims=True)
        dn = dout * g
        dx = r * (dn - n * jnp.mean(dn * n, axis=-1, keepdims=True))
        dx_ref[...] = dx
        dxb_ref[...] = dx.astype(BF16)

    row = pl.BlockSpec((tm, D), lambda i: (i, 0))
    vec = pl.BlockSpec((1, D), lambda i: (0, 0))
    return pl.pallas_call(
        body, name=name, grid=(S // tm,),
        in_specs=[row, row, vec],
        out_specs=[row, row, vec, pl.BlockSpec((8, LANE), lambda i: (0, 0))],
        out_shape=[jax.ShapeDtypeStruct((S, D), F32), jax.ShapeDtypeStruct((S, D), BF16),
                   jax.ShapeDtypeStruct((1, D), F32), jax.ShapeDtypeStruct((8, LANE), F32)],
        compiler_params=_params(("arbitrary",)),
    )(x2, target, fg)


def _adamw(w, g, m, v, *, name):
    R, C = w.shape
    tr, tc = R, C
    for cand in (256, 128, 64, 32, 16, 8):
        if R % cand == 0 and R > cand:
            tr = cand
            break
    if tr == R and R > 256 and C % 256 == 0:
        tc = 256
    c1 = 1.0 - ADAM_B1 ** ADAM_STEP
    c2 = 1.0 - ADAM_B2 ** ADAM_STEP

    def body(w_ref, g_ref, m_ref, v_ref, d_ref, nm_ref, nv_ref):
        gv = g_ref[...]
        nm = ADAM_B1 * m_ref[...] + (1.0 - ADAM_B1) * gv
        nv = ADAM_B2 * v_ref[...] + (1.0 - ADAM_B2) * (gv * gv)
        nm_ref[...] = nm
        nv_ref[...] = nv
        d_ref[...] = -ADAM_LR * ((nm / c1) / (jnp.sqrt(nv / c2) + ADAM_EPS) + ADAM_WD * w_ref[...])

    spec = pl.BlockSpec((tr, tc), lambda i, j: (i, j))
    shp = jax.ShapeDtypeStruct((R, C), F32)
    return pl.pallas_call(
        body, name=name, grid=(R // tr, C // tc),
        in_specs=[spec] * 4, out_specs=[spec] * 3, out_shape=[shp] * 3,
        compiler_params=_params(("parallel", "parallel")),
    )(w, g, m, v)


def _sum4(q, own, me, *, name):
    _, R, C = q.shape
    tr = R
    for cand in (256, 128, 64, 32, 16, 8):
        if R % cand == 0 and R > cand:
            tr = cand
            break

    def body(me_ref, own_ref, q1_ref, q2_ref, q3_ref, o_ref):
        f = lambda r: r[...].astype(F32)
        o_ref[...] = ((f(own_ref) + f(q1_ref)) + f(q2_ref)) + f(q3_ref)

    blk = lambda d: pl.BlockSpec((None, tr, C), lambda i, me_ref, d=d: (me_ref[0] ^ d, i, 0))
    return pl.pallas_call(
        body, name=name,
        grid_spec=pltpu.PrefetchScalarGridSpec(
            num_scalar_prefetch=1, grid=(R // tr,),
            in_specs=[blk(0), blk(1), blk(2), blk(3)],
            out_specs=pl.BlockSpec((tr, C), lambda i, me_ref: (i, 0))),
        out_shape=jax.ShapeDtypeStruct((R, C), F32),
        compiler_params=_params(("parallel",)),
    )(jnp.reshape(me, (1,)).astype(jnp.int32), own, q, q, q)


def _add_sibling(g, got, c, *, name):
    K, _, R, C = g.shape
    tr = R
    for cand in (256, 128, 64, 32, 16, 8):
        if R % cand == 0 and R > cand:
            tr = cand
            break

    def body(c_ref, a_ref, b_ref, o_ref):
        o_ref[...] = (a_ref[...] + b_ref[...]).astype(BF16)

    spec = pl.BlockSpec((None, tr, C), lambda k, i, c_ref: (k, i, 0))
    return pl.pallas_call(
        body, name=name,
        grid_spec=pltpu.PrefetchScalarGridSpec(
            num_scalar_prefetch=1, grid=(K, R // tr),
            in_specs=[pl.BlockSpec((None, None, tr, C), lambda k, i, c_ref: (k, c_ref[0], i, 0)), spec],
            out_specs=spec),
        out_shape=jax.ShapeDtypeStruct((K, R, C), BF16),
        compiler_params=_params(("parallel", "parallel")),
    )(jnp.reshape(c, (1,)).astype(jnp.int32), g, got)


ANY = pl.BlockSpec(memory_space=pl.ANY)


def _other_chips(x, y):
    return [(d, 1 - x if d & 2 else x, 1 - y if d & 1 else y) for d in (1, 2, 3)]


def _gather_ici_copies(src_refs, out_refs, ici_send, ici_recv):
    x, y, c = lax.axis_index("x"), lax.axis_index("y"), lax.axis_index("c")
    me = 2 * x + y
    return [pltpu.make_async_remote_copy(
        src_ref=src_refs[t].at[c], dst_ref=out_refs[t].at[me, c],
        send_sem=ici_send.at[3 * t + d - 1], recv_sem=ici_recv.at[3 * t + d - 1],
        device_id=(tx, ty, c), device_id_type=MESH)
        for t in range(len(src_refs)) for d, tx, ty in _other_chips(x, y)]


def _gather_finish(src_refs, out_refs, ici_send, ici_recv, d2d_send, d2d_recv):
    n = len(src_refs)
    x, y, c = lax.axis_index("x"), lax.axis_index("y"), lax.axis_index("c")
    passed = []
    for t in range(n):
        for d, tx, ty in _other_chips(x, y):
            landed = out_refs[t].at[2 * tx + ty, c]
            pltpu.make_async_remote_copy(
                src_ref=landed, dst_ref=landed,
                send_sem=ici_send.at[3 * t + d - 1], recv_sem=ici_recv.at[3 * t + d - 1],
                device_id=(tx, ty, c), device_id_type=MESH).wait_recv()
            cp = pltpu.make_async_remote_copy(
                src_ref=landed, dst_ref=landed,
                send_sem=d2d_send.at[3 * t + d - 1], recv_sem=d2d_recv.at[3 * t + d - 1],
                device_id=(x, y, 1 - c), device_id_type=MESH)
            cp.start()
            passed.append(cp)
    for t in range(n):
        for d, tx, ty in _other_chips(x, y):
            theirs = out_refs[t].at[2 * tx + ty, 1 - c]
            pltpu.make_async_remote_copy(
                src_ref=theirs, dst_ref=theirs,
                send_sem=d2d_send.at[3 * t + d - 1], recv_sem=d2d_recv.at[3 * t + d - 1],
                device_id=(x, y, 1 - c), device_id_type=MESH).wait_recv()
    for cp in _gather_ici_copies(src_refs, out_refs, ici_send, ici_recv) + passed:
        cp.wait_send()


def _call_with_gather(body, shards, first, last, *, n_in, n_out, in_specs, out_specs, out_shape, scratch_shapes,
                      **kw):
    n = len(shards)
    n_scr = len(scratch_shapes)

    def wrapped(*refs):
        ins, srcs = refs[:n_in], refs[n_in:n_in + n]
        outs = refs[n_in + n:n_in + n + n_out]
        gouts = refs[n_in + n + n_out:n_in + 2 * n + n_out]
        scr = refs[n_in + 2 * n + n_out:n_in + 2 * n + n_out + n_scr]
        ici_send, ici_recv, d2d_send, d2d_recv = refs[n_in + 2 * n + n_out + n_scr:]

        @pl.when(first())
        def _():
            for cp in _gather_ici_copies(srcs, gouts, ici_send, ici_recv):
                cp.start()

        body(*ins, *outs, *scr)

        @pl.when(last())
        def _():
            _gather_finish(srcs, gouts, ici_send, ici_recv, d2d_send, d2d_recv)

    sem = pltpu.SemaphoreType.DMA((3 * n,))
    return pl.pallas_call(
        wrapped,
        in_specs=list(in_specs) + [ANY] * n,
        out_specs=list(out_specs) + [ANY] * n,
        out_shape=list(out_shape) + [jax.ShapeDtypeStruct((N_CHIPS,) + s.shape, s.dtype) for s in shards],
        scratch_shapes=list(scratch_shapes) + [sem, sem, sem, sem],
        **kw)


def _scatter_copies(src_refs, out_refs, send_sems, recv_sems):
    n = len(src_refs)
    x, y, c = lax.axis_index("x"), lax.axis_index("y"), lax.axis_index("c")
    me = 2 * x + y
    copies = []
    for t in range(n):
        for d, tx, ty in _other_chips(x, y):
            copies.append(pltpu.make_async_remote_copy(
                src_ref=src_refs[t].at[2 * tx + ty], dst_ref=out_refs[t].at[me],
                send_sem=send_sems.at[3 * t + d - 1], recv_sem=recv_sems.at[3 * t + d - 1],
                device_id=(tx, ty, c), device_id_type=MESH))
    return copies


def _mm_scatter(a, b, add, srcs, *, tm, tn, tk, name):
    M, K = a.shape
    N = b.shape[1]
    tm, tn, tk = min(tm, M), min(tn, N), min(tk, K)
    assert M % tm == 0 and N % tn == 0 and K % tk == 0, (name, M, N, K, tm, tn, tk)
    gi, gj, gk = M // tm, N // tn, K // tk
    n = len(srcs)

    def body(*refs):
        a_ref, b_ref, add_ref = refs[:3]
        src_refs = refs[3:3 + n]
        o_ref = refs[3 + n]
        out_refs = refs[4 + n:4 + 2 * n]
        acc_ref, send_sems, recv_sems = refs[4 + 2 * n:]
        i, j, k = pl.program_id(0), pl.program_id(1), pl.program_id(2)

        @pl.when((i == 0) & (j == 0) & (k == 0))
        def _():
            for cp in _scatter_copies(src_refs, out_refs, send_sems, recv_sems):
                cp.start()

        @pl.when(k == 0)
        def _():
            acc_ref[...] = jnp.zeros_like(acc_ref)

        acc_ref[...] += jnp.dot(a_ref[...].astype(BF16), b_ref[...].astype(BF16), preferred_element_type=F32)

        @pl.when(k == gk - 1)
        def _():
            o_ref[...] = acc_ref[...] + add_ref[...]

        @pl.when((i == gi - 1) & (j == gj - 1) & (k == gk - 1))
        def _():
            for cp in _scatter_copies(src_refs, out_refs, send_sems, recv_sems):
                cp.wait()

    o_spec = pl.BlockSpec((tm, tn), lambda i, j, k: (i, j))
    sem = pltpu.SemaphoreType.DMA((3 * n,))
    res = pl.pallas_call(
        body, name=name, grid=(gi, gj, gk),
        in_specs=[pl.BlockSpec((tm, tk), lambda i, j, k: (i, k)), pl.BlockSpec((tk, tn), lambda i, j, k: (k, j)),
                  o_spec] + [ANY] * n,
        out_specs=[o_spec] + [ANY] * n,
        out_shape=[jax.ShapeDtypeStruct((M, N), F32)] + [jax.ShapeDtypeStruct(s.shape, s.dtype) for s in srcs],
        scratch_shapes=[pltpu.VMEM((tm, tn), F32), sem, sem],
        compiler_params=_params(("arbitrary", "arbitrary", "arbitrary")),
    )(a, b, add, *srcs)
    return res[0], res[1:]


def _swap_halves(grads, *, name):
    n = len(grads)

    def body(*refs):
        src_refs, out_refs = refs[:n], refs[n:2 * n]
        send_sems, recv_sems = refs[2 * n:]
        x, y, c = lax.axis_index("x"), lax.axis_index("y"), lax.axis_index("c")
        copies = []
        for t in range(n):
            for k in range(N_CHIPS):
                cp = pltpu.make_async_remote_copy(
                    src_ref=src_refs[t].at[k, 1 - c], dst_ref=out_refs[t].at[k],
                    send_sem=send_sems.at[N_CHIPS * t + k], recv_sem=recv_sems.at[N_CHIPS * t + k],
                    device_id=(x, y, 1 - c), device_id_type=MESH)
                cp.start()
                copies.append(cp)
        for cp in copies:
            cp.wait()

    sem = pltpu.SemaphoreType.DMA((N_CHIPS * n,))
    return pl.pallas_call(
        body, name=name,
        in_specs=[ANY] * n, out_specs=[ANY] * n,
        out_shape=[jax.ShapeDtypeStruct((N_CHIPS,) + g.shape[2:], g.dtype) for g in grads],
        scratch_shapes=[sem, sem],
    )(*grads)


def _sibling_swap(srcs, *, name):
    n = len(srcs)

    def body(*refs):
        src_refs, out_refs = refs[:n], refs[n:2 * n]
        send_sems, recv_sems = refs[2 * n:]
        x, y, c = lax.axis_index("x"), lax.axis_index("y"), lax.axis_index("c")
        copies = []
        for t in range(n):
            cp = pltpu.make_async_remote_copy(
                src_ref=src_refs[t], dst_ref=out_refs[t],
                send_sem=send_sems.at[t], recv_sem=recv_sems.at[t],
                device_id=(x, y, 1 - c), device_id_type=MESH)
            cp.start()
            copies.append(cp)
        for cp in copies:
            cp.wait()

    sem = pltpu.SemaphoreType.DMA((n,))
    return pl.pallas_call(
        body, name=name,
        in_specs=[ANY] * n, out_specs=[ANY] * n,
        out_shape=[jax.ShapeDtypeStruct(s.shape, s.dtype) for s in srcs],
        scratch_shapes=[sem, sem],
    )(*srcs)


def _allreduce_small(v, *, name):
    R, C = v.shape

    def body(v_ref, o_ref, buf_ref, send_sems, recv_sems):
        x, y, c = lax.axis_index("x"), lax.axis_index("y"), lax.axis_index("c")
        me = 4 * x + 2 * y + c
        buf_ref[me] = v_ref[...]
        copies = []
        for d in range(1, N_DEV):
            tx = 1 - x if d & 4 else x
            ty = 1 - y if d & 2 else y
            tc = 1 - c if d & 1 else c
            cp = pltpu.make_async_remote_copy(
                src_ref=v_ref, dst_ref=buf_ref.at[me],
                send_sem=send_sems.at[d - 1], recv_sem=recv_sems.at[d - 1],
                device_id=(tx, ty, tc), device_id_type=MESH)
            cp.start()
            copies.append(cp)
        for cp in copies:
            cp.wait()
        acc = buf_ref[0]
        for k in range(1, N_DEV):
            acc = acc + buf_ref[k]
        o_ref[...] = acc

    return pl.pallas_call(
        body, name=name,
        in_specs=[pl.BlockSpec(memory_space=pltpu.VMEM)],
        out_specs=pl.BlockSpec(memory_space=pltpu.VMEM),
        out_shape=jax.ShapeDtypeStruct((R, C), F32),
        scratch_shapes=[pltpu.VMEM((N_DEV, R, C), F32), pltpu.SemaphoreType.DMA((N_DEV - 1,)),
                        pltpu.SemaphoreType.DMA((N_DEV - 1,))],
    )(v)


_A0, _B0, _GC0 = 0, CONV_W, 2 * CONV_W
_Q0 = 3 * CONV_W
_FL0 = _Q0 + 3 * FOX_W
_FG0 = _FL0 + N_FOX
_MQ0 = _FG0 + FOX_W
_MG0 = _MQ0 + MEM_W
_DIN = _MG0 + MEM_W
_WMAIN = _DIN - N_FOX
_PC_N = 3 * CONV_W
_QKV_N = 3 * FOX_W
_PG_N = FOX_W + 2 * MEM_W


def _rows_pad8(a):
    r = a.shape[0]
    p = (-r) % 8
    return jnp.pad(a, ((0, p), (0, 0))) if p else a


def _step(x, mem, target, norm_g, mem_norm_g, final_g, b_f, conv_b, ln_g, ln_b,
          s_in, s_pw, s_mkv, s_out, s_cw, ci, me):
    S, D = x.shape
    M = mem.shape[0]
    tq = min(512, S)
    tf = min(1024, S)
    tt = min(256, S)
    tm = min(256, S)
    nin = s_in.shape[2]
    bf_pad = jnp.pad(b_f, ((0, 0), (0, LANE - N_FOX)))
    whole = lambda g, s: lax.dynamic_update_slice(g, s[None], (me, 0, 0, 0))

    h, (g_in,) = _rms_fwd(x, norm_g, tm=tm, name="rms_fwd_gather_w_in", gather=[s_in])
    h_t = h.T
    w_in = jnp.transpose(whole(g_in, s_in).reshape(N_CHIPS, D, nin), (1, 0, 2)).reshape(D, N_CHIPS * nin)
    w_main = jnp.concatenate([w_in[:, :_FL0], w_in[:, _FG0:]], axis=1)
    w_fl = jnp.pad(w_in[:, _FL0:_FG0], ((0, 0), (0, LANE - N_FOX)))
    w_main_t, w_fl_t = w_main.T, w_fl.T

    pc, (g_pw, g_mkv, g_out, g_cw) = _mm(h, w_main, tm=1024, tn=512, tk=2048, n=_PC_N, b_off=0,
                                         name="proj_conv_gather_rest", gather=[s_pw, s_mkv, s_out, s_cw])
    w_pw = whole(g_pw, s_pw).reshape(CONV_W, CONV_W)
    w_mkv = whole(g_mkv, s_mkv).reshape(D, 2 * MEM_W)
    w_out = whole(g_out, s_out).reshape(MIX_W, D)
    w_out_t = w_out.T
    cw_pad = jnp.transpose(whole(g_cw, s_cw).reshape(N_CHIPS, CONV_HALO, CONV_W // N_CHIPS),
                           (1, 0, 2)).reshape(CONV_HALO, CONV_W)

    qkv = _mm(h, w_main, out_dtype=BF16, tm=1024, tn=512, tk=2048, n=_QKV_N, b_off=_PC_N // 512, name="proj_qkv")
    pg = _mm(h, w_main, tm=1024, tn=512, tk=2048, n=_PG_N, b_off=(_PC_N + _QKV_N) // 512, name="proj_gate")
    pfl = _mm(h, w_fl, tm=1024, tn=LANE, tk=2048, name="proj_logit")

    y, u1s = _conv_fwd(pc, cw_pad, conv_b, ln_g, ln_b, w_pw, tt=tt, name="conv_fwd")

    qa, ka = _fox_cumsum(pfl, bf_pad, tc=tt, name="fox_cumsum")
    y, o_fox, fox_m, fox_linv = _fox_fwd(qkv, pg, qa, ka, y, tq=tf, nsub=2, name="fox_fwd")

    hm = _rms_fwd(mem, mem_norm_g, tm=min(256, M), name="rms_mem")
    mkv = _mm(hm, w_mkv, tm=256, tn=512, tk=2048, name="mem_kv")
    y = _mem_fwd(pg, mkv, y, tq=tq, name="mem_fwd")

    x2 = _mm(y, w_out, add=x, tm=1024, tn=512, tk=2048, name="out_proj")
    dx2, dx2_b, dfg, sq = _final(x2, target, final_g.reshape(1, D), tm=tm, name="final")

    dy = _mm(dx2_b, w_out_t, tm=1024, tn=512, tk=2048, name="d_y")
    dw_out = _mm(y.T, dx2_b, tm=1024, tn=512, tk=2048, name="d_w_out")

    dpc, dw_pw, dcw, dsm = _conv_bwd(pc, u1s, dy, cw_pad, ln_g, ln_b, w_pw, tt=tt, name="conv_bwd")

    do, dfgate, delta = _fox_bwd_prep(dy, o_fox, pg, tq=tf, name="fox_bwd_prep")
    dq, dk, dv, dc = _fox_bwd(qkv, do, qa, ka, fox_m, fox_linv, delta, tq=tf, nsub=2, name="fox_bwd")
    dc_pad = jnp.pad(dc.reshape(N_FOX, S).T, ((0, 0), (0, LANE - N_FOX)))
    dfl, dbf = _fox_dlogit(dc_pad, pfl, bf_pad, tc=tt, name="fox_dlogit")

    dmq, dmgate, dmkv = _mem_bwd(pg, mkv, dy, tq=tq, name="mem_bwd")
    dw_mkv = _mm(hm, dmkv, ta=True, tm=512, tn=512, tk=256, name="d_w_mkv")
    dhm = _mm(dmkv, w_mkv, tb=True, tm=256, tn=512, tk=1024, name="d_hm")
    _, dmg = _rms_bwd(mem, mem_norm_g, dhm, None, tm=min(256, M), name="rms_mem_bwd")

    dproj = jnp.concatenate([dpc, dq.astype(BF16), dk, dv, dfgate, dmq, dmgate], axis=1)
    dw_main = _mm(h_t, dproj, tm=1024, tn=_WMAIN // 4, tk=1024, name="d_w_main")
    dw_fl = _mm(h_t, dfl, tm=1024, tn=LANE, tk=1024, name="d_w_logit")

    def chip_columns(k):
        lo, hi = k * nin, (k + 1) * nin
        parts = []
        if lo < _FL0:
            parts.append(dw_main[:, lo:min(hi, _FL0)])
        if lo < _FG0 and hi > _FL0:
            parts.append(dw_fl[:, max(lo, _FL0) - _FL0:min(hi, _FG0) - _FL0])
        if hi > _FG0:
            parts.append(dw_main[:, max(lo, _FG0) - N_FOX:hi - N_FOX])
        return parts[0] if len(parts) == 1 else jnp.concatenate(parts, axis=1)

    big = [jnp.stack([chip_columns(k) for k in range(N_CHIPS)]),
           dw_pw.reshape(N_CHIPS, CONV_W // N_CHIPS, CONV_W),
           dw_mkv.reshape(N_CHIPS, D // N_CHIPS, 2 * MEM_W),
           dw_out.reshape(N_CHIPS, MIX_W // N_CHIPS, D)]
    big = [g.reshape(N_CHIPS, 2, g.shape[1] // 2, g.shape[2]) for g in big]
    got = _swap_halves(big, name="grad_swap_halves")
    chip = [_add_sibling(g, o, ci, name=f"grad_add_sibling_{t}") for t, (g, o) in enumerate(zip(big, got))]

    dh0 = _mm(dfl, w_fl_t, tm=512, tn=512, tk=LANE, name="d_h_logit")
    dh, parts = _mm_scatter(dproj, w_main_t, dh0, chip, tm=1024, tn=1024, tk=_WMAIN // 4, name="d_h_grad_scatter")
    grad_x, dng = _rms_bwd(x, norm_g, dh, dx2, tm=tm, name="rms_bwd")

    red = [_sum4(p, own, me, name=f"grad_sum_chips_{t}") for t, (p, own) in enumerate(zip(parts, chip))]
    other = _sibling_swap(red, name="grad_swap_result")
    full = [jnp.where(ci == 0, jnp.concatenate([r, o], axis=0), jnp.concatenate([o, r], axis=0))
            for r, o in zip(red, other)]

    small = dict(norm_g=dng, mem_norm_g=dmg, final_g=dfg, b_f=dbf[0:1, :], conv_w=dcw,
                 conv_b=dsm[0:1], conv_ln_g=dsm[1:2], conv_ln_b=dsm[2:3])
    return sq[0, 0], grad_x, full, small


_SMALL_ORDER = ("norm_g", "mem_norm_g", "final_g", "b_f", "conv_w", "conv_b", "conv_ln_g", "conv_ln_b")


def _pack_small(small):
    parts, layout = [], []
    row = 0
    for k in _SMALL_ORDER:
        p = _rows_pad8(small[k].reshape(-1, LANE))
        layout.append((k, row, small[k].shape))
        parts.append(p)
        row += p.shape[0]
    return jnp.concatenate(parts, axis=0), layout


def _unpack_small(packed, layout):
    out = {}
    for k, row, shape in layout:
        nrow = (shape[0] * shape[1]) // LANE
        out[k] = packed[row:row + nrow].reshape(shape)
    return out


def kernel(x, mem, norm_g, mem_norm_g, w_in, b_f, conv_w, conv_b, conv_ln_g, conv_ln_b, w_conv_pw, w_mem_kv, w_out, final_g, loss_target, m_norm_g, m_mem_norm_g, m_w_in, m_b_f, m_conv_w, m_conv_b, m_conv_ln_g, m_conv_ln_b, m_w_conv_pw, m_w_mem_kv, m_w_out, m_final_g, v_norm_g, v_mem_norm_g, v_w_in, v_b_f, v_conv_w, v_conv_b, v_conv_ln_g, v_conv_ln_b, v_w_conv_pw, v_w_mem_kv, v_w_out, v_final_g):
    S, D = x.shape[1], x.shape[2]
    xi, yi, ci = lax.axis_index("x"), lax.axis_index("y"), lax.axis_index("c")
    chip = 2 * xi + yi

    halves = lambda a: a.reshape(2, a.shape[0] // 2, a.shape[1])
    cw_shard = jnp.pad(conv_w[0], ((0, CONV_HALO - CONV_K), (0, 0)))
    sq, grad_x, (g_w_in, g_w_pw, g_w_mkv, g_w_out), small = _step(
        x[0], mem[0], loss_target[0], norm_g, mem_norm_g, final_g, b_f, conv_b, conv_ln_g, conv_ln_b,
        halves(w_in[0].astype(BF16)), halves(w_conv_pw[0].astype(BF16)), halves(w_mem_kv[0].astype(BF16)),
        halves(w_out[0].astype(BF16)), halves(cw_shard), ci, chip)

    loss = lax.psum(sq, ("x", "y", "c")) * (0.5 / D)

    packed, layout = _pack_small(small)
    sm = _unpack_small(_allreduce_small(packed, name="small_all_reduce"), layout)
    cshard = CONV_W // N_CHIPS
    g_conv_w = lax.dynamic_slice_in_dim(sm["conv_w"][:CONV_K], chip * cshard, cshard, axis=1)

    grads = dict(
        norm_g=sm["norm_g"], mem_norm_g=sm["mem_norm_g"], w_in=g_w_in[None], b_f=sm["b_f"][:, :N_FOX],
        conv_w=g_conv_w[None], conv_b=sm["conv_b"], conv_ln_g=sm["conv_ln_g"], conv_ln_b=sm["conv_ln_b"],
        w_conv_pw=g_w_pw[None], w_mem_kv=g_w_mkv[None], w_out=g_w_out[None], final_g=sm["final_g"].reshape(D))
    weights = dict(norm_g=norm_g, mem_norm_g=mem_norm_g, w_in=w_in, b_f=b_f, conv_w=conv_w, conv_b=conv_b,
                   conv_ln_g=conv_ln_g, conv_ln_b=conv_ln_b, w_conv_pw=w_conv_pw, w_mem_kv=w_mem_kv, w_out=w_out,
                   final_g=final_g)
    ms = dict(norm_g=m_norm_g, mem_norm_g=m_mem_norm_g, w_in=m_w_in, b_f=m_b_f, conv_w=m_conv_w, conv_b=m_conv_b,
              conv_ln_g=m_conv_ln_g, conv_ln_b=m_conv_ln_b, w_conv_pw=m_w_conv_pw, w_mem_kv=m_w_mem_kv,
              w_out=m_w_out, final_g=m_final_g)
    vs = dict(norm_g=v_norm_g, mem_norm_g=v_mem_norm_g, w_in=v_w_in, b_f=v_b_f, conv_w=v_conv_w, conv_b=v_conv_b,
              conv_ln_g=v_conv_ln_g, conv_ln_b=v_conv_ln_b, w_conv_pw=v_w_conv_pw, w_mem_kv=v_w_mem_kv,
              w_out=v_w_out, final_g=v_final_g)

    names = ("norm_g", "mem_norm_g", "w_in", "b_f", "conv_w", "conv_b", "conv_ln_g", "conv_ln_b", "w_conv_pw",
             "w_mem_kv", "w_out", "final_g")
    deltas, new_m, new_v = {}, {}, {}
    for k in names:
        shape = weights[k].shape
        if k == "w_in":
            two_d = lambda a: a.reshape(shape[-2], shape[-1]).T
            back = lambda a: a.T.reshape(shape)
        else:
            two_d = lambda a: a.reshape(-1, shape[-1])
            back = lambda a: a.reshape(shape)
        d, nm, nv = _adamw(two_d(weights[k]), two_d(grads[k]), two_d(ms[k]), two_d(vs[k]), name=f"adamw_{k}")
        deltas[k], new_m[k], new_v[k] = back(d), back(nm), back(nv)

    return (loss, grad_x[None], *[grads[k] for k in names], *[deltas[k] for k in names],
            *[new_m[k] for k in names], *[new_v[k] for k in names])
```

```python
import functools

import jax
import jax.numpy as jnp
from jax import lax
from jax.experimental import pallas as pl
from jax.experimental.pallas import tpu as pltpu

F32 = jnp.float32
BF16 = jnp.bfloat16
MESH = pl.DeviceIdType.MESH

HEAD = 128
N_FOX = 8
N_MEMH = 4
CONV_W = 512
FOX_W = N_FOX * HEAD
MEM_W = N_MEMH * HEAD
MIX_W = CONV_W + FOX_W + MEM_W
CONV_K = 31
CONV_HALO = 32
EPS = 1e-6
NEG = -1e30
LANE = 128
N_CHIPS = 4
N_DEV = 8
VMEM_LIMIT = 56 * 1024 * 1024

ADAM_LR = 0.001
ADAM_B1 = 0.9
ADAM_B2 = 0.999
ADAM_EPS = 1e-08
ADAM_WD = 0.01
ADAM_STEP = 10

NT = (((1,), (1,)), ((), ()))
TN = (((0,), (0,)), ((), ()))
NN = (((1,), (0,)), ((), ()))


def _params(sem=None):
    kw = dict(vmem_limit_bytes=VMEM_LIMIT)
    if sem is not None:
        kw["dimension_semantics"] = sem
    return pltpu.CompilerParams(**kw)


def _sigmoid(v):
    return jax.nn.sigmoid(v)


def _dsilu(v, s):
    return s * (1.0 + v * (1.0 - s))


def _rms_fwd(x, g, *, tm, name, gather=None):
    R, D = x.shape
    steps = R // tm

    def body(x_ref, g_ref, h_ref, *ht_ref):
        xv = x_ref[...]
        r = lax.rsqrt(jnp.mean(xv * xv, axis=-1, keepdims=True) + EPS)
        h = xv * r * g_ref[...]
        h_ref[...] = h.astype(BF16)
        if ht_ref:
            ht_ref[0][...] = jnp.transpose(h).astype(BF16)

    in_specs = [pl.BlockSpec((tm, D), lambda i: (i, 0)), pl.BlockSpec((1, D), lambda i: (0, 0))]
    out_spec = pl.BlockSpec((tm, D), lambda i: (i, 0))
    out_shape = jax.ShapeDtypeStruct((R, D), BF16)
    if gather is None:
        return pl.pallas_call(
            body, name=name, grid=(steps,), in_specs=in_specs, out_specs=out_spec, out_shape=out_shape,
            compiler_params=_params(("parallel",)),
        )(x, g)
    res = _call_with_gather(
        body, gather, lambda: pl.program_id(0) == 0, lambda: pl.program_id(0) == steps - 1,
        n_in=2, n_out=2, in_specs=in_specs, out_specs=[out_spec, pl.BlockSpec((D, tm), lambda i: (0, i))],
        out_shape=[out_shape, jax.ShapeDtypeStruct((D, R), BF16)], scratch_shapes=[],
        name=name, grid=(steps,), compiler_params=_params(("arbitrary",)),
    )(x, g, *gather)
    return res[0], res[1], res[2:]


def _rms_bwd(x, g, dh, dres, *, tm, name):
    R, D = x.shape
    has_res = dres is not None

    def body(*refs):
        if has_res:
            x_ref, g_ref, dh_ref, dres_ref, dx_ref, dg_ref = refs
        else:
            x_ref, g_ref, dh_ref, dx_ref, dg_ref = refs
        i = pl.program_id(0)

        @pl.when(i == 0)
        def _():
            dg_ref[...] = jnp.zeros_like(dg_ref)

        xv = x_ref[...]
        r = lax.rsqrt(jnp.mean(xv * xv, axis=-1, keepdims=True) + EPS)
        n = xv * r
        dh = dh_ref[...]
        dg_ref[...] += jnp.sum(dh * n, axis=0, keepdims=True)
        dn = dh * g_ref[...]
        dx = r * (dn - n * jnp.mean(dn * n, axis=-1, keepdims=True))
        if has_res:
            dx = dx + dres_ref[...]
        dx_ref[...] = dx

    row = pl.BlockSpec((tm, D), lambda i: (i, 0))
    vec = pl.BlockSpec((1, D), lambda i: (0, 0))
    ins = [row, vec, row] + ([row] if has_res else [])
    args = (x, g, dh) + ((dres,) if has_res else ())
    return pl.pallas_call(
        body, name=name, grid=(R // tm,),
        in_specs=ins, out_specs=[row, vec],
        out_shape=[jax.ShapeDtypeStruct((R, D), F32), jax.ShapeDtypeStruct((1, D), F32)],
        compiler_params=_params(("arbitrary",)),
    )(*args)


def _mm(a, b, *, ta=False, tb=False, out_dtype=F32, add=None, tm, tn, tk, n=None, b_off=0, name, gather=None):
    M, K = (a.shape[1], a.shape[0]) if ta else a.shape
    nb = b.shape[0] if tb else b.shape[1]
    n = nb if n is None else n
    tm, tn, tk = min(tm, M), min(tn, n), min(tk, K)
    assert M % tm == 0 and n % tn == 0 and K % tk == 0, (name, M, n, K, tm, tn, tk)
    nk = K // tk
    has_add = add is not None

    def body(*refs):
        if has_add:
            a_ref, b_ref, add_ref, o_ref, acc_ref = refs
        else:
            a_ref, b_ref, o_ref, acc_ref = refs
        k = pl.program_id(2)

        @pl.when(k == 0)
        def _():
            acc_ref[...] = jnp.zeros_like(acc_ref)

        av = a_ref[...].astype(BF16)
        bv = b_ref[...].astype(BF16)
        dims = (((0 if ta else 1,), (1 if tb else 0,)), ((), ()))
        acc_ref[...] += lax.dot_general(av, bv, dims, preferred_element_type=F32)

        @pl.when(k == nk - 1)
        def _():
            r = acc_ref[...]
            if has_add:
                r = r + add_ref[...]
            o_ref[...] = r.astype(out_dtype)

    a_spec = pl.BlockSpec((tk, tm), lambda i, j, k: (k, i)) if ta else pl.BlockSpec((tm, tk), lambda i, j, k: (i, k))
    b_spec = (pl.BlockSpec((tn, tk), lambda i, j, k: (j + b_off, k)) if tb
              else pl.BlockSpec((tk, tn), lambda i, j, k: (k, j + b_off)))
    o_spec = pl.BlockSpec((tm, tn), lambda i, j, k: (i, j))
    ins = [a_spec, b_spec] + ([o_spec] if has_add else [])
    args = (a, b) + ((add,) if has_add else ())
    gi, gj = M // tm, n // tn
    out_shape = jax.ShapeDtypeStruct((M, n), out_dtype)
    if gather is None:
        return pl.pallas_call(
            body, name=name, grid=(gi, gj, nk),
            in_specs=ins, out_specs=o_spec, out_shape=out_shape,
            scratch_shapes=[pltpu.VMEM((tm, tn), F32)],
            compiler_params=_params(("parallel", "parallel", "arbitrary")),
        )(*args)
    at = lambda i, j, k: (pl.program_id(0) == i) & (pl.program_id(1) == j) & (pl.program_id(2) == k)
    res = _call_with_gather(
        body, gather, lambda: at(0, 0, 0), lambda: at(gi - 1, gj - 1, nk - 1),
        n_in=len(ins), n_out=1, in_specs=ins, out_specs=[o_spec], out_shape=[out_shape],
        scratch_shapes=[pltpu.VMEM((tm, tn), F32)],
        name=name, grid=(gi, gj, nk), compiler_params=_params(("arbitrary", "arbitrary", "arbitrary")),
    )(*args, *gather)
    return res[0], res[1:]


SUB = 8


def _shifted_copies(es_ref, rows):
    e = es_ref[0]
    for b in range(1, SUB):
        es_ref[b] = pltpu.roll(e, shift=rows - b, axis=0)


def _window(es_ref, offset, tt):
    b = offset % SUB
    return es_ref[b, offset - b:offset - b + tt, :]


def _conv_taps(w_ref, es_ref, offsets, tt):
    acc = w_ref[0:1, :] * _window(es_ref, offsets[0], tt)
    for k in range(1, CONV_K):
        acc = acc + w_ref[k:k + 1, :] * _window(es_ref, offsets[k], tt)
    return acc


def _conv_fwd(pc, cw, cb, lng, lnb, wpw, *, tt, name):
    S = pc.shape[0]
    C = CONV_W
    lead = CONV_HALO - (CONV_K - 1)
    rows = tt + CONV_HALO

    def body(a_ref, b_ref, gc_ref, cw_ref, cb_ref, lng_ref, lnb_ref, wpw_ref, y_ref, u1_ref, es_ref):
        i = pl.program_id(0)

        @pl.when(i == 0)
        def _():
            es_ref[0, 0:CONV_HALO, :] = jnp.zeros((CONV_HALO, C), F32)

        @pl.when(i > 0)
        def _():
            es_ref[0, 0:CONV_HALO, :] = es_ref[0, tt:tt + CONV_HALO, :]

        es_ref[0, CONV_HALO:CONV_HALO + tt, :] = a_ref[...] * _sigmoid(b_ref[...])
        _shifted_copies(es_ref, rows)
        u1 = _conv_taps(cw_ref, es_ref, [lead + k for k in range(CONV_K)], tt) + cb_ref[...]
        u1_ref[...] = u1
        mu = jnp.mean(u1, axis=-1, keepdims=True)
        xc = u1 - mu
        rstd = lax.rsqrt(jnp.mean(xc * xc, axis=-1, keepdims=True) + EPS)
        u2 = xc * rstd * lng_ref[...] + lnb_ref[...]
        u3 = u2 * _sigmoid(u2)
        z = jnp.dot(u3.astype(BF16), wpw_ref[...], preferred_element_type=F32)
        gc = gc_ref[...]
        y_ref[...] = (z * gc * _sigmoid(gc)).astype(BF16)

    col = lambda c: pl.BlockSpec((tt, C), lambda i, c=c: (i, c))
    vec = pl.BlockSpec((1, C), lambda i: (0, 0))
    return pl.pallas_call(
        body, name=name, grid=(S // tt,),
        in_specs=[col(0), col(1), col(2), pl.BlockSpec((CONV_HALO, C), lambda i: (0, 0)), vec, vec, vec,
                  pl.BlockSpec((C, C), lambda i: (0, 0))],
        out_specs=[pl.BlockSpec((tt, C), lambda i: (i, 0))] * 2,
        out_shape=[jax.ShapeDtypeStruct((S, MIX_W), BF16), jax.ShapeDtypeStruct((S, C), F32)],
        scratch_shapes=[pltpu.VMEM((SUB, rows, C), F32)],
        compiler_params=_params(("arbitrary",)),
    )(pc, pc, pc, cw, cb, lng, lnb, wpw)


def _conv_bwd(pc, u1s, dy, cw, lng, lnb, wpw, *, tt, name):
    S = pc.shape[0]
    C = CONV_W
    nt = S // tt
    hb = tt // CONV_HALO
    lead = CONV_HALO - (CONV_K - 1)
    rows = tt + CONV_HALO

    def body(a_ref, b_ref, gc_ref, ah_ref, bh_ref, u1_ref, dy_ref, cw_ref, lng_ref, lnb_ref, wpw_ref,
             dpc_ref, dwpw_ref, dcw_ref, dsm_ref, eu_ref, ed_ref, dcw8_ref):
        i = pl.program_id(0)
        ti = nt - 1 - i

        @pl.when(i == 0)
        def _():
            ed_ref[0, tt:tt + CONV_HALO, :] = jnp.zeros((CONV_HALO, C), F32)
            dwpw_ref[...] = jnp.zeros_like(dwpw_ref)
            dcw8_ref[...] = jnp.zeros_like(dcw8_ref)
            dsm_ref[...] = jnp.zeros_like(dsm_ref)

        @pl.when(i > 0)
        def _():
            ed_ref[0, tt:tt + CONV_HALO, :] = ed_ref[0, 0:CONV_HALO, :]

        keep = jnp.where(ti > 0, 1.0, 0.0).astype(F32)
        eu_ref[0, 0:CONV_HALO, :] = ah_ref[...] * _sigmoid(bh_ref[...]) * keep
        a = a_ref[...]
        sb = _sigmoid(b_ref[...])
        eu_ref[0, CONV_HALO:CONV_HALO + tt, :] = a * sb
        _shifted_copies(eu_ref, rows)

        u1 = u1_ref[...]
        mu = jnp.mean(u1, axis=-1, keepdims=True)
        xc = u1 - mu
        rstd = lax.rsqrt(jnp.mean(xc * xc, axis=-1, keepdims=True) + EPS)
        nhat = xc * rstd
        g = lng_ref[...]
        u2 = nhat * g + lnb_ref[...]
        s2 = _sigmoid(u2)
        u3 = (u2 * s2).astype(BF16)
        z = jnp.dot(u3, wpw_ref[...], preferred_element_type=F32)

        gc = gc_ref[...]
        sg = _sigmoid(gc)
        dyv = dy_ref[...]
        dz = (dyv * gc * sg).astype(BF16)
        dpc_ref[:, 2 * C:3 * C] = (dyv * z * _dsilu(gc, sg)).astype(BF16)

        du3 = lax.dot_general(dz, wpw_ref[...], NT, preferred_element_type=F32)
        dwpw_ref[...] += lax.dot_general(u3, dz, TN, preferred_element_type=F32)
        du2 = du3 * _dsilu(u2, s2)
        dsm_ref[1:2, :] += jnp.sum(du2 * nhat, axis=0, keepdims=True)
        dsm_ref[2:3, :] += jnp.sum(du2, axis=0, keepdims=True)
        dn = du2 * g
        du1 = rstd * (dn - jnp.mean(dn, axis=-1, keepdims=True)
                      - nhat * jnp.mean(dn * nhat, axis=-1, keepdims=True))
        dsm_ref[0:1, :] += jnp.sum(du1, axis=0, keepdims=True)
        ed_ref[0, 0:tt, :] = du1
        _shifted_copies(ed_ref, rows)

        du0 = _conv_taps(cw_ref, ed_ref, [CONV_K - 1 - k for k in range(CONV_K)], tt)
        for k in range(CONV_K):
            prod = du1 * _window(eu_ref, lead + k, tt)
            part = prod[0:SUB]
            for r in range(1, tt // SUB):
                part = part + prod[r * SUB:(r + 1) * SUB]
            dcw8_ref[k * SUB:(k + 1) * SUB, :] += part

        dpc_ref[:, 0:C] = (du0 * sb).astype(BF16)
        dpc_ref[:, C:2 * C] = (du0 * a * sb * (1.0 - sb)).astype(BF16)

        @pl.when(i == nt - 1)
        def _():
            dcw_ref[...] = jnp.zeros_like(dcw_ref)
            for k in range(CONV_K):
                dcw_ref[k:k + 1, :] = jnp.sum(dcw8_ref[k * SUB:(k + 1) * SUB, :], axis=0, keepdims=True)

    col = lambda c: pl.BlockSpec((tt, C), lambda i, c=c: (nt - 1 - i, c))
    halo = lambda c: pl.BlockSpec((CONV_HALO, C), lambda i, c=c: (jnp.maximum((nt - 1 - i) * hb - 1, 0), c))
    vec = pl.BlockSpec((1, C), lambda i: (0, 0))
    fixed = lambda r: pl.BlockSpec((r, C), lambda i: (0, 0))
    return pl.pallas_call(
        body, name=name, grid=(nt,),
        in_specs=[col(0), col(1), col(2), halo(0), halo(1),
                  pl.BlockSpec((tt, C), lambda i: (nt - 1 - i, 0)),
                  pl.BlockSpec((tt, C), lambda i: (nt - 1 - i, 0)),
                  fixed(CONV_HALO), vec, vec, fixed(C)],
        out_specs=[pl.BlockSpec((tt, 3 * C), lambda i: (nt - 1 - i, 0)), fixed(C), fixed(CONV_HALO), fixed(8)],
        out_shape=[jax.ShapeDtypeStruct((S, 3 * C), BF16), jax.ShapeDtypeStruct((C, C), F32),
                   jax.ShapeDtypeStruct((CONV_HALO, C), F32), jax.ShapeDtypeStruct((8, C), F32)],
        scratch_shapes=[pltpu.VMEM((SUB, rows, C), F32), pltpu.VMEM((SUB, rows, C), F32),
                        pltpu.VMEM((CONV_HALO * SUB, C), F32)],
        compiler_params=_params(("arbitrary",)),
    )(pc, pc, pc, pc, pc, u1s, dy, cw, lng, lnb, wpw)


def _tri(n, lower):
    r = lax.broadcasted_iota(jnp.int32, (n, n), 0)
    c = lax.broadcasted_iota(jnp.int32, (n, n), 1)
    return jnp.where((r >= c) if lower else (r <= c), 1.0, 0.0).astype(F32)


def _fox_cumsum(pfl, bf, *, tc, name):
    S = pfl.shape[0]

    def body(fl_ref, bf_ref, qa_ref, ka_ref, carry_ref):
        i = pl.program_id(0)

        @pl.when(i == 0)
        def _():
            carry_ref[...] = jnp.zeros_like(carry_ref)

        z = fl_ref[...] + bf_ref[...]
        logf = jnp.minimum(z, 0.0) - jnp.log1p(jnp.exp(-jnp.abs(z)))
        c = jnp.dot(_tri(tc, True), logf, precision=lax.Precision.HIGHEST,
                    preferred_element_type=F32) + carry_ref[0:1, :]
        carry_ref[0:1, :] = c[tc - 1:tc, :]

        cs = c * (HEAD ** 0.5)
        hi = cs.astype(BF16).astype(F32)
        r1 = cs - hi
        mid = r1.astype(BF16).astype(F32)
        lo = r1 - mid
        lane = lax.broadcasted_iota(jnp.int32, (tc, LANE), 1)
        is_hi = (lane == 0) | (lane == 3)
        is_mid = (lane == 1) | (lane == 4)
        for h in range(N_FOX):
            col = lambda v: jnp.sum(jnp.where(lane == h, v, 0.0), axis=1, keepdims=True)
            pieces = jnp.where(is_hi, col(hi), jnp.where(is_mid, col(mid), col(lo)))
            qa_ref[h] = jnp.where(lane < 3, pieces, jnp.where(lane < 6, 1.0, 0.0)).astype(BF16)
            ka_ref[h] = jnp.where(lane < 3, 1.0, jnp.where(lane < 6, -pieces, 0.0)).astype(BF16)

    out = pl.BlockSpec((N_FOX, tc, HEAD), lambda i: (0, i, 0))
    return pl.pallas_call(
        body, name=name, grid=(S // tc,),
        in_specs=[pl.BlockSpec((tc, LANE), lambda i: (i, 0)), pl.BlockSpec((1, LANE), lambda i: (0, 0))],
        out_specs=[out, out],
        out_shape=[jax.ShapeDtypeStruct((N_FOX, S, HEAD), BF16)] * 2,
        scratch_shapes=[pltpu.VMEM((8, LANE), F32)],
        compiler_params=_params(("arbitrary",)),
    )(pfl, bf)


def _fox_dlogit(dc, pfl, bf, *, tc, name):
    S = pfl.shape[0]
    nt = S // tc

    def body(dc_ref, fl_ref, bf_ref, dfl_ref, dbf_ref, carry_ref):
        i = pl.program_id(0)

        @pl.when(i == 0)
        def _():
            carry_ref[...] = jnp.zeros_like(carry_ref)
            dbf_ref[...] = jnp.zeros_like(dbf_ref)

        dlogf = jnp.dot(_tri(tc, False), dc_ref[...], precision=lax.Precision.HIGHEST,
                        preferred_element_type=F32) + carry_ref[0:1, :]
        carry_ref[0:1, :] = dlogf[0:1, :]
        dz = dlogf * _sigmoid(-(fl_ref[...] + bf_ref[...]))
        dfl_ref[...] = dz.astype(BF16)
        dbf_ref[0:1, :] += jnp.sum(dz, axis=0, keepdims=True)

    rev = pl.BlockSpec((tc, LANE), lambda i: (nt - 1 - i, 0))
    return pl.pallas_call(
        body, name=name, grid=(nt,),
        in_specs=[rev, rev, pl.BlockSpec((1, LANE), lambda i: (0, 0))],
        out_specs=[rev, pl.BlockSpec((8, LANE), lambda i: (0, 0))],
        out_shape=[jax.ShapeDtypeStruct((S, LANE), BF16), jax.ShapeDtypeStruct((8, LANE), F32)],
        scratch_shapes=[pltpu.VMEM((8, LANE), F32)],
        compiler_params=_params(("arbitrary",)),
    )(dc, pfl, bf)


def _as_row(col):
    return jnp.transpose(jnp.broadcast_to(col, (col.shape[0], LANE)))[0:1, :]


def _causal_part(rows, cols, row0, col0, q_is_row=True):
    r = lax.broadcasted_iota(jnp.int32, (rows, cols), 0) + row0
    c = lax.broadcasted_iota(jnp.int32, (rows, cols), 1) + col0
    return (r >= c) if q_is_row else (c >= r)


LOG2E = 1.4426950408889634
FOX_SCALE2 = (HEAD ** -0.5) * LOG2E


def _fox_fwd(qkv, pg, qa, ka, y_all, *, tq, nsub, name):
    S = qkv.shape[0]
    nq = S // tq
    rs = tq // nsub
    ybase = CONV_W // HEAD

    def body(q_ref, qa_ref, k_ref, ka_ref, v_ref, g_ref, yin_ref, y_ref, o_ref, m_ref, linv_ref):
        i = pl.program_id(1)
        q = jnp.concatenate([q_ref[...], qa_ref[...]], axis=1)
        qs = [q[r * rs:(r + 1) * rs] for r in range(nsub)]
        ones = jnp.where(lax.broadcasted_iota(jnp.int32, (tq, HEAD), 1) == 0, 1.0, 0.0).astype(BF16)

        def blk(j, carry, masked):
            off = pl.multiple_of(j * tq, tq)
            kj = jnp.concatenate([k_ref[pl.ds(off, tq), :], ka_ref[pl.ds(off, tq), :]], axis=1)
            vj = jnp.concatenate([v_ref[pl.ds(off, tq), :], ones], axis=1)
            out = []
            for r in range(nsub):
                m, acc = carry[r]
                nk = (r + 1) * rs if masked else tq
                s = lax.dot_general(qs[r], kj[:nk], NT, preferred_element_type=F32) * FOX_SCALE2
                if masked:
                    s = jnp.where(_causal_part(rs, nk, r * rs, 0), s, NEG)
                m_new = jnp.maximum(m, jnp.ceil(jnp.max(s, axis=-1, keepdims=True)))
                pb = jnp.exp2(s - m_new).astype(BF16)
                acc = jnp.exp2(m - m_new) * acc + jnp.dot(pb, vj[:nk], preferred_element_type=F32)
                out.append((m_new, acc))
            return tuple(out)

        init = tuple((jnp.full((rs, 1), NEG, F32), jnp.zeros((rs, 2 * HEAD), F32)) for _ in range(nsub))
        carry = lax.fori_loop(0, i, lambda j, c: blk(j, c, False), init)
        carry = blk(i, carry, True)
        m = jnp.concatenate([c[0] for c in carry], axis=0)
        acc = jnp.concatenate([c[1] for c in carry], axis=0)
        linv = 1.0 / acc[:, HEAD:HEAD + 1]
        o = acc[:, :HEAD] * linv
        g = g_ref[...]
        y_ref[...] = (o * g * _sigmoid(g)).astype(BF16)
        o_ref[...] = o
        m_ref[...] = _as_row(m)
        linv_ref[...] = _as_row(linv)

    tile = lambda base: pl.BlockSpec((tq, HEAD), lambda h, i, base=base: (i, base + h))
    full = lambda base: pl.BlockSpec((S, HEAD), lambda h, i, base=base: (0, base + h))
    rowv = pl.BlockSpec((None, None, 1, tq), lambda h, i: (h, i, 0, 0))
    stat = jax.ShapeDtypeStruct((N_FOX, nq, 1, tq), F32)
    return pl.pallas_call(
        body, name=name, grid=(N_FOX, nq),
        in_specs=[tile(0), pl.BlockSpec((None, tq, HEAD), lambda h, i: (h, i, 0)),
                  full(N_FOX), pl.BlockSpec((None, S, HEAD), lambda h, i: (h, 0, 0)),
                  full(2 * N_FOX), tile(0), ANY],
        out_specs=[tile(ybase), tile(0), rowv, rowv],
        out_shape=[jax.ShapeDtypeStruct(y_all.shape, BF16), jax.ShapeDtypeStruct((S, FOX_W), F32), stat, stat],
        input_output_aliases={6: 0},
        compiler_params=_params(("parallel", "arbitrary")),
    )(qkv, qa, qkv, ka, qkv, pg, y_all)


def _fox_bwd_prep(dy, o, pg, *, tq, name):
    S = o.shape[0]
    base = CONV_W // HEAD

    def body(dy_ref, o_ref, g_ref, do_ref, dg_ref, dl_ref):
        g = g_ref[...]
        sg = _sigmoid(g)
        dyv = dy_ref[...]
        ov = o_ref[...]
        dob = (dyv * g * sg).astype(BF16)
        do_ref[...] = dob
        dg_ref[...] = (dyv * ov * _dsilu(g, sg)).astype(BF16)
        dl_ref[...] = _as_row(jnp.sum(dob.astype(F32) * ov, axis=-1, keepdims=True))

    tile = lambda b: pl.BlockSpec((tq, HEAD), lambda h, i, b=b: (i, b + h))
    return pl.pallas_call(
        body, name=name, grid=(N_FOX, S // tq),
        in_specs=[tile(base), tile(0), tile(0)],
        out_specs=[tile(0), tile(0), pl.BlockSpec((None, None, 1, tq), lambda h, i: (h, i, 0, 0))],
        out_shape=[jax.ShapeDtypeStruct((S, FOX_W), BF16), jax.ShapeDtypeStruct((S, FOX_W), BF16),
                   jax.ShapeDtypeStruct((N_FOX, S // tq, 1, tq), F32)],
        compiler_params=_params(("parallel", "parallel")),
    )(dy, o, pg)


def _fox_bwd(qkv, do, qa, ka, m_row, linv_row, delta_row, *, tq, nsub, name):
    S = qkv.shape[0]
    nq = S // tq
    cs = tq // nsub
    scale = HEAD ** -0.5

    def body(k_ref, ka_ref, v_ref, q_ref, qa_ref, do_ref, mr_ref, lir_ref, dlr_ref, dq_ref, dk_ref, dv_ref, dc_ref):
        j = pl.program_id(1)

        @pl.when(j == 0)
        def _():
            dq_ref[...] = jnp.zeros_like(dq_ref)

        kj = k_ref[...]
        kja = jnp.concatenate([kj, ka_ref[...]], axis=1)
        vj = v_ref[...]

        def blk(i, carry, masked):
            dk, dv, dc = carry
            m_i = mr_ref[i]
            linv_i = lir_ref[i]
            dl_i = dlr_ref[i]
            for c in range(nsub):
                off = pl.multiple_of(i * tq + c * cs, cs)
                cols = slice(c * cs, (c + 1) * cs)
                qi = q_ref[pl.ds(off, cs), :]
                qia = jnp.concatenate([qi, qa_ref[pl.ds(off, cs), :]], axis=1)
                doi = do_ref[pl.ds(off, cs), :]
                nk = (c + 1) * cs if masked else tq
                st = lax.dot_general(kja[:nk], qia, NT, preferred_element_type=F32) * FOX_SCALE2
                if masked:
                    st = jnp.where(_causal_part(nk, cs, 0, c * cs, q_is_row=False), st, NEG)
                grow = lambda u, nk=nk: u if nk == tq else jnp.concatenate(
                    [u, jnp.zeros((tq - nk, u.shape[1]), F32)], axis=0)
                pt = jnp.exp2(st - m_i[:, cols]).astype(BF16).astype(F32) * linv_i[:, cols]
                dv = dv + grow(jnp.dot(pt.astype(BF16), doi, preferred_element_type=F32))
                dpt = lax.dot_general(vj[:nk], doi, NT, preferred_element_type=F32)
                dst = pt * (dpt - dl_i[:, cols])
                dsb = dst.astype(BF16)
                dk = dk + grow(jnp.dot(dsb, qi, preferred_element_type=F32))
                dq_ref[pl.ds(off, cs), :] += lax.dot_general(dsb, kj[:nk], TN, preferred_element_type=F32) * scale
                dc = dc - grow(jnp.sum(dst, axis=-1, keepdims=True))
            return dk, dv, dc

        init = (jnp.zeros((tq, HEAD), F32), jnp.zeros((tq, HEAD), F32), jnp.zeros((tq, 1), F32))
        carry = blk(j, init, True)
        dk, dv, dc = lax.fori_loop(j + 1, nq, lambda i, c: blk(i, c, False), carry)
        dk_ref[...] = (dk * scale).astype(BF16)
        dv_ref[...] = dv.astype(BF16)
        dc_ref[...] = _as_row(dc)

    tile = lambda base: pl.BlockSpec((tq, HEAD), lambda h, j, base=base: (j, base + h))
    full = lambda base: pl.BlockSpec((S, HEAD), lambda h, j, base=base: (0, base + h))
    atile = pl.BlockSpec((None, tq, HEAD), lambda h, j: (h, j, 0))
    afull = pl.BlockSpec((None, S, HEAD), lambda h, j: (h, 0, 0))
    rowt = pl.BlockSpec((None, None, 1, tq), lambda h, j: (h, j, 0, 0))
    rowv = pl.BlockSpec((None, nq, 1, tq), lambda h, j: (h, 0, 0, 0))
    return pl.pallas_call(
        body, name=name, grid=(N_FOX, nq),
        in_specs=[tile(N_FOX), atile, tile(2 * N_FOX), full(0), afull, full(0), rowv, rowv, rowv],
        out_specs=[full(0), tile(0), tile(0), rowt],
        out_shape=[jax.ShapeDtypeStruct((S, FOX_W), F32), jax.ShapeDtypeStruct((S, FOX_W), BF16),
                   jax.ShapeDtypeStruct((S, FOX_W), BF16), jax.ShapeDtypeStruct((N_FOX, nq, 1, tq), F32)],
        compiler_params=_params(("arbitrary", "arbitrary")),
    )(qkv, ka, qkv, qkv, qa, do, m_row, linv_row, delta_row)


def _mem_heads(mq, mkv, h):
    lo = h * HEAD
    qh = mq[:, lo:lo + HEAD].astype(BF16)
    kh = mkv[:, lo:lo + HEAD].astype(BF16)
    vh = mkv[:, MEM_W + lo:MEM_W + lo + HEAD].astype(BF16)
    return qh, kh, vh


def _mem_softmax(qh, kh):
    s = lax.dot_general(qh, kh, NT, preferred_element_type=F32) * (HEAD ** -0.5)
    e = jnp.exp(s - jnp.max(s, axis=-1, keepdims=True))
    return e / jnp.sum(e, axis=-1, keepdims=True)


def _mem_fwd(pg, mkv, y_all, *, tq, name):
    S = pg.shape[0]
    M = mkv.shape[0]
    qb = FOX_W // MEM_W
    yb = (CONV_W + FOX_W) // MEM_W

    def body(mq_ref, g_ref, mkv_ref, yin_ref, y_ref):
        mq = mq_ref[...]
        mkvv = mkv_ref[...]
        for h in range(N_MEMH):
            qh, kh, vh = _mem_heads(mq, mkvv, h)
            p = _mem_softmax(qh, kh)
            o = jnp.dot(p.astype(BF16), vh, preferred_element_type=F32)
            g = g_ref[:, h * HEAD:(h + 1) * HEAD]
            y_ref[:, h * HEAD:(h + 1) * HEAD] = (o * g * _sigmoid(g)).astype(BF16)

    return pl.pallas_call(
        body, name=name, grid=(S // tq,),
        in_specs=[pl.BlockSpec((tq, MEM_W), lambda i: (i, qb)), pl.BlockSpec((tq, MEM_W), lambda i: (i, qb + 1)),
                  pl.BlockSpec((M, 2 * MEM_W), lambda i: (0, 0)), ANY],
        out_specs=pl.BlockSpec((tq, MEM_W), lambda i: (i, yb)),
        out_shape=jax.ShapeDtypeStruct(y_all.shape, BF16),
        input_output_aliases={3: 0},
        compiler_params=_params(("parallel",)),
    )(pg, pg, mkv, y_all)


def _mem_bwd(pg, mkv, dy, *, tq, name):
    S = pg.shape[0]
    M = mkv.shape[0]
    qb = FOX_W // MEM_W
    yb = (CONV_W + FOX_W) // MEM_W
    scale = HEAD ** -0.5

    def body(mq_ref, g_ref, mkv_ref, dy_ref, dmq_ref, dg_ref, dmkv_ref):
        i = pl.program_id(0)

        @pl.when(i == 0)
        def _():
            dmkv_ref[...] = jnp.zeros_like(dmkv_ref)

        mq = mq_ref[...]
        mkvv = mkv_ref[...]
        for h in range(N_MEMH):
            lo = h * HEAD
            qh, kh, vh = _mem_heads(mq, mkvv, h)
            p = _mem_softmax(qh, kh)
            o = jnp.dot(p.astype(BF16), vh, preferred_element_type=F32)
            g = g_ref[:, lo:lo + HEAD]
            sg = _sigmoid(g)
            dyh = dy_ref[:, lo:lo + HEAD]
            do = dyh * g * sg
            dg_ref[:, lo:lo + HEAD] = (dyh * o * _dsilu(g, sg)).astype(BF16)
            dob = do.astype(BF16)
            dp = lax.dot_general(dob, vh, NT, preferred_element_type=F32)
            ds = p * (dp - jnp.sum(do * o, axis=-1, keepdims=True))
            dsb = ds.astype(BF16)
            dmq_ref[:, lo:lo + HEAD] = (jnp.dot(dsb, kh, preferred_element_type=F32) * scale).astype(BF16)
            dmkv_ref[:, lo:lo + HEAD] += lax.dot_general(dsb, qh, TN, preferred_element_type=F32) * scale
            dmkv_ref[:, MEM_W + lo:MEM_W + lo + HEAD] += lax.dot_general(
                p.astype(BF16), dob, TN, preferred_element_type=F32)

    return pl.pallas_call(
        body, name=name, grid=(S // tq,),
        in_specs=[pl.BlockSpec((tq, MEM_W), lambda i: (i, qb)), pl.BlockSpec((tq, MEM_W), lambda i: (i, qb + 1)),
                  pl.BlockSpec((M, 2 * MEM_W), lambda i: (0, 0)), pl.BlockSpec((tq, MEM_W), lambda i: (i, yb))],
        out_specs=[pl.BlockSpec((tq, MEM_W), lambda i: (i, 0)), pl.BlockSpec((tq, MEM_W), lambda i: (i, 0)),
                   pl.BlockSpec((M, 2 * MEM_W), lambda i: (0, 0))],
        out_shape=[jax.ShapeDtypeStruct((S, MEM_W), BF16), jax.ShapeDtypeStruct((S, MEM_W), BF16),
                   jax.ShapeDtypeStruct((M, 2 * MEM_W), F32)],
        compiler_params=_params(("arbitrary",)),
    )(pg, pg, mkv, dy)


def _final(x2, target, fg, *, tm, name):
    S, D = x2.shape

    def body(x_ref, t_ref, g_ref, dx_ref, dxb_ref, dg_ref, ls_ref):
        i = pl.program_id(0)

        @pl.when(i == 0)
        def _():
            dg_ref[...] = jnp.zeros_like(dg_ref)
            ls_ref[...] = jnp.zeros_like(ls_ref)

        xv = x_ref[...]
        r = lax.rsqrt(jnp.mean(xv * xv, axis=-1, keepdims=True) + EPS)
        n = xv * r
        g = g_ref[...]
        diff = n * g - t_ref[...]
        ls_ref[...] += jnp.sum(diff * diff)
        dout = diff * (1.0 / D)
        dg_ref[...] += jnp.sum(dout * n, axis=0, keepdims=True)
        dn = dout * g
        dx = r * (dn - n * jnp.mean(dn * n, axis=-1, keepdims=True))
        dx_ref[...] = dx
        dxb_ref[...] = dx.astype(BF16)

    row = pl.BlockSpec((tm, D), lambda i: (i, 0))
    vec = pl.BlockSpec((1, D), lambda i: (0, 0))
    return pl.pallas_call(
        body, name=name, grid=(S // tm,),
        in_specs=[row, row, vec],
        out_specs=[row, row, vec, pl.BlockSpec((8, LANE), lambda i: (0, 0))],
        out_shape=[jax.ShapeDtypeStruct((S, D), F32), jax.ShapeDtypeStruct((S, D), BF16),
                   jax.ShapeDtypeStruct((1, D), F32), jax.ShapeDtypeStruct((8, LANE), F32)],
        compiler_params=_params(("arbitrary",)),
    )(x2, target, fg)


def _adamw(w, g, m, v, *, name):
    R, C = w.shape
    tr, tc = R, C
    for cand in (256, 128, 64, 32, 16, 8):
        if R % cand == 0 and R > cand:
            tr = cand
            break
    if tr == R and R > 256 and C % 256 == 0:
        tc = 256
    c1 = 1.0 - ADAM_B1 ** ADAM_STEP
    c2 = 1.0 - ADAM_B2 ** ADAM_STEP

    def body(w_ref, g_ref, m_ref, v_ref, d_ref, nm_ref, nv_ref):
        gv = g_ref[...]
        nm = ADAM_B1 * m_ref[...] + (1.0 - ADAM_B1) * gv
        nv = ADAM_B2 * v_ref[...] + (1.0 - ADAM_B2) * (gv * gv)
        nm_ref[...] = nm
        nv_ref[...] = nv
        d_ref[...] = -ADAM_LR * ((nm / c1) / (jnp.sqrt(nv / c2) + ADAM_EPS) + ADAM_WD * w_ref[...])

    spec = pl.BlockSpec((tr, tc), lambda i, j: (i, j))
    shp = jax.ShapeDtypeStruct((R, C), F32)
    return pl.pallas_call(
        body, name=name, grid=(R // tr, C // tc),
        in_specs=[spec] * 4, out_specs=[spec] * 3, out_shape=[shp] * 3,
        compiler_params=_params(("parallel", "parallel")),
    )(w, g, m, v)


def _sum4(q, own, me, *, name):
    _, R, C = q.shape
    tr = R
    for cand in (256, 128, 64, 32, 16, 8):
        if R % cand == 0 and R > cand:
            tr = cand
            break

    def body(me_ref, own_ref, q1_ref, q2_ref, q3_ref, o_ref):
        f = lambda r: r[...].astype(F32)
        o_ref[...] = ((f(own_ref) + f(q1_ref)) + f(q2_ref)) + f(q3_ref)

    blk = lambda d: pl.BlockSpec((None, tr, C), lambda i, me_ref, d=d: (me_ref[0] ^ d, i, 0))
    return pl.pallas_call(
        body, name=name,
        grid_spec=pltpu.PrefetchScalarGridSpec(
            num_scalar_prefetch=1, grid=(R // tr,),
            in_specs=[blk(0), blk(1), blk(2), blk(3)],
            out_specs=pl.BlockSpec((tr, C), lambda i, me_ref: (i, 0))),
        out_shape=jax.ShapeDtypeStruct((R, C), F32),
        compiler_params=_params(("parallel",)),
    )(jnp.reshape(me, (1,)).astype(jnp.int32), own, q, q, q)


def _add_sibling(g, got, c, *, name):
    K, _, R, C = g.shape
    tr = R
    for cand in (256, 128, 64, 32, 16, 8):
        if R % cand == 0 and R > cand:
            tr = cand
            break

    def body(c_ref, a_ref, b_ref, o_ref):
        o_ref[...] = (a_ref[...] + b_ref[...]).astype(BF16)

    spec = pl.BlockSpec((None, tr, C), lambda k, i, c_ref: (k, i, 0))
    return pl.pallas_call(
        body, name=name,
        grid_spec=pltpu.PrefetchScalarGridSpec(
            num_scalar_prefetch=1, grid=(K, R // tr),
            in_specs=[pl.BlockSpec((None, None, tr, C), lambda k, i, c_ref: (k, c_ref[0], i, 0)), spec],
            out_specs=spec),
        out_shape=jax.ShapeDtypeStruct((K, R, C), BF16),
        compiler_params=_params(("parallel", "parallel")),
    )(jnp.reshape(c, (1,)).astype(jnp.int32), g, got)


ANY = pl.BlockSpec(memory_space=pl.ANY)


def _other_chips(x, y):
    return [(d, 1 - x if d & 2 else x, 1 - y if d & 1 else y) for d in (1, 2, 3)]


def _gather_ici_copies(src_refs, out_refs, ici_send, ici_recv):
    x, y, c = lax.axis_index("x"), lax.axis_index("y"), lax.axis_index("c")
    me = 2 * x + y
    return [pltpu.make_async_remote_copy(
        src_ref=src_refs[t].at[c], dst_ref=out_refs[t].at[me, c],
        send_sem=ici_send.at[3 * t + d - 1], recv_sem=ici_recv.at[3 * t + d - 1],
        device_id=(tx, ty, c), device_id_type=MESH)
        for t in range(len(src_refs)) for d, tx, ty in _other_chips(x, y)]


def _gather_finish(src_refs, out_refs, ici_send, ici_recv, d2d_send, d2d_recv):
    n = len(src_refs)
    x, y, c = lax.axis_index("x"), lax.axis_index("y"), lax.axis_index("c")
    passed = []
    for t in range(n):
        for d, tx, ty in _other_chips(x, y):
            landed = out_refs[t].at[2 * tx + ty, c]
            pltpu.make_async_remote_copy(
                src_ref=landed, dst_ref=landed,
                send_sem=ici_send.at[3 * t + d - 1], recv_sem=ici_recv.at[3 * t + d - 1],
                device_id=(tx, ty, c), device_id_type=MESH).wait_recv()
            cp = pltpu.make_async_remote_copy(
                src_ref=landed, dst_ref=landed,
                send_sem=d2d_send.at[3 * t + d - 1], recv_sem=d2d_recv.at[3 * t + d - 1],
                device_id=(x, y, 1 - c), device_id_type=MESH)
            cp.start()
            passed.append(cp)
    for t in range(n):
        for d, tx, ty in _other_chips(x, y):
            theirs = out_refs[t].at[2 * tx + ty, 1 - c]
            pltpu.make_async_remote_copy(
                src_ref=theirs, dst_ref=theirs,
                send_sem=d2d_send.at[3 * t + d - 1], recv_sem=d2d_recv.at[3 * t + d - 1],
                device_id=(x, y, 1 - c), device_id_type=MESH).wait_recv()
    for cp in _gather_ici_copies(src_refs, out_refs, ici_send, ici_recv) + passed:
        cp.wait_send()


def _call_with_gather(body, shards, first, last, *, n_in, n_out, in_specs, out_specs, out_shape, scratch_shapes,
                      **kw):
    n = len(shards)
    n_scr = len(scratch_shapes)

    def wrapped(*refs):
        ins, srcs = refs[:n_in], refs[n_in:n_in + n]
        outs = refs[n_in + n:n_in + n + n_out]
        gouts = refs[n_in + n + n_out:n_in + 2 * n + n_out]
        scr = refs[n_in + 2 * n + n_out:n_in + 2 * n + n_out + n_scr]
        ici_send, ici_recv, d2d_send, d2d_recv = refs[n_in + 2 * n + n_out + n_scr:]

        @pl.when(first())
        def _():
            for cp in _gather_ici_copies(srcs, gouts, ici_send, ici_recv):
                cp.start()

        body(*ins, *outs, *scr)

        @pl.when(last())
        def _():
            _gather_finish(srcs, gouts, ici_send, ici_recv, d2d_send, d2d_recv)

    sem = pltpu.SemaphoreType.DMA((3 * n,))
    return pl.pallas_call(
        wrapped,
        in_specs=list(in_specs) + [ANY] * n,
        out_specs=list(out_specs) + [ANY] * n,
        out_shape=list(out_shape) + [jax.ShapeDtypeStruct((N_CHIPS,) + s.shape, s.dtype) for s in shards],
        scratch_shapes=list(scratch_shapes) + [sem, sem, sem, sem],
        **kw)


def _scatter_copies(src_refs, out_refs, send_sems, recv_sems):
    n = len(src_refs)
    x, y, c = lax.axis_index("x"), lax.axis_index("y"), lax.axis_index("c")
    me = 2 * x + y
    copies = []
    for t in range(n):
        for d, tx, ty in _other_chips(x, y):
            copies.append(pltpu.make_async_remote_copy(
                src_ref=src_refs[t].at[2 * tx + ty], dst_ref=out_refs[t].at[me],
                send_sem=send_sems.at[3 * t + d - 1], recv_sem=recv_sems.at[3 * t + d - 1],
                device_id=(tx, ty, c), device_id_type=MESH))
    return copies


def _mm_scatter(a, b, a2, b2, srcs, *, tm, tn, tk, name):
    M, K = a.shape
    N = b.shape[1]
    K2 = a2.shape[1]
    tm, tn, tk = min(tm, M), min(tn, N), min(tk, K)
    assert M % tm == 0 and N % tn == 0 and K % tk == 0, (name, M, N, K, tm, tn, tk)
    gi, gj, gk = M // tm, N // tn, K // tk
    n = len(srcs)

    def body(*refs):
        a_ref, b_ref, a2_ref, b2_ref = refs[:4]
        src_refs = refs[4:4 + n]
        o_ref = refs[4 + n]
        out_refs = refs[5 + n:5 + 2 * n]
        acc_ref, send_sems, recv_sems = refs[5 + 2 * n:]
        i, j, k = pl.program_id(0), pl.program_id(1), pl.program_id(2)

        @pl.when((i == 0) & (j == 0) & (k == 0))
        def _():
            for cp in _scatter_copies(src_refs, out_refs, send_sems, recv_sems):
                cp.start()

        @pl.when(k == 0)
        def _():
            acc_ref[...] = jnp.dot(a2_ref[...].astype(BF16), b2_ref[...].astype(BF16), preferred_element_type=F32)

        acc_ref[...] += jnp.dot(a_ref[...].astype(BF16), b_ref[...].astype(BF16), preferred_element_type=F32)

        @pl.when(k == gk - 1)
        def _():
            o_ref[...] = acc_ref[...]

        @pl.when((i == gi - 1) & (j == gj - 1) & (k == gk - 1))
        def _():
            for cp in _scatter_copies(src_refs, out_refs, send_sems, recv_sems):
                cp.wait()

    o_spec = pl.BlockSpec((tm, tn), lambda i, j, k: (i, j))
    sem = pltpu.SemaphoreType.DMA((3 * n,))
    res = pl.pallas_call(
        body, name=name, grid=(gi, gj, gk),
        in_specs=[pl.BlockSpec((tm, tk), lambda i, j, k: (i, k)), pl.BlockSpec((tk, tn), lambda i, j, k: (k, j)),
                  pl.BlockSpec((tm, K2), lambda i, j, k: (i, 0)), pl.BlockSpec((K2, tn), lambda i, j, k: (0, j))]
        + [ANY] * n,
        out_specs=[o_spec] + [ANY] * n,
        out_shape=[jax.ShapeDtypeStruct((M, N), F32)] + [jax.ShapeDtypeStruct(s.shape, s.dtype) for s in srcs],
        scratch_shapes=[pltpu.VMEM((tm, tn), F32), sem, sem],
        compiler_params=_params(("arbitrary", "arbitrary", "arbitrary")),
    )(a, b, a2, b2, *srcs)
    return res[0], res[1:]


def _swap_halves(grads, *, name):
    n = len(grads)

    def body(*refs):
        src_refs, out_refs = refs[:n], refs[n:2 * n]
        send_sems, recv_sems = refs[2 * n:]
        x, y, c = lax.axis_index("x"), lax.axis_index("y"), lax.axis_index("c")
        copies = []
        for t in range(n):
            for k in range(N_CHIPS):
                cp = pltpu.make_async_remote_copy(
                    src_ref=src_refs[t].at[k, 1 - c], dst_ref=out_refs[t].at[k],
                    send_sem=send_sems.at[N_CHIPS * t + k], recv_sem=recv_sems.at[N_CHIPS * t + k],
                    device_id=(x, y, 1 - c), device_id_type=MESH)
                cp.start()
                copies.append(cp)
        for cp in copies:
            cp.wait()

    sem = pltpu.SemaphoreType.DMA((N_CHIPS * n,))
    return pl.pallas_call(
        body, name=name,
        in_specs=[ANY] * n, out_specs=[ANY] * n,
        out_shape=[jax.ShapeDtypeStruct((N_CHIPS,) + g.shape[2:], g.dtype) for g in grads],
        scratch_shapes=[sem, sem],
    )(*grads)


def _sibling_swap(srcs, *, name):
    n = len(srcs)

    def body(*refs):
        src_refs, out_refs = refs[:n], refs[n:2 * n]
        send_sems, recv_sems = refs[2 * n:]
        x, y, c = lax.axis_index("x"), lax.axis_index("y"), lax.axis_index("c")
        copies = []
        for t in range(n):
            cp = pltpu.make_async_remote_copy(
                src_ref=src_refs[t], dst_ref=out_refs[t],
                send_sem=send_sems.at[t], recv_sem=recv_sems.at[t],
                device_id=(x, y, 1 - c), device_id_type=MESH)
            cp.start()
            copies.append(cp)
        for cp in copies:
            cp.wait()

    sem = pltpu.SemaphoreType.DMA((n,))
    return pl.pallas_call(
        body, name=name,
        in_specs=[ANY] * n, out_specs=[ANY] * n,
        out_shape=[jax.ShapeDtypeStruct(s.shape, s.dtype) for s in srcs],
        scratch_shapes=[sem, sem],
    )(*srcs)


def _allreduce_small(v, *, name):
    R, C = v.shape

    def body(v_ref, o_ref, buf_ref, send_sems, recv_sems):
        x, y, c = lax.axis_index("x"), lax.axis_index("y"), lax.axis_index("c")
        me = 4 * x + 2 * y + c
        buf_ref[me] = v_ref[...]
        copies = []
        for d in range(1, N_DEV):
            tx = 1 - x if d & 4 else x
            ty = 1 - y if d & 2 else y
            tc = 1 - c if d & 1 else c
            cp = pltpu.make_async_remote_copy(
                src_ref=v_ref, dst_ref=buf_ref.at[me],
                send_sem=send_sems.at[d - 1], recv_sem=recv_sems.at[d - 1],
                device_id=(tx, ty, tc), device_id_type=MESH)
            cp.start()
            copies.append(cp)
        for cp in copies:
            cp.wait()
        acc = buf_ref[0]
        for k in range(1, N_DEV):
            acc = acc + buf_ref[k]
        o_ref[...] = acc

    return pl.pallas_call(
        body, name=name,
        in_specs=[pl.BlockSpec(memory_space=pltpu.VMEM)],
        out_specs=pl.BlockSpec(memory_space=pltpu.VMEM),
        out_shape=jax.ShapeDtypeStruct((R, C), F32),
        scratch_shapes=[pltpu.VMEM((N_DEV, R, C), F32), pltpu.SemaphoreType.DMA((N_DEV - 1,)),
                        pltpu.SemaphoreType.DMA((N_DEV - 1,))],
    )(v)


_A0, _B0, _GC0 = 0, CONV_W, 2 * CONV_W
_Q0 = 3 * CONV_W
_FL0 = _Q0 + 3 * FOX_W
_FG0 = _FL0 + N_FOX
_MQ0 = _FG0 + FOX_W
_MG0 = _MQ0 + MEM_W
_DIN = _MG0 + MEM_W
_WMAIN = _DIN - N_FOX
_PC_N = 3 * CONV_W
_QKV_N = 3 * FOX_W
_PG_N = FOX_W + 2 * MEM_W


def _rows_pad8(a):
    r = a.shape[0]
    p = (-r) % 8
    return jnp.pad(a, ((0, p), (0, 0))) if p else a


def _step(x, mem, target, norm_g, mem_norm_g, final_g, b_f, conv_b, ln_g, ln_b,
          s_in, s_pw, s_mkv, s_out, s_cw, ci, me):
    S, D = x.shape
    M = mem.shape[0]
    tq = min(512, S)
    tf = min(1024, S)
    tt = min(256, S)
    tm = min(256, S)
    nin = s_in.shape[2]
    bf_pad = jnp.pad(b_f, ((0, 0), (0, LANE - N_FOX)))
    whole = lambda g, s: lax.dynamic_update_slice(g, s[None], (me, 0, 0, 0))

    h, h_t, (g_in,) = _rms_fwd(x, norm_g, tm=tm, name="rms_fwd_gather_w_in", gather=[s_in])
    w_in = jnp.transpose(whole(g_in, s_in).reshape(N_CHIPS, D, nin), (1, 0, 2)).reshape(D, N_CHIPS * nin)
    w_main = jnp.concatenate([w_in[:, :_FL0], w_in[:, _FG0:]], axis=1)
    w_fl = jnp.pad(w_in[:, _FL0:_FG0], ((0, 0), (0, LANE - N_FOX)))
    w_main_t, w_fl_t = w_main.T, w_fl.T

    pc, (g_pw, g_mkv, g_out, g_cw) = _mm(h, w_main, tm=1024, tn=512, tk=2048, n=_PC_N, b_off=0,
                                         name="proj_conv_gather_rest", gather=[s_pw, s_mkv, s_out, s_cw])
    w_pw = whole(g_pw, s_pw).reshape(CONV_W, CONV_W)
    w_mkv = whole(g_mkv, s_mkv).reshape(D, 2 * MEM_W)
    w_out = whole(g_out, s_out).reshape(MIX_W, D)
    w_out_t = w_out.T
    cw_pad = jnp.transpose(whole(g_cw, s_cw).reshape(N_CHIPS, CONV_HALO, CONV_W // N_CHIPS),
                           (1, 0, 2)).reshape(CONV_HALO, CONV_W)

    qkv = _mm(h, w_main, out_dtype=BF16, tm=1024, tn=512, tk=2048, n=_QKV_N, b_off=_PC_N // 512, name="proj_qkv")
    pg = _mm(h, w_main, tm=1024, tn=512, tk=2048, n=_PG_N, b_off=(_PC_N + _QKV_N) // 512, name="proj_gate")
    pfl = _mm(h, w_fl, tm=1024, tn=LANE, tk=2048, name="proj_logit")

    y, u1s = _conv_fwd(pc, cw_pad, conv_b, ln_g, ln_b, w_pw, tt=tt, name="conv_fwd")

    qa, ka = _fox_cumsum(pfl, bf_pad, tc=tt, name="fox_cumsum")
    y, o_fox, fox_m, fox_linv = _fox_fwd(qkv, pg, qa, ka, y, tq=tf, nsub=2, name="fox_fwd")

    hm = _rms_fwd(mem, mem_norm_g, tm=min(256, M), name="rms_mem")
    mkv = _mm(hm, w_mkv, tm=256, tn=512, tk=2048, name="mem_kv")
    y = _mem_fwd(pg, mkv, y, tq=tq, name="mem_fwd")

    x2 = _mm(y, w_out, add=x, tm=1024, tn=512, tk=2048, name="out_proj")
    dx2, dx2_b, dfg, sq = _final(x2, target, final_g.reshape(1, D), tm=tm, name="final")

    dy = _mm(dx2_b, w_out_t, tm=1024, tn=512, tk=2048, name="d_y")
    dw_out = _mm(y.T, dx2_b, tm=1024, tn=512, tk=2048, name="d_w_out")

    dpc, dw_pw, dcw, dsm = _conv_bwd(pc, u1s, dy, cw_pad, ln_g, ln_b, w_pw, tt=tt, name="conv_bwd")

    do, dfgate, delta = _fox_bwd_prep(dy, o_fox, pg, tq=tf, name="fox_bwd_prep")
    dq, dk, dv, dc = _fox_bwd(qkv, do, qa, ka, fox_m, fox_linv, delta, tq=tf, nsub=2, name="fox_bwd")
    dc_pad = jnp.pad(dc.reshape(N_FOX, S).T, ((0, 0), (0, LANE - N_FOX)))
    dfl, dbf = _fox_dlogit(dc_pad, pfl, bf_pad, tc=tt, name="fox_dlogit")

    dmq, dmgate, dmkv = _mem_bwd(pg, mkv, dy, tq=tq, name="mem_bwd")
    dw_mkv = _mm(hm, dmkv, ta=True, tm=512, tn=512, tk=256, name="d_w_mkv")
    dhm = _mm(dmkv, w_mkv, tb=True, tm=256, tn=512, tk=1024, name="d_hm")
    _, dmg = _rms_bwd(mem, mem_norm_g, dhm, None, tm=min(256, M), name="rms_mem_bwd")

    dproj = jnp.concatenate([dpc, dq.astype(BF16), dk, dv, dfgate, dmq, dmgate], axis=1)
    dw_main = _mm(h_t, dproj, tm=1024, tn=_WMAIN // 4, tk=1024, name="d_w_main")
    dw_fl = _mm(h_t, dfl, tm=1024, tn=LANE, tk=1024, name="d_w_logit")

    def chip_columns(k):
        lo, hi = k * nin, (k + 1) * nin
        parts = []
        if lo < _FL0:
            parts.append(dw_main[:, lo:min(hi, _FL0)])
        if lo < _FG0 and hi > _FL0:
            parts.append(dw_fl[:, max(lo, _FL0) - _FL0:min(hi, _FG0) - _FL0])
        if hi > _FG0:
            parts.append(dw_main[:, max(lo, _FG0) - N_FOX:hi - N_FOX])
        return parts[0] if len(parts) == 1 else jnp.concatenate(parts, axis=1)

    big = [jnp.stack([chip_columns(k) for k in range(N_CHIPS)]),
           dw_pw.reshape(N_CHIPS, CONV_W // N_CHIPS, CONV_W),
           dw_mkv.reshape(N_CHIPS, D // N_CHIPS, 2 * MEM_W),
           dw_out.reshape(N_CHIPS, MIX_W // N_CHIPS, D)]
    big = [g.reshape(N_CHIPS, 2, g.shape[1] // 2, g.shape[2]) for g in big]
    got = _swap_halves(big, name="grad_swap_halves")
    chip = [_add_sibling(g, o, ci, name=f"grad_add_sibling_{t}") for t, (g, o) in enumerate(zip(big, got))]

    dh, parts = _mm_scatter(dproj, w_main_t, dfl, w_fl_t, chip, tm=1024, tn=1024, tk=_WMAIN // 4,
                            name="d_h_grad_scatter")
    grad_x, dng = _rms_bwd(x, norm_g, dh, dx2, tm=tm, name="rms_bwd")

    red = [_sum4(p, own, me, name=f"grad_sum_chips_{t}") for t, (p, own) in enumerate(zip(parts, chip))]
    other = _sibling_swap(red, name="grad_swap_result")
    full = [jnp.where(ci == 0, jnp.concatenate([r, o], axis=0), jnp.concatenate([o, r], axis=0))
            for r, o in zip(red, other)]

    small = dict(norm_g=dng, mem_norm_g=dmg, final_g=dfg, b_f=dbf[0:1, :], conv_w=dcw,
                 conv_b=dsm[0:1], conv_ln_g=dsm[1:2], conv_ln_b=dsm[2:3])
    return sq[0, 0], grad_x, full, small


_SMALL_ORDER = ("norm_g", "mem_norm_g", "final_g", "b_f", "conv_w", "conv_b", "conv_ln_g", "conv_ln_b")


def _pack_small(small):
    parts, layout = [], []
    row = 0
    for k in _SMALL_ORDER:
        p = _rows_pad8(small[k].reshape(-1, LANE))
        layout.append((k, row, small[k].shape))
        parts.append(p)
        row += p.shape[0]
    return jnp.concatenate(parts, axis=0), layout


def _unpack_small(packed, layout):
    out = {}
    for k, row, shape in layout:
        nrow = (shape[0] * shape[1]) // LANE
        out[k] = packed[row:row + nrow].reshape(shape)
    return out


def kernel(x, mem, norm_g, mem_norm_g, w_in, b_f, conv_w, conv_b, conv_ln_g, conv_ln_b, w_conv_pw, w_mem_kv, w_out, final_g, loss_target, m_norm_g, m_mem_norm_g, m_w_in, m_b_f, m_conv_w, m_conv_b, m_conv_ln_g, m_conv_ln_b, m_w_conv_pw, m_w_mem_kv, m_w_out, m_final_g, v_norm_g, v_mem_norm_g, v_w_in, v_b_f, v_conv_w, v_conv_b, v_conv_ln_g, v_conv_ln_b, v_w_conv_pw, v_w_mem_kv, v_w_out, v_final_g):
    S, D = x.shape[1], x.shape[2]
    xi, yi, ci = lax.axis_index("x"), lax.axis_index("y"), lax.axis_index("c")
    chip = 2 * xi + yi

    halves = lambda a: a.reshape(2, a.shape[0] // 2, a.shape[1])
    cw_shard = jnp.pad(conv_w[0], ((0, CONV_HALO - CONV_K), (0, 0)))
    sq, grad_x, (g_w_in, g_w_pw, g_w_mkv, g_w_out), small = _step(
        x[0], mem[0], loss_target[0], norm_g, mem_norm_g, final_g, b_f, conv_b, conv_ln_g, conv_ln_b,
        halves(w_in[0].astype(BF16)), halves(w_conv_pw[0].astype(BF16)), halves(w_mem_kv[0].astype(BF16)),
        halves(w_out[0].astype(BF16)), halves(cw_shard), ci, chip)

    loss = lax.psum(sq, ("x", "y", "c")) * (0.5 / D)

    packed, layout = _pack_small(small)
    sm = _unpack_small(_allreduce_small(packed, name="small_all_reduce"), layout)
    cshard = CONV_W // N_CHIPS
    g_conv_w = lax.dynamic_slice_in_dim(sm["conv_w"][:CONV_K], chip * cshard, cshard, axis=1)

    grads = dict(
        norm_g=sm["norm_g"], mem_norm_g=sm["mem_norm_g"], w_in=g_w_in[None], b_f=sm["b_f"][:, :N_FOX],
        conv_w=g_conv_w[None], conv_b=sm["conv_b"], conv_ln_g=sm["conv_ln_g"], conv_ln_b=sm["conv_ln_b"],
        w_conv_pw=g_w_pw[None], w_mem_kv=g_w_mkv[None], w_out=g_w_out[None], final_g=sm["final_g"].reshape(D))
    weights = dict(norm_g=norm_g, mem_norm_g=mem_norm_g, w_in=w_in, b_f=b_f, conv_w=conv_w, conv_b=conv_b,
                   conv_ln_g=conv_ln_g, conv_ln_b=conv_ln_b, w_conv_pw=w_conv_pw, w_mem_kv=w_mem_kv, w_out=w_out,
                   final_g=final_g)
    ms = dict(norm_g=m_norm_g, mem_norm_g=m_mem_norm_g, w_in=m_w_in, b_f=m_b_f, conv_w=m_conv_w, conv_b=m_conv_b,
              conv_ln_g=m_conv_ln_g, conv_ln_b=m_conv_ln_b, w_conv_pw=m_w_conv_pw, w_mem_kv=m_w_mem_kv,
              w_out=m_w_out, final_g=m_final_g)
    vs = dict(norm_g=v_norm_g, mem_norm_g=v_mem_norm_g, w_in=v_w_in, b_f=v_b_f, conv_w=v_conv_w, conv_b=v_conv_b,
              conv_ln_g=v_conv_ln_g, conv_ln_b=v_conv_ln_b, w_conv_pw=v_w_conv_pw, w_mem_kv=v_w_mem_kv,
              w_out=v_w_out, final_g=v_final_g)

    names = ("norm_g", "mem_norm_g", "w_in", "b_f", "conv_w", "conv_b", "conv_ln_g", "conv_ln_b", "w_conv_pw",
             "w_mem_kv", "w_out", "final_g")
    deltas, new_m, new_v = {}, {}, {}
    for k in names:
        shape = weights[k].shape
        if k == "w_in":
            two_d = lambda a: a.reshape(shape[-2], shape[-1]).T
            back = lambda a: a.T.reshape(shape)
        else:
            two_d = lambda a: a.reshape(-1, shape[-1])
            back = lambda a: a.reshape(shape)
        d, nm, nv = _adamw(two_d(weights[k]), two_d(grads[k]), two_d(ms[k]), two_d(vs[k]), name=f"adamw_{k}")
        deltas[k], new_m[k], new_v[k] = back(d), back(nm), back(nv)

    return (loss, grad_x[None], *[grads[k] for k in names], *[deltas[k] for k in names],
            *[new_m[k] for k in names], *[new_v[k] for k in names])
```

```python
import functools

import jax
import jax.numpy as jnp
from jax import lax
from jax.experimental import pallas as pl
from jax.experimental.pallas import tpu as pltpu

F32 = jnp.float32
BF16 = jnp.bfloat16
MESH = pl.DeviceIdType.MESH

HEAD = 128
N_FOX = 8
N_MEMH = 4
CONV_W = 512
FOX_W = N_FOX * HEAD
MEM_W = N_MEMH * HEAD
MIX_W = CONV_W + FOX_W + MEM_W
CONV_K = 31
CONV_HALO = 32
EPS = 1e-6
NEG = -1e30
LANE = 128
N_CHIPS = 4
N_DEV = 8
VMEM_LIMIT = 56 * 1024 * 1024

ADAM_LR = 0.001
ADAM_B1 = 0.9
ADAM_B2 = 0.999
ADAM_EPS = 1e-08
ADAM_WD = 0.01
ADAM_STEP = 10

NT = (((1,), (1,)), ((), ()))
TN = (((0,), (0,)), ((), ()))
NN = (((1,), (0,)), ((), ()))


def _params(sem=None):
    kw = dict(vmem_limit_bytes=VMEM_LIMIT)
    if sem is not None:
        kw["dimension_semantics"] = sem
    return pltpu.CompilerParams(**kw)


def _sigmoid(v):
    return jax.nn.sigmoid(v)


def _dsilu(v, s):
    return s * (1.0 + v * (1.0 - s))


def _rms_fwd(x, g, *, tm, name, gather=None):
    R, D = x.shape
    steps = R // tm

    def body(x_ref, g_ref, h_ref, *ht_ref):
        xv = x_ref[...]
        r = lax.rsqrt(jnp.mean(xv * xv, axis=-1, keepdims=True) + EPS)
        h = xv * r * g_ref[...]
        h_ref[...] = h.astype(BF16)
        if ht_ref:
            ht_ref[0][...] = jnp.transpose(h).astype(BF16)

    in_specs = [pl.BlockSpec((tm, D), lambda i: (i, 0)), pl.BlockSpec((1, D), lambda i: (0, 0))]
    out_spec = pl.BlockSpec((tm, D), lambda i: (i, 0))
    out_shape = jax.ShapeDtypeStruct((R, D), BF16)
    if gather is None:
        return pl.pallas_call(
            body, name=name, grid=(steps,), in_specs=in_specs, out_specs=out_spec, out_shape=out_shape,
            compiler_params=_params(("parallel",)),
        )(x, g)
    res = _call_with_gather(
        body, gather, lambda: pl.program_id(0) == 0, lambda: pl.program_id(0) == steps - 1,
        n_in=2, n_out=2, in_specs=in_specs, out_specs=[out_spec, pl.BlockSpec((D, tm), lambda i: (0, i))],
        out_shape=[out_shape, jax.ShapeDtypeStruct((D, R), BF16)], scratch_shapes=[],
        name=name, grid=(steps,), compiler_params=_params(("arbitrary",)),
    )(x, g, *gather)
    return res[0], res[1], res[2:]


def _rms_bwd(x, g, dh, dres, *, tm, name):
    R, D = x.shape
    has_res = dres is not None

    def body(*refs):
        if has_res:
            x_ref, g_ref, dh_ref, dres_ref, dx_ref, dg_ref = refs
        else:
            x_ref, g_ref, dh_ref, dx_ref, dg_ref = refs
        i = pl.program_id(0)

        @pl.when(i == 0)
        def _():
            dg_ref[...] = jnp.zeros_like(dg_ref)

        xv = x_ref[...]
        r = lax.rsqrt(jnp.mean(xv * xv, axis=-1, keepdims=True) + EPS)
        n = xv * r
        dh = dh_ref[...]
        dg_ref[...] += jnp.sum(dh * n, axis=0, keepdims=True)
        dn = dh * g_ref[...]
        dx = r * (dn - n * jnp.mean(dn * n, axis=-1, keepdims=True))
        if has_res:
            dx = dx + dres_ref[...]
        dx_ref[...] = dx

    row = pl.BlockSpec((tm, D), lambda i: (i, 0))
    vec = pl.BlockSpec((1, D), lambda i: (0, 0))
    ins = [row, vec, row] + ([row] if has_res else [])
    args = (x, g, dh) + ((dres,) if has_res else ())
    return pl.pallas_call(
        body, name=name, grid=(R // tm,),
        in_specs=ins, out_specs=[row, vec],
        out_shape=[jax.ShapeDtypeStruct((R, D), F32), jax.ShapeDtypeStruct((1, D), F32)],
        compiler_params=_params(("arbitrary",)),
    )(*args)


def _mm(a, b, *, ta=False, tb=False, out_dtype=F32, add=None, tm, tn, tk, n=None, b_off=0, name, gather=None):
    M, K = (a.shape[1], a.shape[0]) if ta else a.shape
    nb = b.shape[0] if tb else b.shape[1]
    n = nb if n is None else n
    tm, tn, tk = min(tm, M), min(tn, n), min(tk, K)
    assert M % tm == 0 and n % tn == 0 and K % tk == 0, (name, M, n, K, tm, tn, tk)
    nk = K // tk
    has_add = add is not None

    def body(*refs):
        if has_add:
            a_ref, b_ref, add_ref, o_ref, acc_ref = refs
        else:
            a_ref, b_ref, o_ref, acc_ref = refs
        k = pl.program_id(2)

        @pl.when(k == 0)
        def _():
            acc_ref[...] = jnp.zeros_like(acc_ref)

        av = a_ref[...].astype(BF16)
        bv = b_ref[...].astype(BF16)
        dims = (((0 if ta else 1,), (1 if tb else 0,)), ((), ()))
        acc_ref[...] += lax.dot_general(av, bv, dims, preferred_element_type=F32)

        @pl.when(k == nk - 1)
        def _():
            r = acc_ref[...]
            if has_add:
                r = r + add_ref[...]
            o_ref[...] = r.astype(out_dtype)

    a_spec = pl.BlockSpec((tk, tm), lambda i, j, k: (k, i)) if ta else pl.BlockSpec((tm, tk), lambda i, j, k: (i, k))
    b_spec = (pl.BlockSpec((tn, tk), lambda i, j, k: (j + b_off, k)) if tb
              else pl.BlockSpec((tk, tn), lambda i, j, k: (k, j + b_off)))
    o_spec = pl.BlockSpec((tm, tn), lambda i, j, k: (i, j))
    ins = [a_spec, b_spec] + ([o_spec] if has_add else [])
    args = (a, b) + ((add,) if has_add else ())
    gi, gj = M // tm, n // tn
    out_shape = jax.ShapeDtypeStruct((M, n), out_dtype)
    if gather is None:
        return pl.pallas_call(
            body, name=name, grid=(gi, gj, nk),
            in_specs=ins, out_specs=o_spec, out_shape=out_shape,
            scratch_shapes=[pltpu.VMEM((tm, tn), F32)],
            compiler_params=_params(("parallel", "parallel", "arbitrary")),
        )(*args)
    at = lambda i, j, k: (pl.program_id(0) == i) & (pl.program_id(1) == j) & (pl.program_id(2) == k)
    res = _call_with_gather(
        body, gather, lambda: at(0, 0, 0), lambda: at(gi - 1, gj - 1, nk - 1),
        n_in=len(ins), n_out=1, in_specs=ins, out_specs=[o_spec], out_shape=[out_shape],
        scratch_shapes=[pltpu.VMEM((tm, tn), F32)],
        name=name, grid=(gi, gj, nk), compiler_params=_params(("arbitrary", "arbitrary", "arbitrary")),
    )(*args, *gather)
    return res[0], res[1:]


SUB = 8


def _shifted_copies(es_ref, rows):
    e = es_ref[0]
    for b in range(1, SUB):
        es_ref[b] = pltpu.roll(e, shift=rows - b, axis=0)


def _window(es_ref, offset, tt):
    b = offset % SUB
    return es_ref[b, offset - b:offset - b + tt, :]


def _conv_taps(w_ref, es_ref, offsets, tt):
    acc = w_ref[0:1, :] * _window(es_ref, offsets[0], tt)
    for k in range(1, CONV_K):
        acc = acc + w_ref[k:k + 1, :] * _window(es_ref, offsets[k], tt)
    return acc


def _conv_fwd(pc, cw, cb, lng, lnb, wpw, *, tt, name):
    S = pc.shape[0]
    C = CONV_W
    lead = CONV_HALO - (CONV_K - 1)
    rows = tt + CONV_HALO

    def body(a_ref, b_ref, gc_ref, cw_ref, cb_ref, lng_ref, lnb_ref, wpw_ref, y_ref, yt_ref, u1_ref, es_ref):
        i = pl.program_id(0)

        @pl.when(i == 0)
        def _():
            es_ref[0, 0:CONV_HALO, :] = jnp.zeros((CONV_HALO, C), F32)

        @pl.when(i > 0)
        def _():
            es_ref[0, 0:CONV_HALO, :] = es_ref[0, tt:tt + CONV_HALO, :]

        es_ref[0, CONV_HALO:CONV_HALO + tt, :] = a_ref[...] * _sigmoid(b_ref[...])
        _shifted_copies(es_ref, rows)
        u1 = _conv_taps(cw_ref, es_ref, [lead + k for k in range(CONV_K)], tt) + cb_ref[...]
        u1_ref[...] = u1
        mu = jnp.mean(u1, axis=-1, keepdims=True)
        xc = u1 - mu
        rstd = lax.rsqrt(jnp.mean(xc * xc, axis=-1, keepdims=True) + EPS)
        u2 = xc * rstd * lng_ref[...] + lnb_ref[...]
        u3 = u2 * _sigmoid(u2)
        z = jnp.dot(u3.astype(BF16), wpw_ref[...], preferred_element_type=F32)
        gc = gc_ref[...]
        yv = z * gc * _sigmoid(gc)
        y_ref[...] = yv.astype(BF16)
        yt_ref[...] = jnp.transpose(yv).astype(BF16)

    col = lambda c: pl.BlockSpec((tt, C), lambda i, c=c: (i, c))
    vec = pl.BlockSpec((1, C), lambda i: (0, 0))
    return pl.pallas_call(
        body, name=name, grid=(S // tt,),
        in_specs=[col(0), col(1), col(2), pl.BlockSpec((CONV_HALO, C), lambda i: (0, 0)), vec, vec, vec,
                  pl.BlockSpec((C, C), lambda i: (0, 0))],
        out_specs=[pl.BlockSpec((tt, C), lambda i: (i, 0)), pl.BlockSpec((C, tt), lambda i: (0, i)),
                   pl.BlockSpec((tt, C), lambda i: (i, 0))],
        out_shape=[jax.ShapeDtypeStruct((S, MIX_W), BF16), jax.ShapeDtypeStruct((MIX_W, S), BF16),
                   jax.ShapeDtypeStruct((S, C), F32)],
        scratch_shapes=[pltpu.VMEM((SUB, rows, C), F32)],
        compiler_params=_params(("arbitrary",)),
    )(pc, pc, pc, cw, cb, lng, lnb, wpw)


def _conv_bwd(pc, u1s, dy, cw, lng, lnb, wpw, *, tt, name):
    S = pc.shape[0]
    C = CONV_W
    nt = S // tt
    hb = tt // CONV_HALO
    lead = CONV_HALO - (CONV_K - 1)
    rows = tt + CONV_HALO

    def body(a_ref, b_ref, gc_ref, ah_ref, bh_ref, u1_ref, dy_ref, cw_ref, lng_ref, lnb_ref, wpw_ref,
             dpc_ref, dwpw_ref, dcw_ref, dsm_ref, eu_ref, ed_ref, dcw8_ref):
        i = pl.program_id(0)
        ti = nt - 1 - i

        @pl.when(i == 0)
        def _():
            ed_ref[0, tt:tt + CONV_HALO, :] = jnp.zeros((CONV_HALO, C), F32)
            dwpw_ref[...] = jnp.zeros_like(dwpw_ref)
            dcw8_ref[...] = jnp.zeros_like(dcw8_ref)
            dsm_ref[...] = jnp.zeros_like(dsm_ref)

        @pl.when(i > 0)
        def _():
            ed_ref[0, tt:tt + CONV_HALO, :] = ed_ref[0, 0:CONV_HALO, :]

        keep = jnp.where(ti > 0, 1.0, 0.0).astype(F32)
        eu_ref[0, 0:CONV_HALO, :] = ah_ref[...] * _sigmoid(bh_ref[...]) * keep
        a = a_ref[...]
        sb = _sigmoid(b_ref[...])
        eu_ref[0, CONV_HALO:CONV_HALO + tt, :] = a * sb
        _shifted_copies(eu_ref, rows)

        u1 = u1_ref[...]
        mu = jnp.mean(u1, axis=-1, keepdims=True)
        xc = u1 - mu
        rstd = lax.rsqrt(jnp.mean(xc * xc, axis=-1, keepdims=True) + EPS)
        nhat = xc * rstd
        g = lng_ref[...]
        u2 = nhat * g + lnb_ref[...]
        s2 = _sigmoid(u2)
        u3 = (u2 * s2).astype(BF16)
        z = jnp.dot(u3, wpw_ref[...], preferred_element_type=F32)

        gc = gc_ref[...]
        sg = _sigmoid(gc)
        dyv = dy_ref[...]
        dz = (dyv * gc * sg).astype(BF16)
        dpc_ref[:, 2 * C:3 * C] = (dyv * z * _dsilu(gc, sg)).astype(BF16)

        du3 = lax.dot_general(dz, wpw_ref[...], NT, preferred_element_type=F32)
        dwpw_ref[...] += lax.dot_general(u3, dz, TN, preferred_element_type=F32)
        du2 = du3 * _dsilu(u2, s2)
        dsm_ref[1:2, :] += jnp.sum(du2 * nhat, axis=0, keepdims=True)
        dsm_ref[2:3, :] += jnp.sum(du2, axis=0, keepdims=True)
        dn = du2 * g
        du1 = rstd * (dn - jnp.mean(dn, axis=-1, keepdims=True)
                      - nhat * jnp.mean(dn * nhat, axis=-1, keepdims=True))
        dsm_ref[0:1, :] += jnp.sum(du1, axis=0, keepdims=True)
        ed_ref[0, 0:tt, :] = du1
        _shifted_copies(ed_ref, rows)

        du0 = _conv_taps(cw_ref, ed_ref, [CONV_K - 1 - k for k in range(CONV_K)], tt)
        for k in range(CONV_K):
            prod = du1 * _window(eu_ref, lead + k, tt)
            part = prod[0:SUB]
            for r in range(1, tt // SUB):
                part = part + prod[r * SUB:(r + 1) * SUB]
            dcw8_ref[k * SUB:(k + 1) * SUB, :] += part

        dpc_ref[:, 0:C] = (du0 * sb).astype(BF16)
        dpc_ref[:, C:2 * C] = (du0 * a * sb * (1.0 - sb)).astype(BF16)

        @pl.when(i == nt - 1)
        def _():
            dcw_ref[...] = jnp.zeros_like(dcw_ref)
            for k in range(CONV_K):
                dcw_ref[k:k + 1, :] = jnp.sum(dcw8_ref[k * SUB:(k + 1) * SUB, :], axis=0, keepdims=True)

    col = lambda c: pl.BlockSpec((tt, C), lambda i, c=c: (nt - 1 - i, c))
    halo = lambda c: pl.BlockSpec((CONV_HALO, C), lambda i, c=c: (jnp.maximum((nt - 1 - i) * hb - 1, 0), c))
    vec = pl.BlockSpec((1, C), lambda i: (0, 0))
    fixed = lambda r: pl.BlockSpec((r, C), lambda i: (0, 0))
    return pl.pallas_call(
        body, name=name, grid=(nt,),
        in_specs=[col(0), col(1), col(2), halo(0), halo(1),
                  pl.BlockSpec((tt, C), lambda i: (nt - 1 - i, 0)),
                  pl.BlockSpec((tt, C), lambda i: (nt - 1 - i, 0)),
                  fixed(CONV_HALO), vec, vec, fixed(C)],
        out_specs=[pl.BlockSpec((tt, 3 * C), lambda i: (nt - 1 - i, 0)), fixed(C), fixed(CONV_HALO), fixed(8)],
        out_shape=[jax.ShapeDtypeStruct((S, 3 * C), BF16), jax.ShapeDtypeStruct((C, C), F32),
                   jax.ShapeDtypeStruct((CONV_HALO, C), F32), jax.ShapeDtypeStruct((8, C), F32)],
        scratch_shapes=[pltpu.VMEM((SUB, rows, C), F32), pltpu.VMEM((SUB, rows, C), F32),
                        pltpu.VMEM((CONV_HALO * SUB, C), F32)],
        compiler_params=_params(("arbitrary",)),
    )(pc, pc, pc, pc, pc, u1s, dy, cw, lng, lnb, wpw)


def _tri(n, lower):
    r = lax.broadcasted_iota(jnp.int32, (n, n), 0)
    c = lax.broadcasted_iota(jnp.int32, (n, n), 1)
    return jnp.where((r >= c) if lower else (r <= c), 1.0, 0.0).astype(F32)


def _fox_cumsum(pfl, bf, *, tc, name):
    S = pfl.shape[0]

    def body(fl_ref, bf_ref, qa_ref, ka_ref, carry_ref):
        i = pl.program_id(0)

        @pl.when(i == 0)
        def _():
            carry_ref[...] = jnp.zeros_like(carry_ref)

        z = fl_ref[...] + bf_ref[...]
        logf = jnp.minimum(z, 0.0) - jnp.log1p(jnp.exp(-jnp.abs(z)))
        c = jnp.dot(_tri(tc, True), logf, precision=lax.Precision.HIGHEST,
                    preferred_element_type=F32) + carry_ref[0:1, :]
        carry_ref[0:1, :] = c[tc - 1:tc, :]

        cs = c * (HEAD ** 0.5)
        hi = cs.astype(BF16).astype(F32)
        r1 = cs - hi
        mid = r1.astype(BF16).astype(F32)
        lo = r1 - mid
        lane = lax.broadcasted_iota(jnp.int32, (tc, LANE), 1)
        is_hi = (lane == 0) | (lane == 3)
        is_mid = (lane == 1) | (lane == 4)
        for h in range(N_FOX):
            col = lambda v: jnp.sum(jnp.where(lane == h, v, 0.0), axis=1, keepdims=True)
            pieces = jnp.where(is_hi, col(hi), jnp.where(is_mid, col(mid), col(lo)))
            qa_ref[h] = jnp.where(lane < 3, pieces, jnp.where(lane < 6, 1.0, 0.0)).astype(BF16)
            ka_ref[h] = jnp.where(lane < 3, 1.0, jnp.where(lane < 6, -pieces, 0.0)).astype(BF16)

    out = pl.BlockSpec((N_FOX, tc, HEAD), lambda i: (0, i, 0))
    return pl.pallas_call(
        body, name=name, grid=(S // tc,),
        in_specs=[pl.BlockSpec((tc, LANE), lambda i: (i, 0)), pl.BlockSpec((1, LANE), lambda i: (0, 0))],
        out_specs=[out, out],
        out_shape=[jax.ShapeDtypeStruct((N_FOX, S, HEAD), BF16)] * 2,
        scratch_shapes=[pltpu.VMEM((8, LANE), F32)],
        compiler_params=_params(("arbitrary",)),
    )(pfl, bf)


def _fox_dlogit(dc, pfl, bf, *, tc, name):
    S = pfl.shape[0]
    nt = S // tc

    def body(dc_ref, fl_ref, bf_ref, dfl_ref, dbf_ref, carry_ref):
        i = pl.program_id(0)

        @pl.when(i == 0)
        def _():
            carry_ref[...] = jnp.zeros_like(carry_ref)
            dbf_ref[...] = jnp.zeros_like(dbf_ref)

        dlogf = jnp.dot(_tri(tc, False), dc_ref[...], precision=lax.Precision.HIGHEST,
                        preferred_element_type=F32) + carry_ref[0:1, :]
        carry_ref[0:1, :] = dlogf[0:1, :]
        dz = dlogf * _sigmoid(-(fl_ref[...] + bf_ref[...]))
        dfl_ref[...] = dz.astype(BF16)
        dbf_ref[0:1, :] += jnp.sum(dz, axis=0, keepdims=True)

    rev = pl.BlockSpec((tc, LANE), lambda i: (nt - 1 - i, 0))
    return pl.pallas_call(
        body, name=name, grid=(nt,),
        in_specs=[rev, rev, pl.BlockSpec((1, LANE), lambda i: (0, 0))],
        out_specs=[rev, pl.BlockSpec((8, LANE), lambda i: (0, 0))],
        out_shape=[jax.ShapeDtypeStruct((S, LANE), BF16), jax.ShapeDtypeStruct((8, LANE), F32)],
        scratch_shapes=[pltpu.VMEM((8, LANE), F32)],
        compiler_params=_params(("arbitrary",)),
    )(dc, pfl, bf)


def _as_row(col):
    return jnp.transpose(jnp.broadcast_to(col, (col.shape[0], LANE)))[0:1, :]


def _causal_part(rows, cols, row0, col0, q_is_row=True):
    r = lax.broadcasted_iota(jnp.int32, (rows, cols), 0) + row0
    c = lax.broadcasted_iota(jnp.int32, (rows, cols), 1) + col0
    return (r >= c) if q_is_row else (c >= r)


LOG2E = 1.4426950408889634
FOX_SCALE2 = (HEAD ** -0.5) * LOG2E


def _fox_fwd(qkv, pg, qa, ka, y_all, yt_all, *, tq, nsub, name):
    S = qkv.shape[0]
    nq = S // tq
    rs = tq // nsub
    ybase = CONV_W // HEAD

    def body(q_ref, qa_ref, k_ref, ka_ref, v_ref, g_ref, yin_ref, ytin_ref, y_ref, yt_ref, o_ref, m_ref, linv_ref):
        i = pl.program_id(1)
        q = jnp.concatenate([q_ref[...], qa_ref[...]], axis=1)
        qs = [q[r * rs:(r + 1) * rs] for r in range(nsub)]
        ones = jnp.where(lax.broadcasted_iota(jnp.int32, (tq, HEAD), 1) == 0, 1.0, 0.0).astype(BF16)

        def blk(j, carry, masked):
            off = pl.multiple_of(j * tq, tq)
            kj = jnp.concatenate([k_ref[pl.ds(off, tq), :], ka_ref[pl.ds(off, tq), :]], axis=1)
            vj = jnp.concatenate([v_ref[pl.ds(off, tq), :], ones], axis=1)
            out = []
            for r in range(nsub):
                m, acc = carry[r]
                nk = (r + 1) * rs if masked else tq
                s = lax.dot_general(qs[r], kj[:nk], NT, preferred_element_type=F32) * FOX_SCALE2
                if masked:
                    s = jnp.where(_causal_part(rs, nk, r * rs, 0), s, NEG)
                m_new = jnp.maximum(m, jnp.ceil(jnp.max(s, axis=-1, keepdims=True)))
                pb = jnp.exp2(s - m_new).astype(BF16)
                acc = jnp.exp2(m - m_new) * acc + jnp.dot(pb, vj[:nk], preferred_element_type=F32)
                out.append((m_new, acc))
            return tuple(out)

        init = tuple((jnp.full((rs, 1), NEG, F32), jnp.zeros((rs, 2 * HEAD), F32)) for _ in range(nsub))
        carry = lax.fori_loop(0, i, lambda j, c: blk(j, c, False), init)
        carry = blk(i, carry, True)
        m = jnp.concatenate([c[0] for c in carry], axis=0)
        acc = jnp.concatenate([c[1] for c in carry], axis=0)
        linv = 1.0 / acc[:, HEAD:HEAD + 1]
        o = acc[:, :HEAD] * linv
        g = g_ref[...]
        yv = o * g * _sigmoid(g)
        y_ref[...] = yv.astype(BF16)
        yt_ref[...] = jnp.transpose(yv).astype(BF16)
        o_ref[...] = o
        m_ref[...] = _as_row(m)
        linv_ref[...] = _as_row(linv)

    tile = lambda base: pl.BlockSpec((tq, HEAD), lambda h, i, base=base: (i, base + h))
    full = lambda base: pl.BlockSpec((S, HEAD), lambda h, i, base=base: (0, base + h))
    rowv = pl.BlockSpec((None, None, 1, tq), lambda h, i: (h, i, 0, 0))
    stat = jax.ShapeDtypeStruct((N_FOX, nq, 1, tq), F32)
    return pl.pallas_call(
        body, name=name, grid=(N_FOX, nq),
        in_specs=[tile(0), pl.BlockSpec((None, tq, HEAD), lambda h, i: (h, i, 0)),
                  full(N_FOX), pl.BlockSpec((None, S, HEAD), lambda h, i: (h, 0, 0)),
                  full(2 * N_FOX), tile(0), ANY, ANY],
        out_specs=[tile(ybase), pl.BlockSpec((HEAD, tq), lambda h, i: (ybase + h, i)), tile(0), rowv, rowv],
        out_shape=[jax.ShapeDtypeStruct(y_all.shape, BF16), jax.ShapeDtypeStruct(yt_all.shape, BF16),
                   jax.ShapeDtypeStruct((S, FOX_W), F32), stat, stat],
        input_output_aliases={6: 0, 7: 1},
        compiler_params=_params(("parallel", "arbitrary")),
    )(qkv, qa, qkv, ka, qkv, pg, y_all, yt_all)


def _fox_bwd_prep(dy, o, pg, *, tq, name):
    S = o.shape[0]
    base = CONV_W // HEAD

    def body(dy_ref, o_ref, g_ref, do_ref, dg_ref, dl_ref):
        g = g_ref[...]
        sg = _sigmoid(g)
        dyv = dy_ref[...]
        ov = o_ref[...]
        dob = (dyv * g * sg).astype(BF16)
        do_ref[...] = dob
        dg_ref[...] = (dyv * ov * _dsilu(g, sg)).astype(BF16)
        dl_ref[...] = _as_row(jnp.sum(dob.astype(F32) * ov, axis=-1, keepdims=True))

    tile = lambda b: pl.BlockSpec((tq, HEAD), lambda h, i, b=b: (i, b + h))
    return pl.pallas_call(
        body, name=name, grid=(N_FOX, S // tq),
        in_specs=[tile(base), tile(0), tile(0)],
        out_specs=[tile(0), tile(0), pl.BlockSpec((None, None, 1, tq), lambda h, i: (h, i, 0, 0))],
        out_shape=[jax.ShapeDtypeStruct((S, FOX_W), BF16), jax.ShapeDtypeStruct((S, FOX_W), BF16),
                   jax.ShapeDtypeStruct((N_FOX, S // tq, 1, tq), F32)],
        compiler_params=_params(("parallel", "parallel")),
    )(dy, o, pg)


def _fox_bwd(qkv, do, qa, ka, m_row, linv_row, delta_row, *, tq, nsub, name):
    S = qkv.shape[0]
    nq = S // tq
    cs = tq // nsub
    scale = HEAD ** -0.5

    def body(k_ref, ka_ref, v_ref, q_ref, qa_ref, do_ref, mr_ref, lir_ref, dlr_ref, dq_ref, dk_ref, dv_ref, dc_ref):
        j = pl.program_id(1)

        @pl.when(j == 0)
        def _():
            dq_ref[...] = jnp.zeros_like(dq_ref)

        kj = k_ref[...]
        kja = jnp.concatenate([kj, ka_ref[...]], axis=1)
        vj = v_ref[...]

        def blk(i, carry, masked):
            dk, dv, dc = carry
            m_i = mr_ref[i]
            linv_i = lir_ref[i]
            dl_i = dlr_ref[i]
            for c in range(nsub):
                off = pl.multiple_of(i * tq + c * cs, cs)
                cols = slice(c * cs, (c + 1) * cs)
                qi = q_ref[pl.ds(off, cs), :]
                qia = jnp.concatenate([qi, qa_ref[pl.ds(off, cs), :]], axis=1)
                doi = do_ref[pl.ds(off, cs), :]
                nk = (c + 1) * cs if masked else tq
                st = lax.dot_general(kja[:nk], qia, NT, preferred_element_type=F32) * FOX_SCALE2
                if masked:
                    st = jnp.where(_causal_part(nk, cs, 0, c * cs, q_is_row=False), st, NEG)
                grow = lambda u, nk=nk: u if nk == tq else jnp.concatenate(
                    [u, jnp.zeros((tq - nk, u.shape[1]), F32)], axis=0)
                pt = jnp.exp2(st - m_i[:, cols]).astype(BF16).astype(F32) * linv_i[:, cols]
                dv = dv + grow(jnp.dot(pt.astype(BF16), doi, preferred_element_type=F32))
                dpt = lax.dot_general(vj[:nk], doi, NT, preferred_element_type=F32)
                dst = pt * (dpt - dl_i[:, cols])
                dsb = dst.astype(BF16)
                dk = dk + grow(jnp.dot(dsb, qi, preferred_element_type=F32))
                dq_ref[pl.ds(off, cs), :] += lax.dot_general(dsb, kj[:nk], TN, preferred_element_type=F32) * scale
                dc = dc - grow(jnp.sum(dst, axis=-1, keepdims=True))
            return dk, dv, dc

        init = (jnp.zeros((tq, HEAD), F32), jnp.zeros((tq, HEAD), F32), jnp.zeros((tq, 1), F32))
        carry = blk(j, init, True)
        dk, dv, dc = lax.fori_loop(j + 1, nq, lambda i, c: blk(i, c, False), carry)
        dk_ref[...] = (dk * scale).astype(BF16)
        dv_ref[...] = dv.astype(BF16)
        dc_ref[...] = _as_row(dc)

    tile = lambda base: pl.BlockSpec((tq, HEAD), lambda h, j, base=base: (j, base + h))
    full = lambda base: pl.BlockSpec((S, HEAD), lambda h, j, base=base: (0, base + h))
    atile = pl.BlockSpec((None, tq, HEAD), lambda h, j: (h, j, 0))
    afull = pl.BlockSpec((None, S, HEAD), lambda h, j: (h, 0, 0))
    rowt = pl.BlockSpec((None, None, 1, tq), lambda h, j: (h, j, 0, 0))
    rowv = pl.BlockSpec((None, nq, 1, tq), lambda h, j: (h, 0, 0, 0))
    return pl.pallas_call(
        body, name=name, grid=(N_FOX, nq),
        in_specs=[tile(N_FOX), atile, tile(2 * N_FOX), full(0), afull, full(0), rowv, rowv, rowv],
        out_specs=[full(0), tile(0), tile(0), rowt],
        out_shape=[jax.ShapeDtypeStruct((S, FOX_W), F32), jax.ShapeDtypeStruct((S, FOX_W), BF16),
                   jax.ShapeDtypeStruct((S, FOX_W), BF16), jax.ShapeDtypeStruct((N_FOX, nq, 1, tq), F32)],
        compiler_params=_params(("arbitrary", "arbitrary")),
    )(qkv, ka, qkv, qkv, qa, do, m_row, linv_row, delta_row)


def _mem_heads(mq, mkv, h):
    lo = h * HEAD
    qh = mq[:, lo:lo + HEAD].astype(BF16)
    kh = mkv[:, lo:lo + HEAD].astype(BF16)
    vh = mkv[:, MEM_W + lo:MEM_W + lo + HEAD].astype(BF16)
    return qh, kh, vh


def _mem_softmax(qh, kh):
    s = lax.dot_general(qh, kh, NT, preferred_element_type=F32) * (HEAD ** -0.5)
    e = jnp.exp(s - jnp.max(s, axis=-1, keepdims=True))
    return e / jnp.sum(e, axis=-1, keepdims=True)


def _mem_fwd(pg, mkv, y_all, yt_all, *, tq, name):
    S = pg.shape[0]
    M = mkv.shape[0]
    qb = FOX_W // MEM_W
    yb = (CONV_W + FOX_W) // MEM_W

    def body(mq_ref, g_ref, mkv_ref, yin_ref, ytin_ref, y_ref, yt_ref):
        mq = mq_ref[...]
        mkvv = mkv_ref[...]
        for h in range(N_MEMH):
            qh, kh, vh = _mem_heads(mq, mkvv, h)
            p = _mem_softmax(qh, kh)
            o = jnp.dot(p.astype(BF16), vh, preferred_element_type=F32)
            g = g_ref[:, h * HEAD:(h + 1) * HEAD]
            yv = o * g * _sigmoid(g)
            y_ref[:, h * HEAD:(h + 1) * HEAD] = yv.astype(BF16)
            yt_ref[h * HEAD:(h + 1) * HEAD, :] = jnp.transpose(yv).astype(BF16)

    return pl.pallas_call(
        body, name=name, grid=(S // tq,),
        in_specs=[pl.BlockSpec((tq, MEM_W), lambda i: (i, qb)), pl.BlockSpec((tq, MEM_W), lambda i: (i, qb + 1)),
                  pl.BlockSpec((M, 2 * MEM_W), lambda i: (0, 0)), ANY, ANY],
        out_specs=[pl.BlockSpec((tq, MEM_W), lambda i: (i, yb)), pl.BlockSpec((MEM_W, tq), lambda i: (yb, i))],
        out_shape=[jax.ShapeDtypeStruct(y_all.shape, BF16), jax.ShapeDtypeStruct(yt_all.shape, BF16)],
        input_output_aliases={3: 0, 4: 1},
        compiler_params=_params(("parallel",)),
    )(pg, pg, mkv, y_all, yt_all)


def _mem_bwd(pg, mkv, dy, *, tq, name):
    S = pg.shape[0]
    M = mkv.shape[0]
    qb = FOX_W // MEM_W
    yb = (CONV_W + FOX_W) // MEM_W
    scale = HEAD ** -0.5

    def body(mq_ref, g_ref, mkv_ref, dy_ref, dmq_ref, dg_ref, dmkv_ref):
        i = pl.program_id(0)

        @pl.when(i == 0)
        def _():
            dmkv_ref[...] = jnp.zeros_like(dmkv_ref)

        mq = mq_ref[...]
        mkvv = mkv_ref[...]
        for h in range(N_MEMH):
            lo = h * HEAD
            qh, kh, vh = _mem_heads(mq, mkvv, h)
            p = _mem_softmax(qh, kh)
            o = jnp.dot(p.astype(BF16), vh, preferred_element_type=F32)
            g = g_ref[:, lo:lo + HEAD]
            sg = _sigmoid(g)
            dyh = dy_ref[:, lo:lo + HEAD]
            do = dyh * g * sg
            dg_ref[:, lo:lo + HEAD] = (dyh * o * _dsilu(g, sg)).astype(BF16)
            dob = do.astype(BF16)
            dp = lax.dot_general(dob, vh, NT, preferred_element_type=F32)
            ds = p * (dp - jnp.sum(do * o, axis=-1, keepdims=True))
            dsb = ds.astype(BF16)
            dmq_ref[:, lo:lo + HEAD] = (jnp.dot(dsb, kh, preferred_element_type=F32) * scale).astype(BF16)
            dmkv_ref[:, lo:lo + HEAD] += lax.dot_general(dsb, qh, TN, preferred_element_type=F32) * scale
            dmkv_ref[:, MEM_W + lo:MEM_W + lo + HEAD] += lax.dot_general(
                p.astype(BF16), dob, TN, preferred_element_type=F32)

    return pl.pallas_call(
        body, name=name, grid=(S // tq,),
        in_specs=[pl.BlockSpec((tq, MEM_W), lambda i: (i, qb)), pl.BlockSpec((tq, MEM_W), lambda i: (i, qb + 1)),
                  pl.BlockSpec((M, 2 * MEM_W), lambda i: (0, 0)), pl.BlockSpec((tq, MEM_W), lambda i: (i, yb))],
        out_specs=[pl.BlockSpec((tq, MEM_W), lambda i: (i, 0)), pl.BlockSpec((tq, MEM_W), lambda i: (i, 0)),
                   pl.BlockSpec((M, 2 * MEM_W), lambda i: (0, 0))],
        out_shape=[jax.ShapeDtypeStruct((S, MEM_W), BF16), jax.ShapeDtypeStruct((S, MEM_W), BF16),
                   jax.ShapeDtypeStruct((M, 2 * MEM_W), F32)],
        compiler_params=_params(("arbitrary",)),
    )(pg, pg, mkv, dy)


def _final(x2, target, fg, *, tm, name):
    S, D = x2.shape

    def body(x_ref, t_ref, g_ref, dx_ref, dxb_ref, dg_ref, ls_ref):
        i = pl.program_id(0)

        @pl.when(i == 0)
        def _():
            dg_ref[...] = jnp.zeros_like(dg_ref)
            ls_ref[...] = jnp.zeros_like(ls_ref)

        xv = x_ref[...]
        r = lax.rsqrt(jnp.mean(xv * xv, axis=-1, keepdims=True) + EPS)
        n = xv * r
        g = g_ref[...]
        diff = n * g - t_ref[...]
        ls_ref[...] += jnp.sum(diff * diff)
        dout = diff * (1.0 / D)
        dg_ref[...] += jnp.sum(dout * n, axis=0, keepdims=True)
        dn = dout * g
        dx = r * (dn - n * jnp.mean(dn * n, axis=-1, keepdims=True))
        dx_ref[...] = dx
        dxb_ref[...] = dx.astype(BF16)

    row = pl.BlockSpec((tm, D), lambda i: (i, 0))
    vec = pl.BlockSpec((1, D), lambda i: (0, 0))
    return pl.pallas_call(
        body, name=name, grid=(S // tm,),
        in_specs=[row, row, vec],
        out_specs=[row, row, vec, pl.BlockSpec((8, LANE), lambda i: (0, 0))],
        out_shape=[jax.ShapeDtypeStruct((S, D), F32), jax.ShapeDtypeStruct((S, D), BF16),
                   jax.ShapeDtypeStruct((1, D), F32), jax.ShapeDtypeStruct((8, LANE), F32)],
        compiler_params=_params(("arbitrary",)),
    )(x2, target, fg)


def _adamw(w, g, m, v, *, name):
    R, C = w.shape
    tr, tc = R, C
    for cand in (256, 128, 64, 32, 16, 8):
        if R % cand == 0 and R > cand:
            tr = cand
            break
    if tr == R and R > 256 and C % 256 == 0:
        tc = 256
    c1 = 1.0 - ADAM_B1 ** ADAM_STEP
    c2 = 1.0 - ADAM_B2 ** ADAM_STEP

    def body(w_ref, g_ref, m_ref, v_ref, d_ref, nm_ref, nv_ref):
        gv = g_ref[...]
        nm = ADAM_B1 * m_ref[...] + (1.0 - ADAM_B1) * gv
        nv = ADAM_B2 * v_ref[...] + (1.0 - ADAM_B2) * (gv * gv)
        nm_ref[...] = nm
        nv_ref[...] = nv
        d_ref[...] = -ADAM_LR * ((nm / c1) / (jnp.sqrt(nv / c2) + ADAM_EPS) + ADAM_WD * w_ref[...])

    spec = pl.BlockSpec((tr, tc), lambda i, j: (i, j))
    shp = jax.ShapeDtypeStruct((R, C), F32)
    return pl.pallas_call(
        body, name=name, grid=(R // tr, C // tc),
        in_specs=[spec] * 4, out_specs=[spec] * 3, out_shape=[shp] * 3,
        compiler_params=_params(("parallel", "parallel")),
    )(w, g, m, v)


def _sum4(q, own, me, *, name):
    _, R, C = q.shape
    tr = R
    for cand in (256, 128, 64, 32, 16, 8):
        if R % cand == 0 and R > cand:
            tr = cand
            break

    def body(me_ref, own_ref, q1_ref, q2_ref, q3_ref, o_ref):
        f = lambda r: r[...].astype(F32)
        o_ref[...] = ((f(own_ref) + f(q1_ref)) + f(q2_ref)) + f(q3_ref)

    blk = lambda d: pl.BlockSpec((None, tr, C), lambda i, me_ref, d=d: (me_ref[0] ^ d, i, 0))
    return pl.pallas_call(
        body, name=name,
        grid_spec=pltpu.PrefetchScalarGridSpec(
            num_scalar_prefetch=1, grid=(R // tr,),
            in_specs=[blk(0), blk(1), blk(2), blk(3)],
            out_specs=pl.BlockSpec((tr, C), lambda i, me_ref: (i, 0))),
        out_shape=jax.ShapeDtypeStruct((R, C), F32),
        compiler_params=_params(("parallel",)),
    )(jnp.reshape(me, (1,)).astype(jnp.int32), own, q, q, q)


def _add_sibling(g, got, c, *, name):
    K, _, R, C = g.shape
    tr = R
    for cand in (256, 128, 64, 32, 16, 8):
        if R % cand == 0 and R > cand:
            tr = cand
            break

    def body(c_ref, a_ref, b_ref, o_ref):
        o_ref[...] = (a_ref[...] + b_ref[...]).astype(BF16)

    spec = pl.BlockSpec((None, tr, C), lambda k, i, c_ref: (k, i, 0))
    return pl.pallas_call(
        body, name=name,
        grid_spec=pltpu.PrefetchScalarGridSpec(
            num_scalar_prefetch=1, grid=(K, R // tr),
            in_specs=[pl.BlockSpec((None, None, tr, C), lambda k, i, c_ref: (k, c_ref[0], i, 0)), spec],
            out_specs=spec),
        out_shape=jax.ShapeDtypeStruct((K, R, C), BF16),
        compiler_params=_params(("parallel", "parallel")),
    )(jnp.reshape(c, (1,)).astype(jnp.int32), g, got)


ANY = pl.BlockSpec(memory_space=pl.ANY)


def _other_chips(x, y):
    return [(d, 1 - x if d & 2 else x, 1 - y if d & 1 else y) for d in (1, 2, 3)]


def _gather_ici_copies(src_refs, out_refs, ici_send, ici_recv):
    x, y, c = lax.axis_index("x"), lax.axis_index("y"), lax.axis_index("c")
    me = 2 * x + y
    return [pltpu.make_async_remote_copy(
        src_ref=src_refs[t].at[c], dst_ref=out_refs[t].at[me, c],
        send_sem=ici_send.at[3 * t + d - 1], recv_sem=ici_recv.at[3 * t + d - 1],
        device_id=(tx, ty, c), device_id_type=MESH)
        for t in range(len(src_refs)) for d, tx, ty in _other_chips(x, y)]


def _gather_finish(src_refs, out_refs, ici_send, ici_recv, d2d_send, d2d_recv):
    n = len(src_refs)
    x, y, c = lax.axis_index("x"), lax.axis_index("y"), lax.axis_index("c")
    passed = []
    for t in range(n):
        for d, tx, ty in _other_chips(x, y):
            landed = out_refs[t].at[2 * tx + ty, c]
            pltpu.make_async_remote_copy(
                src_ref=landed, dst_ref=landed,
                send_sem=ici_send.at[3 * t + d - 1], recv_sem=ici_recv.at[3 * t + d - 1],
                device_id=(tx, ty, c), device_id_type=MESH).wait_recv()
            cp = pltpu.make_async_remote_copy(
                src_ref=landed, dst_ref=landed,
                send_sem=d2d_send.at[3 * t + d - 1], recv_sem=d2d_recv.at[3 * t + d - 1],
                device_id=(x, y, 1 - c), device_id_type=MESH)
            cp.start()
            passed.append(cp)
    for t in range(n):
        for d, tx, ty in _other_chips(x, y):
            theirs = out_refs[t].at[2 * tx + ty, 1 - c]
            pltpu.make_async_remote_copy(
                src_ref=theirs, dst_ref=theirs,
                send_sem=d2d_send.at[3 * t + d - 1], recv_sem=d2d_recv.at[3 * t + d - 1],
                device_id=(x, y, 1 - c), device_id_type=MESH).wait_recv()
    for cp in _gather_ici_copies(src_refs, out_refs, ici_send, ici_recv) + passed:
        cp.wait_send()


def _call_with_gather(body, shards, first, last, *, n_in, n_out, in_specs, out_specs, out_shape, scratch_shapes,
                      **kw):
    n = len(shards)
    n_scr = len(scratch_shapes)

    def wrapped(*refs):
        ins, srcs = refs[:n_in], refs[n_in:n_in + n]
        outs = refs[n_in + n:n_in + n + n_out]
        gouts = refs[n_in + n + n_out:n_in + 2 * n + n_out]
        scr = refs[n_in + 2 * n + n_out:n_in + 2 * n + n_out + n_scr]
        ici_send, ici_recv, d2d_send, d2d_recv = refs[n_in + 2 * n + n_out + n_scr:]

        @pl.when(first())
        def _():
            for cp in _gather_ici_copies(srcs, gouts, ici_send, ici_recv):
                cp.start()

        body(*ins, *outs, *scr)

        @pl.when(last())
        def _():
            _gather_finish(srcs, gouts, ici_send, ici_recv, d2d_send, d2d_recv)

    sem = pltpu.SemaphoreType.DMA((3 * n,))
    return pl.pallas_call(
        wrapped,
        in_specs=list(in_specs) + [ANY] * n,
        out_specs=list(out_specs) + [ANY] * n,
        out_shape=list(out_shape) + [jax.ShapeDtypeStruct((N_CHIPS,) + s.shape, s.dtype) for s in shards],
        scratch_shapes=list(scratch_shapes) + [sem, sem, sem, sem],
        **kw)


def _scatter_copies(src_refs, out_refs, send_sems, recv_sems):
    n = len(src_refs)
    x, y, c = lax.axis_index("x"), lax.axis_index("y"), lax.axis_index("c")
    me = 2 * x + y
    copies = []
    for t in range(n):
        for d, tx, ty in _other_chips(x, y):
            copies.append(pltpu.make_async_remote_copy(
                src_ref=src_refs[t].at[2 * tx + ty], dst_ref=out_refs[t].at[me],
                send_sem=send_sems.at[3 * t + d - 1], recv_sem=recv_sems.at[3 * t + d - 1],
                device_id=(tx, ty, c), device_id_type=MESH))
    return copies


def _mm_scatter(a, b, a2, b2, srcs, *, tm, tn, tk, name):
    M, K = a.shape
    N = b.shape[1]
    K2 = a2.shape[1]
    tm, tn, tk = min(tm, M), min(tn, N), min(tk, K)
    assert M % tm == 0 and N % tn == 0 and K % tk == 0, (name, M, N, K, tm, tn, tk)
    gi, gj, gk = M // tm, N // tn, K // tk
    n = len(srcs)

    def body(*refs):
        a_ref, b_ref, a2_ref, b2_ref = refs[:4]
        src_refs = refs[4:4 + n]
        o_ref = refs[4 + n]
        out_refs = refs[5 + n:5 + 2 * n]
        acc_ref, send_sems, recv_sems = refs[5 + 2 * n:]
        i, j, k = pl.program_id(0), pl.program_id(1), pl.program_id(2)

        @pl.when((i == 0) & (j == 0) & (k == 0))
        def _():
            for cp in _scatter_copies(src_refs, out_refs, send_sems, recv_sems):
                cp.start()

        @pl.when(k == 0)
        def _():
            acc_ref[...] = jnp.dot(a2_ref[...].astype(BF16), b2_ref[...].astype(BF16), preferred_element_type=F32)

        acc_ref[...] += jnp.dot(a_ref[...].astype(BF16), b_ref[...].astype(BF16), preferred_element_type=F32)

        @pl.when(k == gk - 1)
        def _():
            o_ref[...] = acc_ref[...]

        @pl.when((i == gi - 1) & (j == gj - 1) & (k == gk - 1))
        def _():
            for cp in _scatter_copies(src_refs, out_refs, send_sems, recv_sems):
                cp.wait()

    o_spec = pl.BlockSpec((tm, tn), lambda i, j, k: (i, j))
    sem = pltpu.SemaphoreType.DMA((3 * n,))
    res = pl.pallas_call(
        body, name=name, grid=(gi, gj, gk),
        in_specs=[pl.BlockSpec((tm, tk), lambda i, j, k: (i, k)), pl.BlockSpec((tk, tn), lambda i, j, k: (k, j)),
                  pl.BlockSpec((tm, K2), lambda i, j, k: (i, 0)), pl.BlockSpec((K2, tn), lambda i, j, k: (0, j))]
        + [ANY] * n,
        out_specs=[o_spec] + [ANY] * n,
        out_shape=[jax.ShapeDtypeStruct((M, N), F32)] + [jax.ShapeDtypeStruct(s.shape, s.dtype) for s in srcs],
        scratch_shapes=[pltpu.VMEM((tm, tn), F32), sem, sem],
        compiler_params=_params(("arbitrary", "arbitrary", "arbitrary")),
    )(a, b, a2, b2, *srcs)
    return res[0], res[1:]


def _swap_halves(grads, *, name):
    n = len(grads)

    def body(*refs):
        src_refs, out_refs = refs[:n], refs[n:2 * n]
        send_sems, recv_sems = refs[2 * n:]
        x, y, c = lax.axis_index("x"), lax.axis_index("y"), lax.axis_index("c")
        copies = []
        for t in range(n):
            for k in range(N_CHIPS):
                cp = pltpu.make_async_remote_copy(
                    src_ref=src_refs[t].at[k, 1 - c], dst_ref=out_refs[t].at[k],
                    send_sem=send_sems.at[N_CHIPS * t + k], recv_sem=recv_sems.at[N_CHIPS * t + k],
                    device_id=(x, y, 1 - c), device_id_type=MESH)
                cp.start()
                copies.append(cp)
        for cp in copies:
            cp.wait()

    sem = pltpu.SemaphoreType.DMA((N_CHIPS * n,))
    return pl.pallas_call(
        body, name=name,
        in_specs=[ANY] * n, out_specs=[ANY] * n,
        out_shape=[jax.ShapeDtypeStruct((N_CHIPS,) + g.shape[2:], g.dtype) for g in grads],
        scratch_shapes=[sem, sem],
    )(*grads)


def _sibling_swap(srcs, *, name):
    n = len(srcs)

    def body(*refs):
        src_refs, out_refs = refs[:n], refs[n:2 * n]
        send_sems, recv_sems = refs[2 * n:]
        x, y, c = lax.axis_index("x"), lax.axis_index("y"), lax.axis_index("c")
        copies = []
        for t in range(n):
            cp = pltpu.make_async_remote_copy(
                src_ref=src_refs[t], dst_ref=out_refs[t],
                send_sem=send_sems.at[t], recv_sem=recv_sems.at[t],
                device_id=(x, y, 1 - c), device_id_type=MESH)
            cp.start()
            copies.append(cp)
        for cp in copies:
            cp.wait()

    sem = pltpu.SemaphoreType.DMA((n,))
    return pl.pallas_call(
        body, name=name,
        in_specs=[ANY] * n, out_specs=[ANY] * n,
        out_shape=[jax.ShapeDtypeStruct(s.shape, s.dtype) for s in srcs],
        scratch_shapes=[sem, sem],
    )(*srcs)


def _allreduce_small(v, *, name):
    R, C = v.shape

    def body(v_ref, o_ref, buf_ref, send_sems, recv_sems):
        x, y, c = lax.axis_index("x"), lax.axis_index("y"), lax.axis_index("c")
        me = 4 * x + 2 * y + c
        buf_ref[me] = v_ref[...]
        copies = []
        for d in range(1, N_DEV):
            tx = 1 - x if d & 4 else x
            ty = 1 - y if d & 2 else y
            tc = 1 - c if d & 1 else c
            cp = pltpu.make_async_remote_copy(
                src_ref=v_ref, dst_ref=buf_ref.at[me],
                send_sem=send_sems.at[d - 1], recv_sem=recv_sems.at[d - 1],
                device_id=(tx, ty, tc), device_id_type=MESH)
            cp.start()
            copies.append(cp)
        for cp in copies:
            cp.wait()
        acc = buf_ref[0]
        for k in range(1, N_DEV):
            acc = acc + buf_ref[k]
        o_ref[...] = acc

    return pl.pallas_call(
        body, name=name,
        in_specs=[pl.BlockSpec(memory_space=pltpu.VMEM)],
        out_specs=pl.BlockSpec(memory_space=pltpu.VMEM),
        out_shape=jax.ShapeDtypeStruct((R, C), F32),
        scratch_shapes=[pltpu.VMEM((N_DEV, R, C), F32), pltpu.SemaphoreType.DMA((N_DEV - 1,)),
                        pltpu.SemaphoreType.DMA((N_DEV - 1,))],
    )(v)


_A0, _B0, _GC0 = 0, CONV_W, 2 * CONV_W
_Q0 = 3 * CONV_W
_FL0 = _Q0 + 3 * FOX_W
_FG0 = _FL0 + N_FOX
_MQ0 = _FG0 + FOX_W
_MG0 = _MQ0 + MEM_W
_DIN = _MG0 + MEM_W
_WMAIN = _DIN - N_FOX
_PC_N = 3 * CONV_W
_QKV_N = 3 * FOX_W
_PG_N = FOX_W + 2 * MEM_W


def _rows_pad8(a):
    r = a.shape[0]
    p = (-r) % 8
    return jnp.pad(a, ((0, p), (0, 0))) if p else a


def _step(x, mem, target, norm_g, mem_norm_g, final_g, b_f, conv_b, ln_g, ln_b,
          s_in, s_pw, s_mkv, s_out, s_cw, ci, me):
    S, D = x.shape
    M = mem.shape[0]
    tq = min(512, S)
    tf = min(1024, S)
    tt = min(256, S)
    tm = min(256, S)
    nin = s_in.shape[2]
    bf_pad = jnp.pad(b_f, ((0, 0), (0, LANE - N_FOX)))
    whole = lambda g, s: lax.dynamic_update_slice(g, s[None], (me, 0, 0, 0))

    h, h_t, (g_in,) = _rms_fwd(x, norm_g, tm=tm, name="rms_fwd_gather_w_in", gather=[s_in])
    w_in = jnp.transpose(whole(g_in, s_in).reshape(N_CHIPS, D, nin), (1, 0, 2)).reshape(D, N_CHIPS * nin)
    w_main = jnp.concatenate([w_in[:, :_FL0], w_in[:, _FG0:]], axis=1)
    w_fl = jnp.pad(w_in[:, _FL0:_FG0], ((0, 0), (0, LANE - N_FOX)))
    w_main_t, w_fl_t = w_main.T, w_fl.T

    pc, (g_pw, g_mkv, g_out, g_cw) = _mm(h, w_main, tm=1024, tn=512, tk=2048, n=_PC_N, b_off=0,
                                         name="proj_conv_gather_rest", gather=[s_pw, s_mkv, s_out, s_cw])
    w_pw = whole(g_pw, s_pw).reshape(CONV_W, CONV_W)
    w_mkv = whole(g_mkv, s_mkv).reshape(D, 2 * MEM_W)
    w_out = whole(g_out, s_out).reshape(MIX_W, D)
    w_out_t = w_out.T
    cw_pad = jnp.transpose(whole(g_cw, s_cw).reshape(N_CHIPS, CONV_HALO, CONV_W // N_CHIPS),
                           (1, 0, 2)).reshape(CONV_HALO, CONV_W)

    qkv = _mm(h, w_main, out_dtype=BF16, tm=1024, tn=512, tk=2048, n=_QKV_N, b_off=_PC_N // 512, name="proj_qkv")
    pg = _mm(h, w_main, tm=1024, tn=512, tk=2048, n=_PG_N, b_off=(_PC_N + _QKV_N) // 512, name="proj_gate")
    pfl = _mm(h, w_fl, tm=1024, tn=LANE, tk=2048, name="proj_logit")

    y, y_t, u1s = _conv_fwd(pc, cw_pad, conv_b, ln_g, ln_b, w_pw, tt=tt, name="conv_fwd")

    qa, ka = _fox_cumsum(pfl, bf_pad, tc=tt, name="fox_cumsum")
    y, y_t, o_fox, fox_m, fox_linv = _fox_fwd(qkv, pg, qa, ka, y, y_t, tq=tf, nsub=2, name="fox_fwd")

    hm = _rms_fwd(mem, mem_norm_g, tm=min(256, M), name="rms_mem")
    mkv = _mm(hm, w_mkv, tm=256, tn=512, tk=2048, name="mem_kv")
    y, y_t = _mem_fwd(pg, mkv, y, y_t, tq=tq, name="mem_fwd")

    x2 = _mm(y, w_out, add=x, tm=1024, tn=512, tk=2048, name="out_proj")
    dx2, dx2_b, dfg, sq = _final(x2, target, final_g.reshape(1, D), tm=min(512, S), name="final")

    dy = _mm(dx2_b, w_out_t, tm=1024, tn=512, tk=2048, name="d_y")
    dw_out = _mm(y_t, dx2_b, tm=1024, tn=512, tk=2048, name="d_w_out")

    dpc, dw_pw, dcw, dsm = _conv_bwd(pc, u1s, dy, cw_pad, ln_g, ln_b, w_pw, tt=tt, name="conv_bwd")

    do, dfgate, delta = _fox_bwd_prep(dy, o_fox, pg, tq=tf, name="fox_bwd_prep")
    dq, dk, dv, dc = _fox_bwd(qkv, do, qa, ka, fox_m, fox_linv, delta, tq=tf, nsub=2, name="fox_bwd")
    dc_pad = jnp.pad(dc.reshape(N_FOX, S).T, ((0, 0), (0, LANE - N_FOX)))
    dfl, dbf = _fox_dlogit(dc_pad, pfl, bf_pad, tc=tt, name="fox_dlogit")

    dmq, dmgate, dmkv = _mem_bwd(pg, mkv, dy, tq=tq, name="mem_bwd")
    dw_mkv = _mm(hm, dmkv, ta=True, tm=512, tn=512, tk=256, name="d_w_mkv")
    dhm = _mm(dmkv, w_mkv, tb=True, tm=256, tn=512, tk=1024, name="d_hm")
    _, dmg = _rms_bwd(mem, mem_norm_g, dhm, None, tm=min(256, M), name="rms_mem_bwd")

    dproj = jnp.concatenate([dpc, dq.astype(BF16), dk, dv, dfgate, dmq, dmgate], axis=1)
    dw_main = _mm(h_t, dproj, tm=1024, tn=_WMAIN // 4, tk=1024, name="d_w_main")
    dw_fl = _mm(h_t, dfl, tm=1024, tn=LANE, tk=1024, name="d_w_logit")

    def chip_columns(k):
        lo, hi = k * nin, (k + 1) * nin
        parts = []
        if lo < _FL0:
            parts.append(dw_main[:, lo:min(hi, _FL0)])
        if lo < _FG0 and hi > _FL0:
            parts.append(dw_fl[:, max(lo, _FL0) - _FL0:min(hi, _FG0) - _FL0])
        if hi > _FG0:
            parts.append(dw_main[:, max(lo, _FG0) - N_FOX:hi - N_FOX])
        return parts[0] if len(parts) == 1 else jnp.concatenate(parts, axis=1)

    big = [jnp.stack([chip_columns(k) for k in range(N_CHIPS)]),
           dw_pw.reshape(N_CHIPS, CONV_W // N_CHIPS, CONV_W),
           dw_mkv.reshape(N_CHIPS, D // N_CHIPS, 2 * MEM_W),
           dw_out.reshape(N_CHIPS, MIX_W // N_CHIPS, D)]
    big = [g.reshape(N_CHIPS, 2, g.shape[1] // 2, g.shape[2]) for g in big]
    got = _swap_halves(big, name="grad_swap_halves")
    chip = [_add_sibling(g, o, ci, name=f"grad_add_sibling_{t}") for t, (g, o) in enumerate(zip(big, got))]

    dh, parts = _mm_scatter(dproj, w_main_t, dfl, w_fl_t, chip, tm=1024, tn=1024, tk=_WMAIN // 4,
                            name="d_h_grad_scatter")
    grad_x, dng = _rms_bwd(x, norm_g, dh, dx2, tm=min(512, S), name="rms_bwd")

    red = [_sum4(p, own, me, name=f"grad_sum_chips_{t}") for t, (p, own) in enumerate(zip(parts, chip))]
    other = _sibling_swap(red, name="grad_swap_result")
    full = [jnp.where(ci == 0, jnp.concatenate([r, o], axis=0), jnp.concatenate([o, r], axis=0))
            for r, o in zip(red, other)]

    small = dict(norm_g=dng, mem_norm_g=dmg, final_g=dfg, b_f=dbf[0:1, :], conv_w=dcw,
                 conv_b=dsm[0:1], conv_ln_g=dsm[1:2], conv_ln_b=dsm[2:3])
    return sq[0, 0], grad_x, full, small


_SMALL_ORDER = ("norm_g", "mem_norm_g", "final_g", "b_f", "conv_w", "conv_b", "conv_ln_g", "conv_ln_b")


def _pack_small(small):
    parts, layout = [], []
    row = 0
    for k in _SMALL_ORDER:
        p = _rows_pad8(small[k].reshape(-1, LANE))
        layout.append((k, row, small[k].shape))
        parts.append(p)
        row += p.shape[0]
    return jnp.concatenate(parts, axis=0), layout


def _unpack_small(packed, layout):
    out = {}
    for k, row, shape in layout:
        nrow = (shape[0] * shape[1]) // LANE
        out[k] = packed[row:row + nrow].reshape(shape)
    return out


def kernel(x, mem, norm_g, mem_norm_g, w_in, b_f, conv_w, conv_b, conv_ln_g, conv_ln_b, w_conv_pw, w_mem_kv, w_out, final_g, loss_target, m_norm_g, m_mem_norm_g, m_w_in, m_b_f, m_conv_w, m_conv_b, m_conv_ln_g, m_conv_ln_b, m_w_conv_pw, m_w_mem_kv, m_w_out, m_final_g, v_norm_g, v_mem_norm_g, v_w_in, v_b_f, v_conv_w, v_conv_b, v_conv_ln_g, v_conv_ln_b, v_w_conv_pw, v_w_mem_kv, v_w_out, v_final_g):
    S, D = x.shape[1], x.shape[2]
    xi, yi, ci = lax.axis_index("x"), lax.axis_index("y"), lax.axis_index("c")
    chip = 2 * xi + yi

    halves = lambda a: a.reshape(2, a.shape[0] // 2, a.shape[1])
    cw_shard = jnp.pad(conv_w[0], ((0, CONV_HALO - CONV_K), (0, 0)))
    sq, grad_x, (g_w_in, g_w_pw, g_w_mkv, g_w_out), small = _step(
        x[0], mem[0], loss_target[0], norm_g, mem_norm_g, final_g, b_f, conv_b, conv_ln_g, conv_ln_b,
        halves(w_in[0].astype(BF16)), halves(w_conv_pw[0].astype(BF16)), halves(w_mem_kv[0].astype(BF16)),
        halves(w_out[0].astype(BF16)), halves(cw_shard), ci, chip)

    loss = lax.psum(sq, ("x", "y", "c")) * (0.5 / D)

    packed, layout = _pack_small(small)
    sm = _unpack_small(_allreduce_small(packed, name="small_all_reduce"), layout)
    cshard = CONV_W // N_CHIPS
    g_conv_w = lax.dynamic_slice_in_dim(sm["conv_w"][:CONV_K], chip * cshard, cshard, axis=1)

    grads = dict(
        norm_g=sm["norm_g"], mem_norm_g=sm["mem_norm_g"], w_in=g_w_in[None], b_f=sm["b_f"][:, :N_FOX],
        conv_w=g_conv_w[None], conv_b=sm["conv_b"], conv_ln_g=sm["conv_ln_g"], conv_ln_b=sm["conv_ln_b"],
        w_conv_pw=g_w_pw[None], w_mem_kv=g_w_mkv[None], w_out=g_w_out[None], final_g=sm["final_g"].reshape(D))
    weights = dict(norm_g=norm_g, mem_norm_g=mem_norm_g, w_in=w_in, b_f=b_f, conv_w=conv_w, conv_b=conv_b,
                   conv_ln_g=conv_ln_g, conv_ln_b=conv_ln_b, w_conv_pw=w_conv_pw, w_mem_kv=w_mem_kv, w_out=w_out,
                   final_g=final_g)
    ms = dict(norm_g=m_norm_g, mem_norm_g=m_mem_norm_g, w_in=m_w_in, b_f=m_b_f, conv_w=m_conv_w, conv_b=m_conv_b,
              conv_ln_g=m_conv_ln_g, conv_ln_b=m_conv_ln_b, w_conv_pw=m_w_conv_pw, w_mem_kv=m_w_mem_kv,
              w_out=m_w_out, final_g=m_final_g)
    vs = dict(norm_g=v_norm_g, mem_norm_g=v_mem_norm_g, w_in=v_w_in, b_f=v_b_f, conv_w=v_conv_w, conv_b=v_conv_b,
              conv_ln_g=v_conv_ln_g, conv_ln_b=v_conv_ln_b, w_conv_pw=v_w_conv_pw, w_mem_kv=v_w_mem_kv,
              w_out=v_w_out, final_g=v_final_g)

    names = ("norm_g", "mem_norm_g", "w_in", "b_f", "conv_w", "conv_b", "conv_ln_g", "conv_ln_b", "w_conv_pw",
             "w_mem_kv", "w_out", "final_g")
    deltas, new_m, new_v = {}, {}, {}
    for k in names:
        shape = weights[k].shape
        if k == "w_in":
            two_d = lambda a: a.reshape(shape[-2], shape[-1]).T
            back = lambda a: a.T.reshape(shape)
        else:
            two_d = lambda a: a.reshape(-1, shape[-1])
            back = lambda a: a.reshape(shape)
        d, nm, nv = _adamw(two_d(weights[k]), two_d(grads[k]), two_d(ms[k]), two_d(vs[k]), name=f"adamw_{k}")
        deltas[k], new_m[k], new_v[k] = back(d), back(nm), back(nv)

    return (loss, grad_x[None], *[grads[k] for k in names], *[deltas[k] for k in names],
            *[new_m[k] for k in names], *[new_v[k] for k in names])
```

```python
import functools

import jax
import jax.numpy as jnp
from jax import lax
from jax.experimental import pallas as pl
from jax.experimental.pallas import tpu as pltpu

F32 = jnp.float32
BF16 = jnp.bfloat16
MESH = pl.DeviceIdType.MESH

HEAD = 128
N_FOX = 8
N_MEMH = 4
CONV_W = 512
FOX_W = N_FOX * HEAD
MEM_W = N_MEMH * HEAD
MIX_W = CONV_W + FOX_W + MEM_W
CONV_K = 31
CONV_HALO = 32
EPS = 1e-6
NEG = -1e30
LANE = 128
N_CHIPS = 4
N_DEV = 8
VMEM_LIMIT = 56 * 1024 * 1024

ADAM_LR = 0.001
ADAM_B1 = 0.9
ADAM_B2 = 0.999
ADAM_EPS = 1e-08
ADAM_WD = 0.01
ADAM_STEP = 10

NT = (((1,), (1,)), ((), ()))
TN = (((0,), (0,)), ((), ()))
NN = (((1,), (0,)), ((), ()))


def _params(sem=None):
    kw = dict(vmem_limit_bytes=VMEM_LIMIT)
    if sem is not None:
        kw["dimension_semantics"] = sem
    return pltpu.CompilerParams(**kw)


def _sigmoid(v):
    return jax.nn.sigmoid(v)


def _dsilu(v, s):
    return s * (1.0 + v * (1.0 - s))


def _rms_fwd(x, g, *, tm, name, gather=None):
    R, D = x.shape
    steps = R // tm

    def body(x_ref, g_ref, h_ref, *ht_ref):
        xv = x_ref[...]
        r = lax.rsqrt(jnp.mean(xv * xv, axis=-1, keepdims=True) + EPS)
        h = xv * r * g_ref[...]
        h_ref[...] = h.astype(BF16)
        if ht_ref:
            ht_ref[0][...] = jnp.transpose(h).astype(BF16)

    in_specs = [pl.BlockSpec((tm, D), lambda i: (i, 0)), pl.BlockSpec((1, D), lambda i: (0, 0))]
    out_spec = pl.BlockSpec((tm, D), lambda i: (i, 0))
    out_shape = jax.ShapeDtypeStruct((R, D), BF16)
    if gather is None:
        return pl.pallas_call(
            body, name=name, grid=(steps,), in_specs=in_specs, out_specs=out_spec, out_shape=out_shape,
            compiler_params=_params(("parallel",)),
        )(x, g)
    res = _call_with_gather(
        body, gather, lambda: pl.program_id(0) == 0, lambda: pl.program_id(0) == steps - 1,
        n_in=2, n_out=2, in_specs=in_specs, out_specs=[out_spec, pl.BlockSpec((D, tm), lambda i: (0, i))],
        out_shape=[out_shape, jax.ShapeDtypeStruct((D, R), BF16)], scratch_shapes=[],
        name=name, grid=(steps,), compiler_params=_params(("arbitrary",)),
    )(x, g, *gather)
    return res[0], res[1], res[2:]


def _rms_bwd(x, g, dh, dres, *, tm, name):
    R, D = x.shape
    has_res = dres is not None

    def body(*refs):
        if has_res:
            x_ref, g_ref, dh_ref, dres_ref, dx_ref, dg_ref = refs
        else:
            x_ref, g_ref, dh_ref, dx_ref, dg_ref = refs
        i = pl.program_id(0)

        @pl.when(i == 0)
        def _():
            dg_ref[...] = jnp.zeros_like(dg_ref)

        xv = x_ref[...]
        r = lax.rsqrt(jnp.mean(xv * xv, axis=-1, keepdims=True) + EPS)
        n = xv * r
        dh = dh_ref[...]
        dg_ref[...] += jnp.sum(dh * n, axis=0, keepdims=True)
        dn = dh * g_ref[...]
        dx = r * (dn - n * jnp.mean(dn * n, axis=-1, keepdims=True))
        if has_res:
            dx = dx + dres_ref[...]
        dx_ref[...] = dx

    row = pl.BlockSpec((tm, D), lambda i: (i, 0))
    vec = pl.BlockSpec((1, D), lambda i: (0, 0))
    ins = [row, vec, row] + ([row] if has_res else [])
    args = (x, g, dh) + ((dres,) if has_res else ())
    return pl.pallas_call(
        body, name=name, grid=(R // tm,),
        in_specs=ins, out_specs=[row, vec],
        out_shape=[jax.ShapeDtypeStruct((R, D), F32), jax.ShapeDtypeStruct((1, D), F32)],
        compiler_params=_params(("arbitrary",)),
    )(*args)


def _mm(a, b, *, ta=False, tb=False, out_dtype=F32, add=None, tm, tn, tk, n=None, b_off=0, name, gather=None):
    M, K = (a.shape[1], a.shape[0]) if ta else a.shape
    nb = b.shape[0] if tb else b.shape[1]
    n = nb if n is None else n
    tm, tn, tk = min(tm, M), min(tn, n), min(tk, K)
    assert M % tm == 0 and n % tn == 0 and K % tk == 0, (name, M, n, K, tm, tn, tk)
    nk = K // tk
    has_add = add is not None

    def body(*refs):
        if has_add:
            a_ref, b_ref, add_ref, o_ref, acc_ref = refs
        else:
            a_ref, b_ref, o_ref, acc_ref = refs
        k = pl.program_id(2)

        @pl.when(k == 0)
        def _():
            acc_ref[...] = jnp.zeros_like(acc_ref)

        av = a_ref[...].astype(BF16)
        bv = b_ref[...].astype(BF16)
        dims = (((0 if ta else 1,), (1 if tb else 0,)), ((), ()))
        acc_ref[...] += lax.dot_general(av, bv, dims, preferred_element_type=F32)

        @pl.when(k == nk - 1)
        def _():
            r = acc_ref[...]
            if has_add:
                r = r + add_ref[...]
            o_ref[...] = r.astype(out_dtype)

    a_spec = pl.BlockSpec((tk, tm), lambda i, j, k: (k, i)) if ta else pl.BlockSpec((tm, tk), lambda i, j, k: (i, k))
    b_spec = (pl.BlockSpec((tn, tk), lambda i, j, k: (j + b_off, k)) if tb
              else pl.BlockSpec((tk, tn), lambda i, j, k: (k, j + b_off)))
    o_spec = pl.BlockSpec((tm, tn), lambda i, j, k: (i, j))
    ins = [a_spec, b_spec] + ([o_spec] if has_add else [])
    args = (a, b) + ((add,) if has_add else ())
    gi, gj = M // tm, n // tn
    out_shape = jax.ShapeDtypeStruct((M, n), out_dtype)
    if gather is None:
        return pl.pallas_call(
            body, name=name, grid=(gi, gj, nk),
            in_specs=ins, out_specs=o_spec, out_shape=out_shape,
            scratch_shapes=[pltpu.VMEM((tm, tn), F32)],
            compiler_params=_params(("parallel", "parallel", "arbitrary")),
        )(*args)
    at = lambda i, j, k: (pl.program_id(0) == i) & (pl.program_id(1) == j) & (pl.program_id(2) == k)
    res = _call_with_gather(
        body, gather, lambda: at(0, 0, 0), lambda: at(gi - 1, gj - 1, nk - 1),
        n_in=len(ins), n_out=1, in_specs=ins, out_specs=[o_spec], out_shape=[out_shape],
        scratch_shapes=[pltpu.VMEM((tm, tn), F32)],
        name=name, grid=(gi, gj, nk), compiler_params=_params(("arbitrary", "arbitrary", "arbitrary")),
    )(*args, *gather)
    return res[0], res[1:]


SUB = 8


def _shifted_copies(es_ref, rows):
    e = es_ref[0]
    for b in range(1, SUB):
        es_ref[b] = pltpu.roll(e, shift=rows - b, axis=0)


def _window(es_ref, offset, tt):
    b = offset % SUB
    return es_ref[b, offset - b:offset - b + tt, :]


def _conv_taps(w_ref, es_ref, offsets, tt):
    acc = w_ref[0:1, :] * _window(es_ref, offsets[0], tt)
    for k in range(1, CONV_K):
        acc = acc + w_ref[k:k + 1, :] * _window(es_ref, offsets[k], tt)
    return acc


def _conv_fwd(pc, cw, cb, lng, lnb, wpw, *, tt, name):
    S = pc.shape[0]
    C = CONV_W
    lead = CONV_HALO - (CONV_K - 1)
    rows = tt + CONV_HALO

    def body(a_ref, b_ref, gc_ref, cw_ref, cb_ref, lng_ref, lnb_ref, wpw_ref, y_ref, yt_ref, u1_ref, es_ref):
        i = pl.program_id(0)

        @pl.when(i == 0)
        def _():
            es_ref[0, 0:CONV_HALO, :] = jnp.zeros((CONV_HALO, C), F32)

        @pl.when(i > 0)
        def _():
            es_ref[0, 0:CONV_HALO, :] = es_ref[0, tt:tt + CONV_HALO, :]

        es_ref[0, CONV_HALO:CONV_HALO + tt, :] = a_ref[...] * _sigmoid(b_ref[...])
        _shifted_copies(es_ref, rows)
        u1 = _conv_taps(cw_ref, es_ref, [lead + k for k in range(CONV_K)], tt) + cb_ref[...]
        u1_ref[...] = u1
        mu = jnp.mean(u1, axis=-1, keepdims=True)
        xc = u1 - mu
        rstd = lax.rsqrt(jnp.mean(xc * xc, axis=-1, keepdims=True) + EPS)
        u2 = xc * rstd * lng_ref[...] + lnb_ref[...]
        u3 = u2 * _sigmoid(u2)
        z = jnp.dot(u3.astype(BF16), wpw_ref[...], preferred_element_type=F32)
        gc = gc_ref[...]
        yv = z * gc * _sigmoid(gc)
        y_ref[...] = yv.astype(BF16)
        yt_ref[...] = jnp.transpose(yv).astype(BF16)

    col = lambda c: pl.BlockSpec((tt, C), lambda i, c=c: (i, c))
    vec = pl.BlockSpec((1, C), lambda i: (0, 0))
    return pl.pallas_call(
        body, name=name, grid=(S // tt,),
        in_specs=[col(0), col(1), col(2), pl.BlockSpec((CONV_HALO, C), lambda i: (0, 0)), vec, vec, vec,
                  pl.BlockSpec((C, C), lambda i: (0, 0))],
        out_specs=[pl.BlockSpec((tt, C), lambda i: (i, 0)), pl.BlockSpec((C, tt), lambda i: (0, i)),
                   pl.BlockSpec((tt, C), lambda i: (i, 0))],
        out_shape=[jax.ShapeDtypeStruct((S, MIX_W), BF16), jax.ShapeDtypeStruct((MIX_W, S), BF16),
                   jax.ShapeDtypeStruct((S, C), F32)],
        scratch_shapes=[pltpu.VMEM((SUB, rows, C), F32)],
        compiler_params=_params(("arbitrary",)),
    )(pc, pc, pc, cw, cb, lng, lnb, wpw)


def _conv_bwd(pc, u1s, dy, cw, lng, lnb, wpw, *, tt, name):
    S = pc.shape[0]
    C = CONV_W
    nt = S // tt
    hb = tt // CONV_HALO
    lead = CONV_HALO - (CONV_K - 1)
    rows = tt + CONV_HALO

    def body(a_ref, b_ref, gc_ref, ah_ref, bh_ref, u1_ref, dy_ref, cw_ref, lng_ref, lnb_ref, wpw_ref,
             dpc_ref, dwpw_ref, dcw_ref, dsm_ref, eu_ref, ed_ref, dcw8_ref):
        i = pl.program_id(0)
        ti = nt - 1 - i

        @pl.when(i == 0)
        def _():
            ed_ref[0, tt:tt + CONV_HALO, :] = jnp.zeros((CONV_HALO, C), F32)
            dwpw_ref[...] = jnp.zeros_like(dwpw_ref)
            dcw8_ref[...] = jnp.zeros_like(dcw8_ref)
            dsm_ref[...] = jnp.zeros_like(dsm_ref)

        @pl.when(i > 0)
        def _():
            ed_ref[0, tt:tt + CONV_HALO, :] = ed_ref[0, 0:CONV_HALO, :]

        keep = jnp.where(ti > 0, 1.0, 0.0).astype(F32)
        eu_ref[0, 0:CONV_HALO, :] = ah_ref[...] * _sigmoid(bh_ref[...]) * keep
        a = a_ref[...]
        sb = _sigmoid(b_ref[...])
        eu_ref[0, CONV_HALO:CONV_HALO + tt, :] = a * sb
        _shifted_copies(eu_ref, rows)

        u1 = u1_ref[...]
        mu = jnp.mean(u1, axis=-1, keepdims=True)
        xc = u1 - mu
        rstd = lax.rsqrt(jnp.mean(xc * xc, axis=-1, keepdims=True) + EPS)
        nhat = xc * rstd
        g = lng_ref[...]
        u2 = nhat * g + lnb_ref[...]
        s2 = _sigmoid(u2)
        u3 = (u2 * s2).astype(BF16)
        z = jnp.dot(u3, wpw_ref[...], preferred_element_type=F32)

        gc = gc_ref[...]
        sg = _sigmoid(gc)
        dyv = dy_ref[...]
        dz = (dyv * gc * sg).astype(BF16)
        dpc_ref[:, 2 * C:3 * C] = (dyv * z * _dsilu(gc, sg)).astype(BF16)

        du3 = lax.dot_general(dz, wpw_ref[...], NT, preferred_element_type=F32)
        dwpw_ref[...] += lax.dot_general(u3, dz, TN, preferred_element_type=F32)
        du2 = du3 * _dsilu(u2, s2)
        dsm_ref[1:2, :] += jnp.sum(du2 * nhat, axis=0, keepdims=True)
        dsm_ref[2:3, :] += jnp.sum(du2, axis=0, keepdims=True)
        dn = du2 * g
        du1 = rstd * (dn - jnp.mean(dn, axis=-1, keepdims=True)
                      - nhat * jnp.mean(dn * nhat, axis=-1, keepdims=True))
        dsm_ref[0:1, :] += jnp.sum(du1, axis=0, keepdims=True)
        ed_ref[0, 0:tt, :] = du1
        _shifted_copies(ed_ref, rows)

        du0 = _conv_taps(cw_ref, ed_ref, [CONV_K - 1 - k for k in range(CONV_K)], tt)
        for k in range(CONV_K):
            prod = du1 * _window(eu_ref, lead + k, tt)
            part = prod[0:SUB]
            for r in range(1, tt // SUB):
                part = part + prod[r * SUB:(r + 1) * SUB]
            dcw8_ref[k * SUB:(k + 1) * SUB, :] += part

        dpc_ref[:, 0:C] = (du0 * sb).astype(BF16)
        dpc_ref[:, C:2 * C] = (du0 * a * sb * (1.0 - sb)).astype(BF16)

        @pl.when(i == nt - 1)
        def _():
            dcw_ref[...] = jnp.zeros_like(dcw_ref)
            for k in range(CONV_K):
                dcw_ref[k:k + 1, :] = jnp.sum(dcw8_ref[k * SUB:(k + 1) * SUB, :], axis=0, keepdims=True)

    col = lambda c: pl.BlockSpec((tt, C), lambda i, c=c: (nt - 1 - i, c))
    halo = lambda c: pl.BlockSpec((CONV_HALO, C), lambda i, c=c: (jnp.maximum((nt - 1 - i) * hb - 1, 0), c))
    vec = pl.BlockSpec((1, C), lambda i: (0, 0))
    fixed = lambda r: pl.BlockSpec((r, C), lambda i: (0, 0))
    return pl.pallas_call(
        body, name=name, grid=(nt,),
        in_specs=[col(0), col(1), col(2), halo(0), halo(1),
                  pl.BlockSpec((tt, C), lambda i: (nt - 1 - i, 0)),
                  pl.BlockSpec((tt, C), lambda i: (nt - 1 - i, 0)),
                  fixed(CONV_HALO), vec, vec, fixed(C)],
        out_specs=[pl.BlockSpec((tt, 3 * C), lambda i: (nt - 1 - i, 0)), fixed(C), fixed(CONV_HALO), fixed(8)],
        out_shape=[jax.ShapeDtypeStruct((S, 3 * C), BF16), jax.ShapeDtypeStruct((C, C), F32),
                   jax.ShapeDtypeStruct((CONV_HALO, C), F32), jax.ShapeDtypeStruct((8, C), F32)],
        scratch_shapes=[pltpu.VMEM((SUB, rows, C), F32), pltpu.VMEM((SUB, rows, C), F32),
                        pltpu.VMEM((CONV_HALO * SUB, C), F32)],
        compiler_params=_params(("arbitrary",)),
    )(pc, pc, pc, pc, pc, u1s, dy, cw, lng, lnb, wpw)


def _tri(n, lower):
    r = lax.broadcasted_iota(jnp.int32, (n, n), 0)
    c = lax.broadcasted_iota(jnp.int32, (n, n), 1)
    return jnp.where((r >= c) if lower else (r <= c), 1.0, 0.0).astype(F32)


def _fox_cumsum(pfl, bf, *, tc, name):
    S = pfl.shape[0]

    def body(fl_ref, bf_ref, qa_ref, ka_ref, carry_ref):
        i = pl.program_id(0)

        @pl.when(i == 0)
        def _():
            carry_ref[...] = jnp.zeros_like(carry_ref)

        z = fl_ref[...] + bf_ref[...]
        logf = jnp.minimum(z, 0.0) - jnp.log1p(jnp.exp(-jnp.abs(z)))
        c = jnp.dot(_tri(tc, True), logf, precision=lax.Precision.HIGHEST,
                    preferred_element_type=F32) + carry_ref[0:1, :]
        carry_ref[0:1, :] = c[tc - 1:tc, :]

        cs = c * (HEAD ** 0.5)
        hi = cs.astype(BF16).astype(F32)
        r1 = cs - hi
        mid = r1.astype(BF16).astype(F32)
        lo = r1 - mid
        lane = lax.broadcasted_iota(jnp.int32, (tc, LANE), 1)
        is_hi = (lane == 0) | (lane == 3)
        is_mid = (lane == 1) | (lane == 4)
        for h in range(N_FOX):
            col = lambda v: jnp.sum(jnp.where(lane == h, v, 0.0), axis=1, keepdims=True)
            pieces = jnp.where(is_hi, col(hi), jnp.where(is_mid, col(mid), col(lo)))
            qa_ref[h] = jnp.where(lane < 3, pieces, jnp.where(lane < 6, 1.0, 0.0)).astype(BF16)
            ka_ref[h] = jnp.where(lane < 3, 1.0, jnp.where(lane < 6, -pieces, 0.0)).astype(BF16)

    out = pl.BlockSpec((N_FOX, tc, HEAD), lambda i: (0, i, 0))
    return pl.pallas_call(
        body, name=name, grid=(S // tc,),
        in_specs=[pl.BlockSpec((tc, LANE), lambda i: (i, 0)), pl.BlockSpec((1, LANE), lambda i: (0, 0))],
        out_specs=[out, out],
        out_shape=[jax.ShapeDtypeStruct((N_FOX, S, HEAD), BF16)] * 2,
        scratch_shapes=[pltpu.VMEM((8, LANE), F32)],
        compiler_params=_params(("arbitrary",)),
    )(pfl, bf)


def _fox_dlogit(dc, pfl, bf, *, tc, name):
    S = pfl.shape[0]
    nt = S // tc

    def body(dc_ref, fl_ref, bf_ref, dfl_ref, dbf_ref, carry_ref):
        i = pl.program_id(0)

        @pl.when(i == 0)
        def _():
            carry_ref[...] = jnp.zeros_like(carry_ref)
            dbf_ref[...] = jnp.zeros_like(dbf_ref)

        dlogf = jnp.dot(_tri(tc, False), dc_ref[...], precision=lax.Precision.HIGHEST,
                        preferred_element_type=F32) + carry_ref[0:1, :]
        carry_ref[0:1, :] = dlogf[0:1, :]
        dz = dlogf * _sigmoid(-(fl_ref[...] + bf_ref[...]))
        dfl_ref[...] = dz.astype(BF16)
        dbf_ref[0:1, :] += jnp.sum(dz, axis=0, keepdims=True)

    rev = pl.BlockSpec((tc, LANE), lambda i: (nt - 1 - i, 0))
    return pl.pallas_call(
        body, name=name, grid=(nt,),
        in_specs=[rev, rev, pl.BlockSpec((1, LANE), lambda i: (0, 0))],
        out_specs=[rev, pl.BlockSpec((8, LANE), lambda i: (0, 0))],
        out_shape=[jax.ShapeDtypeStruct((S, LANE), BF16), jax.ShapeDtypeStruct((8, LANE), F32)],
        scratch_shapes=[pltpu.VMEM((8, LANE), F32)],
        compiler_params=_params(("arbitrary",)),
    )(dc, pfl, bf)


def _as_row(col):
    return jnp.transpose(jnp.broadcast_to(col, (col.shape[0], LANE)))[0:1, :]


def _causal_part(rows, cols, row0, col0, q_is_row=True):
    r = lax.broadcasted_iota(jnp.int32, (rows, cols), 0) + row0
    c = lax.broadcasted_iota(jnp.int32, (rows, cols), 1) + col0
    return (r >= c) if q_is_row else (c >= r)


LOG2E = 1.4426950408889634
FOX_SCALE2 = (HEAD ** -0.5) * LOG2E


def _fox_fwd(qkv, pg, qa, ka, y_all, yt_all, *, tq, nsub, name):
    S = qkv.shape[0]
    nq = S // tq
    rs = tq // nsub
    ybase = CONV_W // HEAD

    def body(q_ref, qa_ref, k_ref, ka_ref, v_ref, g_ref, yin_ref, ytin_ref, y_ref, yt_ref, o_ref, m_ref, linv_ref):
        i = pl.program_id(1)
        q = jnp.concatenate([q_ref[...], qa_ref[...]], axis=1)
        qs = [q[r * rs:(r + 1) * rs] for r in range(nsub)]
        ones = jnp.where(lax.broadcasted_iota(jnp.int32, (tq, HEAD), 1) == 0, 1.0, 0.0).astype(BF16)

        def blk(j, carry, masked):
            off = pl.multiple_of(j * tq, tq)
            kj = jnp.concatenate([k_ref[pl.ds(off, tq), :], ka_ref[pl.ds(off, tq), :]], axis=1)
            vj = jnp.concatenate([v_ref[pl.ds(off, tq), :], ones], axis=1)
            out = []
            for r in range(nsub):
                m, acc = carry[r]
                nk = (r + 1) * rs if masked else tq
                s = lax.dot_general(qs[r], kj[:nk], NT, preferred_element_type=F32) * FOX_SCALE2
                if masked:
                    s = jnp.where(_causal_part(rs, nk, r * rs, 0), s, NEG)
                m_new = jnp.maximum(m, jnp.ceil(jnp.max(s, axis=-1, keepdims=True)))
                pb = jnp.exp2(s - m_new).astype(BF16)
                acc = jnp.exp2(m - m_new) * acc + jnp.dot(pb, vj[:nk], preferred_element_type=F32)
                out.append((m_new, acc))
            return tuple(out)

        init = tuple((jnp.full((rs, 1), NEG, F32), jnp.zeros((rs, 2 * HEAD), F32)) for _ in range(nsub))
        carry = lax.fori_loop(0, i, lambda j, c: blk(j, c, False), init)
        carry = blk(i, carry, True)
        m = jnp.concatenate([c[0] for c in carry], axis=0)
        acc = jnp.concatenate([c[1] for c in carry], axis=0)
        linv = 1.0 / acc[:, HEAD:HEAD + 1]
        o = acc[:, :HEAD] * linv
        g = g_ref[...]
        yv = o * g * _sigmoid(g)
        y_ref[...] = yv.astype(BF16)
        yt_ref[...] = jnp.transpose(yv).astype(BF16)
        o_ref[...] = o
        m_ref[...] = _as_row(m)
        linv_ref[...] = _as_row(linv)

    tile = lambda base: pl.BlockSpec((tq, HEAD), lambda h, i, base=base: (i, base + h))
    full = lambda base: pl.BlockSpec((S, HEAD), lambda h, i, base=base: (0, base + h))
    rowv = pl.BlockSpec((None, None, 1, tq), lambda h, i: (h, i, 0, 0))
    stat = jax.ShapeDtypeStruct((N_FOX, nq, 1, tq), F32)
    return pl.pallas_call(
        body, name=name, grid=(N_FOX, nq),
        in_specs=[tile(0), pl.BlockSpec((None, tq, HEAD), lambda h, i: (h, i, 0)),
                  full(N_FOX), pl.BlockSpec((None, S, HEAD), lambda h, i: (h, 0, 0)),
                  full(2 * N_FOX), tile(0), ANY, ANY],
        out_specs=[tile(ybase), pl.BlockSpec((HEAD, tq), lambda h, i: (ybase + h, i)), tile(0), rowv, rowv],
        out_shape=[jax.ShapeDtypeStruct(y_all.shape, BF16), jax.ShapeDtypeStruct(yt_all.shape, BF16),
                   jax.ShapeDtypeStruct((S, FOX_W), F32), stat, stat],
        input_output_aliases={6: 0, 7: 1},
        compiler_params=_params(("parallel", "arbitrary")),
    )(qkv, qa, qkv, ka, qkv, pg, y_all, yt_all)


def _fox_bwd_prep(dy, o, pg, *, tq, name):
    S = o.shape[0]
    base = CONV_W // HEAD

    def body(dy_ref, o_ref, g_ref, do_ref, dg_ref, dl_ref):
        g = g_ref[...]
        sg = _sigmoid(g)
        dyv = dy_ref[...]
        ov = o_ref[...]
        dob = (dyv * g * sg).astype(BF16)
        do_ref[...] = dob
        dg_ref[...] = (dyv * ov * _dsilu(g, sg)).astype(BF16)
        dl_ref[...] = _as_row(jnp.sum(dob.astype(F32) * ov, axis=-1, keepdims=True))

    tile = lambda b: pl.BlockSpec((tq, HEAD), lambda h, i, b=b: (i, b + h))
    return pl.pallas_call(
        body, name=name, grid=(N_FOX, S // tq),
        in_specs=[tile(base), tile(0), tile(0)],
        out_specs=[tile(0), tile(0), pl.BlockSpec((None, None, 1, tq), lambda h, i: (h, i, 0, 0))],
        out_shape=[jax.ShapeDtypeStruct((S, FOX_W), BF16), jax.ShapeDtypeStruct((S, FOX_W), BF16),
                   jax.ShapeDtypeStruct((N_FOX, S // tq, 1, tq), F32)],
        compiler_params=_params(("parallel", "parallel")),
    )(dy, o, pg)


def _fox_bwd(qkv, do, qa, ka, m_row, linv_row, delta_row, *, tq, nsub, name):
    S = qkv.shape[0]
    nq = S // tq
    cs = tq // nsub
    scale = HEAD ** -0.5

    def body(k_ref, ka_ref, v_ref, q_ref, qa_ref, do_ref, mr_ref, lir_ref, dlr_ref, dq_ref, dk_ref, dv_ref, dc_ref):
        j = pl.program_id(1)

        @pl.when(j == 0)
        def _():
            dq_ref[...] = jnp.zeros_like(dq_ref)

        kj = k_ref[...]
        kja = jnp.concatenate([kj, ka_ref[...]], axis=1)
        vj = v_ref[...]

        def blk(i, carry, masked):
            dk, dv, dc = carry
            m_i = mr_ref[i]
            linv_i = lir_ref[i]
            dl_i = dlr_ref[i]
            for c in range(nsub):
                off = pl.multiple_of(i * tq + c * cs, cs)
                cols = slice(c * cs, (c + 1) * cs)
                qi = q_ref[pl.ds(off, cs), :]
                qia = jnp.concatenate([qi, qa_ref[pl.ds(off, cs), :]], axis=1)
                doi = do_ref[pl.ds(off, cs), :]
                nk = (c + 1) * cs if masked else tq
                st = lax.dot_general(kja[:nk], qia, NT, preferred_element_type=F32) * FOX_SCALE2
                if masked:
                    st = jnp.where(_causal_part(nk, cs, 0, c * cs, q_is_row=False), st, NEG)
                grow = lambda u, nk=nk: u if nk == tq else jnp.concatenate(
                    [u, jnp.zeros((tq - nk, u.shape[1]), F32)], axis=0)
                pt = jnp.exp2(st - m_i[:, cols]).astype(BF16).astype(F32) * linv_i[:, cols]
                dv = dv + grow(jnp.dot(pt.astype(BF16), doi, preferred_element_type=F32))
                dpt = lax.dot_general(vj[:nk], doi, NT, preferred_element_type=F32)
                dst = pt * (dpt - dl_i[:, cols])
                dsb = dst.astype(BF16)
                dk = dk + grow(jnp.dot(dsb, qi, preferred_element_type=F32))
                dq_ref[pl.ds(off, cs), :] += lax.dot_general(dsb, kj[:nk], TN, preferred_element_type=F32) * scale
                dc = dc - grow(jnp.sum(dst, axis=-1, keepdims=True))
            return dk, dv, dc

        init = (jnp.zeros((tq, HEAD), F32), jnp.zeros((tq, HEAD), F32), jnp.zeros((tq, 1), F32))
        carry = blk(j, init, True)
        dk, dv, dc = lax.fori_loop(j + 1, nq, lambda i, c: blk(i, c, False), carry)
        dk_ref[...] = (dk * scale).astype(BF16)
        dv_ref[...] = dv.astype(BF16)
        dc_ref[...] = _as_row(dc)

    tile = lambda base: pl.BlockSpec((tq, HEAD), lambda h, j, base=base: (j, base + h))
    full = lambda base: pl.BlockSpec((S, HEAD), lambda h, j, base=base: (0, base + h))
    atile = pl.BlockSpec((None, tq, HEAD), lambda h, j: (h, j, 0))
    afull = pl.BlockSpec((None, S, HEAD), lambda h, j: (h, 0, 0))
    rowt = pl.BlockSpec((None, None, 1, tq), lambda h, j: (h, j, 0, 0))
    rowv = pl.BlockSpec((None, nq, 1, tq), lambda h, j: (h, 0, 0, 0))
    return pl.pallas_call(
        body, name=name, grid=(N_FOX, nq),
        in_specs=[tile(N_FOX), atile, tile(2 * N_FOX), full(0), afull, full(0), rowv, rowv, rowv],
        out_specs=[full(0), tile(0), tile(0), rowt],
        out_shape=[jax.ShapeDtypeStruct((S, FOX_W), F32), jax.ShapeDtypeStruct((S, FOX_W), BF16),
                   jax.ShapeDtypeStruct((S, FOX_W), BF16), jax.ShapeDtypeStruct((N_FOX, nq, 1, tq), F32)],
        compiler_params=_params(("arbitrary", "arbitrary")),
    )(qkv, ka, qkv, qkv, qa, do, m_row, linv_row, delta_row)


def _mem_heads(mq, mkv, h):
    lo = h * HEAD
    qh = mq[:, lo:lo + HEAD].astype(BF16)
    kh = mkv[:, lo:lo + HEAD].astype(BF16)
    vh = mkv[:, MEM_W + lo:MEM_W + lo + HEAD].astype(BF16)
    return qh, kh, vh


def _mem_softmax(qh, kh):
    s = lax.dot_general(qh, kh, NT, preferred_element_type=F32) * (HEAD ** -0.5)
    e = jnp.exp(s - jnp.max(s, axis=-1, keepdims=True))
    return e / jnp.sum(e, axis=-1, keepdims=True)


def _mem_fwd(pg, mkv, y_all, yt_all, *, tq, name):
    S = pg.shape[0]
    M = mkv.shape[0]
    qb = FOX_W // MEM_W
    yb = (CONV_W + FOX_W) // MEM_W

    def body(mq_ref, g_ref, mkv_ref, yin_ref, ytin_ref, y_ref, yt_ref):
        mq = mq_ref[...]
        mkvv = mkv_ref[...]
        for h in range(N_MEMH):
            qh, kh, vh = _mem_heads(mq, mkvv, h)
            p = _mem_softmax(qh, kh)
            o = jnp.dot(p.astype(BF16), vh, preferred_element_type=F32)
            g = g_ref[:, h * HEAD:(h + 1) * HEAD]
            yv = o * g * _sigmoid(g)
            y_ref[:, h * HEAD:(h + 1) * HEAD] = yv.astype(BF16)
            yt_ref[h * HEAD:(h + 1) * HEAD, :] = jnp.transpose(yv).astype(BF16)

    return pl.pallas_call(
        body, name=name, grid=(S // tq,),
        in_specs=[pl.BlockSpec((tq, MEM_W), lambda i: (i, qb)), pl.BlockSpec((tq, MEM_W), lambda i: (i, qb + 1)),
                  pl.BlockSpec((M, 2 * MEM_W), lambda i: (0, 0)), ANY, ANY],
        out_specs=[pl.BlockSpec((tq, MEM_W), lambda i: (i, yb)), pl.BlockSpec((MEM_W, tq), lambda i: (yb, i))],
        out_shape=[jax.ShapeDtypeStruct(y_all.shape, BF16), jax.ShapeDtypeStruct(yt_all.shape, BF16)],
        input_output_aliases={3: 0, 4: 1},
        compiler_params=_params(("parallel",)),
    )(pg, pg, mkv, y_all, yt_all)


def _mem_bwd(pg, mkv, dy, *, tq, name):
    S = pg.shape[0]
    M = mkv.shape[0]
    qb = FOX_W // MEM_W
    yb = (CONV_W + FOX_W) // MEM_W
    scale = HEAD ** -0.5

    def body(mq_ref, g_ref, mkv_ref, dy_ref, dmq_ref, dg_ref, dmkv_ref):
        i = pl.program_id(0)

        @pl.when(i == 0)
        def _():
            dmkv_ref[...] = jnp.zeros_like(dmkv_ref)

        mq = mq_ref[...]
        mkvv = mkv_ref[...]
        for h in range(N_MEMH):
            lo = h * HEAD
            qh, kh, vh = _mem_heads(mq, mkvv, h)
            p = _mem_softmax(qh, kh)
            o = jnp.dot(p.astype(BF16), vh, preferred_element_type=F32)
            g = g_ref[:, lo:lo + HEAD]
            sg = _sigmoid(g)
            dyh = dy_ref[:, lo:lo + HEAD]
            do = dyh * g * sg
            dg_ref[:, lo:lo + HEAD] = (dyh * o * _dsilu(g, sg)).astype(BF16)
            dob = do.astype(BF16)
            dp = lax.dot_general(dob, vh, NT, preferred_element_type=F32)
            ds = p * (dp - jnp.sum(do * o, axis=-1, keepdims=True))
            dsb = ds.astype(BF16)
            dmq_ref[:, lo:lo + HEAD] = (jnp.dot(dsb, kh, preferred_element_type=F32) * scale).astype(BF16)
            dmkv_ref[:, lo:lo + HEAD] += lax.dot_general(dsb, qh, TN, preferred_element_type=F32) * scale
            dmkv_ref[:, MEM_W + lo:MEM_W + lo + HEAD] += lax.dot_general(
                p.astype(BF16), dob, TN, preferred_element_type=F32)

    return pl.pallas_call(
        body, name=name, grid=(S // tq,),
        in_specs=[pl.BlockSpec((tq, MEM_W), lambda i: (i, qb)), pl.BlockSpec((tq, MEM_W), lambda i: (i, qb + 1)),
                  pl.BlockSpec((M, 2 * MEM_W), lambda i: (0, 0)), pl.BlockSpec((tq, MEM_W), lambda i: (i, yb))],
        out_specs=[pl.BlockSpec((tq, MEM_W), lambda i: (i, 0)), pl.BlockSpec((tq, MEM_W), lambda i: (i, 0)),
                   pl.BlockSpec((M, 2 * MEM_W), lambda i: (0, 0))],
        out_shape=[jax.ShapeDtypeStruct((S, MEM_W), BF16), jax.ShapeDtypeStruct((S, MEM_W), BF16),
                   jax.ShapeDtypeStruct((M, 2 * MEM_W), F32)],
        compiler_params=_params(("arbitrary",)),
    )(pg, pg, mkv, dy)


def _final(x2, target, fg, *, tm, name):
    S, D = x2.shape

    def body(x_ref, t_ref, g_ref, dx_ref, dxb_ref, dg_ref, ls_ref):
        i = pl.program_id(0)

        @pl.when(i == 0)
        def _():
            dg_ref[...] = jnp.zeros_like(dg_ref)
            ls_ref[...] = jnp.zeros_like(ls_ref)

        xv = x_ref[...]
        r = lax.rsqrt(jnp.mean(xv * xv, axis=-1, keepdims=True) + EPS)
        n = xv * r
        g = g_ref[...]
        diff = n * g - t_ref[...]
        ls_ref[...] += jnp.sum(diff * diff)
        dout = diff * (1.0 / D)
        dg_ref[...] += jnp.sum(dout * n, axis=0, keepdims=True)
        dn = dout * g
        dx = r * (dn - n * jnp.mean(dn * n, axis=-1, keepdims=True))
        dx_ref[...] = dx
        dxb_ref[...] = dx.astype(BF16)

    row = pl.BlockSpec((tm, D), lambda i: (i, 0))
    vec = pl.BlockSpec((1, D), lambda i: (0, 0))
    return pl.pallas_call(
        body, name=name, grid=(S // tm,),
        in_specs=[row, row, vec],
        out_specs=[row, row, vec, pl.BlockSpec((8, LANE), lambda i: (0, 0))],
        out_shape=[jax.ShapeDtypeStruct((S, D), F32), jax.ShapeDtypeStruct((S, D), BF16),
                   jax.ShapeDtypeStruct((1, D), F32), jax.ShapeDtypeStruct((8, LANE), F32)],
        compiler_params=_params(("arbitrary",)),
    )(x2, target, fg)


def _adamw(w, g, m, v, *, name):
    R, C = w.shape
    tr, tc = R, C
    for cand in (256, 128, 64, 32, 16, 8):
        if R % cand == 0 and R > cand:
            tr = cand
            break
    if tr == R and R > 256 and C % 256 == 0:
        tc = 256
    c1 = 1.0 - ADAM_B1 ** ADAM_STEP
    c2 = 1.0 - ADAM_B2 ** ADAM_STEP

    def body(w_ref, g_ref, m_ref, v_ref, d_ref, nm_ref, nv_ref):
        gv = g_ref[...]
        nm = ADAM_B1 * m_ref[...] + (1.0 - ADAM_B1) * gv
        nv = ADAM_B2 * v_ref[...] + (1.0 - ADAM_B2) * (gv * gv)
        nm_ref[...] = nm
        nv_ref[...] = nv
        d_ref[...] = -ADAM_LR * ((nm / c1) / (jnp.sqrt(nv / c2) + ADAM_EPS) + ADAM_WD * w_ref[...])

    spec = pl.BlockSpec((tr, tc), lambda i, j: (i, j))
    shp = jax.ShapeDtypeStruct((R, C), F32)
    return pl.pallas_call(
        body, name=name, grid=(R // tr, C // tc),
        in_specs=[spec] * 4, out_specs=[spec] * 3, out_shape=[shp] * 3,
        compiler_params=_params(("parallel", "parallel")),
    )(w, g, m, v)


def _sum4(q, own, me, *, name):
    _, R, C = q.shape
    tr = R
    for cand in (256, 128, 64, 32, 16, 8):
        if R % cand == 0 and R > cand:
            tr = cand
            break

    def body(me_ref, own_ref, q1_ref, q2_ref, q3_ref, o_ref):
        f = lambda r: r[...].astype(F32)
        o_ref[...] = ((f(own_ref) + f(q1_ref)) + f(q2_ref)) + f(q3_ref)

    blk = lambda d: pl.BlockSpec((None, tr, C), lambda i, me_ref, d=d: (me_ref[0] ^ d, i, 0))
    return pl.pallas_call(
        body, name=name,
        grid_spec=pltpu.PrefetchScalarGridSpec(
            num_scalar_prefetch=1, grid=(R // tr,),
            in_specs=[blk(0), blk(1), blk(2), blk(3)],
            out_specs=pl.BlockSpec((tr, C), lambda i, me_ref: (i, 0))),
        out_shape=jax.ShapeDtypeStruct((R, C), F32),
        compiler_params=_params(("parallel",)),
    )(jnp.reshape(me, (1,)).astype(jnp.int32), own, q, q, q)


def _add_sibling(g, got, c, *, name):
    K, _, R, C = g.shape
    tr = R
    for cand in (256, 128, 64, 32, 16, 8):
        if R % cand == 0 and R > cand:
            tr = cand
            break

    def body(c_ref, a_ref, b_ref, o_ref):
        o_ref[...] = (a_ref[...] + b_ref[...]).astype(BF16)

    spec = pl.BlockSpec((None, tr, C), lambda k, i, c_ref: (k, i, 0))
    return pl.pallas_call(
        body, name=name,
        grid_spec=pltpu.PrefetchScalarGridSpec(
            num_scalar_prefetch=1, grid=(K, R // tr),
            in_specs=[pl.BlockSpec((None, None, tr, C), lambda k, i, c_ref: (k, c_ref[0], i, 0)), spec],
            out_specs=spec),
        out_shape=jax.ShapeDtypeStruct((K, R, C), BF16),
        compiler_params=_params(("parallel", "parallel")),
    )(jnp.reshape(c, (1,)).astype(jnp.int32), g, got)


ANY = pl.BlockSpec(memory_space=pl.ANY)


def _other_chips(x, y):
    return [(d, 1 - x if d & 2 else x, 1 - y if d & 1 else y) for d in (1, 2, 3)]


def _gather_ici_copies(src_refs, out_refs, ici_send, ici_recv):
    x, y, c = lax.axis_index("x"), lax.axis_index("y"), lax.axis_index("c")
    me = 2 * x + y
    return [pltpu.make_async_remote_copy(
        src_ref=src_refs[t].at[c], dst_ref=out_refs[t].at[me, c],
        send_sem=ici_send.at[3 * t + d - 1], recv_sem=ici_recv.at[3 * t + d - 1],
        device_id=(tx, ty, c), device_id_type=MESH)
        for t in range(len(src_refs)) for d, tx, ty in _other_chips(x, y)]


def _gather_finish(src_refs, out_refs, ici_send, ici_recv, d2d_send, d2d_recv):
    n = len(src_refs)
    x, y, c = lax.axis_index("x"), lax.axis_index("y"), lax.axis_index("c")
    passed = []
    for t in range(n):
        for d, tx, ty in _other_chips(x, y):
            landed = out_refs[t].at[2 * tx + ty, c]
            pltpu.make_async_remote_copy(
                src_ref=landed, dst_ref=landed,
                send_sem=ici_send.at[3 * t + d - 1], recv_sem=ici_recv.at[3 * t + d - 1],
                device_id=(tx, ty, c), device_id_type=MESH).wait_recv()
            cp = pltpu.make_async_remote_copy(
                src_ref=landed, dst_ref=landed,
                send_sem=d2d_send.at[3 * t + d - 1], recv_sem=d2d_recv.at[3 * t + d - 1],
                device_id=(x, y, 1 - c), device_id_type=MESH)
            cp.start()
            passed.append(cp)
    for t in range(n):
        for d, tx, ty in _other_chips(x, y):
            theirs = out_refs[t].at[2 * tx + ty, 1 - c]
            pltpu.make_async_remote_copy(
                src_ref=theirs, dst_ref=theirs,
                send_sem=d2d_send.at[3 * t + d - 1], recv_sem=d2d_recv.at[3 * t + d - 1],
                device_id=(x, y, 1 - c), device_id_type=MESH).wait_recv()
    for cp in _gather_ici_copies(src_refs, out_refs, ici_send, ici_recv) + passed:
        cp.wait_send()


def _call_with_gather(body, shards, first, last, *, n_in, n_out, in_specs, out_specs, out_shape, scratch_shapes,
                      **kw):
    n = len(shards)
    n_scr = len(scratch_shapes)

    def wrapped(*refs):
        ins, srcs = refs[:n_in], refs[n_in:n_in + n]
        outs = refs[n_in + n:n_in + n + n_out]
        gouts = refs[n_in + n + n_out:n_in + 2 * n + n_out]
        scr = refs[n_in + 2 * n + n_out:n_in + 2 * n + n_out + n_scr]
        ici_send, ici_recv, d2d_send, d2d_recv = refs[n_in + 2 * n + n_out + n_scr:]

        @pl.when(first())
        def _():
            for cp in _gather_ici_copies(srcs, gouts, ici_send, ici_recv):
                cp.start()

        body(*ins, *outs, *scr)

        @pl.when(last())
        def _():
            _gather_finish(srcs, gouts, ici_send, ici_recv, d2d_send, d2d_recv)

    sem = pltpu.SemaphoreType.DMA((3 * n,))
    return pl.pallas_call(
        wrapped,
        in_specs=list(in_specs) + [ANY] * n,
        out_specs=list(out_specs) + [ANY] * n,
        out_shape=list(out_shape) + [jax.ShapeDtypeStruct((N_CHIPS,) + s.shape, s.dtype) for s in shards],
        scratch_shapes=list(scratch_shapes) + [sem, sem, sem, sem],
        **kw)


def _scatter_copies(src_refs, out_refs, send_sems, recv_sems):
    n = len(src_refs)
    x, y, c = lax.axis_index("x"), lax.axis_index("y"), lax.axis_index("c")
    me = 2 * x + y
    copies = []
    for t in range(n):
        for d, tx, ty in _other_chips(x, y):
            copies.append(pltpu.make_async_remote_copy(
                src_ref=src_refs[t].at[2 * tx + ty], dst_ref=out_refs[t].at[me],
                send_sem=send_sems.at[3 * t + d - 1], recv_sem=recv_sems.at[3 * t + d - 1],
                device_id=(tx, ty, c), device_id_type=MESH))
    return copies


def _mm_scatter(a, b, a2, b2, srcs, *, tm, tn, tk, name):
    M, K = a.shape
    N = b.shape[1]
    K2 = a2.shape[1]
    tm, tn, tk = min(tm, M), min(tn, N), min(tk, K)
    assert M % tm == 0 and N % tn == 0 and K % tk == 0, (name, M, N, K, tm, tn, tk)
    gi, gj, gk = M // tm, N // tn, K // tk
    n = len(srcs)

    def body(*refs):
        a_ref, b_ref, a2_ref, b2_ref = refs[:4]
        src_refs = refs[4:4 + n]
        o_ref = refs[4 + n]
        out_refs = refs[5 + n:5 + 2 * n]
        acc_ref, send_sems, recv_sems = refs[5 + 2 * n:]
        i, j, k = pl.program_id(0), pl.program_id(1), pl.program_id(2)

        @pl.when((i == 0) & (j == 0) & (k == 0))
        def _():
            for cp in _scatter_copies(src_refs, out_refs, send_sems, recv_sems):
                cp.start()

        @pl.when(k == 0)
        def _():
            acc_ref[...] = jnp.dot(a2_ref[...].astype(BF16), b2_ref[...].astype(BF16), preferred_element_type=F32)

        acc_ref[...] += jnp.dot(a_ref[...].astype(BF16), b_ref[...].astype(BF16), preferred_element_type=F32)

        @pl.when(k == gk - 1)
        def _():
            o_ref[...] = acc_ref[...]

        @pl.when((i == gi - 1) & (j == gj - 1) & (k == gk - 1))
        def _():
            for cp in _scatter_copies(src_refs, out_refs, send_sems, recv_sems):
                cp.wait()

    o_spec = pl.BlockSpec((tm, tn), lambda i, j, k: (i, j))
    sem = pltpu.SemaphoreType.DMA((3 * n,))
    res = pl.pallas_call(
        body, name=name, grid=(gi, gj, gk),
        in_specs=[pl.BlockSpec((tm, tk), lambda i, j, k: (i, k)), pl.BlockSpec((tk, tn), lambda i, j, k: (k, j)),
                  pl.BlockSpec((tm, K2), lambda i, j, k: (i, 0)), pl.BlockSpec((K2, tn), lambda i, j, k: (0, j))]
        + [ANY] * n,
        out_specs=[o_spec] + [ANY] * n,
        out_shape=[jax.ShapeDtypeStruct((M, N), F32)] + [jax.ShapeDtypeStruct(s.shape, s.dtype) for s in srcs],
        scratch_shapes=[pltpu.VMEM((tm, tn), F32), sem, sem],
        compiler_params=_params(("arbitrary", "arbitrary", "arbitrary")),
    )(a, b, a2, b2, *srcs)
    return res[0], res[1:]


def _swap_halves(grads, *, name):
    n = len(grads)

    def body(*refs):
        src_refs, out_refs = refs[:n], refs[n:2 * n]
        send_sems, recv_sems = refs[2 * n:]
        x, y, c = lax.axis_index("x"), lax.axis_index("y"), lax.axis_index("c")
        copies = []
        for t in range(n):
            for k in range(N_CHIPS):
                cp = pltpu.make_async_remote_copy(
                    src_ref=src_refs[t].at[k, 1 - c], dst_ref=out_refs[t].at[k],
                    send_sem=send_sems.at[N_CHIPS * t + k], recv_sem=recv_sems.at[N_CHIPS * t + k],
                    device_id=(x, y, 1 - c), device_id_type=MESH)
                cp.start()
                copies.append(cp)
        for cp in copies:
            cp.wait()

    sem = pltpu.SemaphoreType.DMA((N_CHIPS * n,))
    return pl.pallas_call(
        body, name=name,
        in_specs=[ANY] * n, out_specs=[ANY] * n,
        out_shape=[jax.ShapeDtypeStruct((N_CHIPS,) + g.shape[2:], g.dtype) for g in grads],
        scratch_shapes=[sem, sem],
    )(*grads)


def _sibling_swap(srcs, *, name):
    n = len(srcs)

    def body(*refs):
        src_refs, out_refs = refs[:n], refs[n:2 * n]
        send_sems, recv_sems = refs[2 * n:]
        x, y, c = lax.axis_index("x"), lax.axis_index("y"), lax.axis_index("c")
        copies = []
        for t in range(n):
            cp = pltpu.make_async_remote_copy(
                src_ref=src_refs[t], dst_ref=out_refs[t],
                send_sem=send_sems.at[t], recv_sem=recv_sems.at[t],
                device_id=(x, y, 1 - c), device_id_type=MESH)
            cp.start()
            copies.append(cp)
        for cp in copies:
            cp.wait()

    sem = pltpu.SemaphoreType.DMA((n,))
    return pl.pallas_call(
        body, name=name,
        in_specs=[ANY] * n, out_specs=[ANY] * n,
        out_shape=[jax.ShapeDtypeStruct(s.shape, s.dtype) for s in srcs],
        scratch_shapes=[sem, sem],
    )(*srcs)


def _allreduce_small(v, *, name):
    R, C = v.shape

    def body(v_ref, o_ref, buf_ref, send_sems, recv_sems):
        x, y, c = lax.axis_index("x"), lax.axis_index("y"), lax.axis_index("c")
        me = 4 * x + 2 * y + c
        buf_ref[me] = v_ref[...]
        copies = []
        for d in range(1, N_DEV):
            tx = 1 - x if d & 4 else x
            ty = 1 - y if d & 2 else y
            tc = 1 - c if d & 1 else c
            cp = pltpu.make_async_remote_copy(
                src_ref=v_ref, dst_ref=buf_ref.at[me],
                send_sem=send_sems.at[d - 1], recv_sem=recv_sems.at[d - 1],
                device_id=(tx, ty, tc), device_id_type=MESH)
            cp.start()
            copies.append(cp)
        for cp in copies:
            cp.wait()
        acc = buf_ref[0]
        for k in range(1, N_DEV):
            acc = acc + buf_ref[k]
        o_ref[...] = acc

    return pl.pallas_call(
        body, name=name,
        in_specs=[pl.BlockSpec(memory_space=pltpu.VMEM)],
        out_specs=pl.BlockSpec(memory_space=pltpu.VMEM),
        out_shape=jax.ShapeDtypeStruct((R, C), F32),
        scratch_shapes=[pltpu.VMEM((N_DEV, R, C), F32), pltpu.SemaphoreType.DMA((N_DEV - 1,)),
                        pltpu.SemaphoreType.DMA((N_DEV - 1,))],
    )(v)


_A0, _B0, _GC0 = 0, CONV_W, 2 * CONV_W
_Q0 = 3 * CONV_W
_FL0 = _Q0 + 3 * FOX_W
_FG0 = _FL0 + N_FOX
_MQ0 = _FG0 + FOX_W
_MG0 = _MQ0 + MEM_W
_DIN = _MG0 + MEM_W
_WMAIN = _DIN - N_FOX
_PC_N = 3 * CONV_W
_QKV_N = 3 * FOX_W
_PG_N = FOX_W + 2 * MEM_W


def _rows_pad8(a):
    r = a.shape[0]
    p = (-r) % 8
    return jnp.pad(a, ((0, p), (0, 0))) if p else a


def _step(x, mem, target, norm_g, mem_norm_g, final_g, b_f, conv_b, ln_g, ln_b,
          s_in, s_pw, s_mkv, s_out, s_cw, ci, me):
    S, D = x.shape
    M = mem.shape[0]
    tq = min(512, S)
    tf = min(1024, S)
    tt = min(256, S)
    tm = min(256, S)
    nin = s_in.shape[2]
    bf_pad = jnp.pad(b_f, ((0, 0), (0, LANE - N_FOX)))
    whole = lambda g, s: lax.dynamic_update_slice(g, s[None], (me, 0, 0, 0))

    h, h_t, (g_in,) = _rms_fwd(x, norm_g, tm=tm, name="rms_fwd_gather_w_in", gather=[s_in])
    w_in = jnp.transpose(whole(g_in, s_in).reshape(N_CHIPS, D, nin), (1, 0, 2)).reshape(D, N_CHIPS * nin)
    w_main = jnp.concatenate([w_in[:, :_FL0], w_in[:, _FG0:]], axis=1)
    w_fl = jnp.pad(w_in[:, _FL0:_FG0], ((0, 0), (0, LANE - N_FOX)))
    w_main_t, w_fl_t = w_main.T, w_fl.T

    pc, (g_pw, g_mkv, g_out, g_cw) = _mm(h, w_main, tm=1024, tn=512, tk=2048, n=_PC_N, b_off=0,
                                         name="proj_conv_gather_rest", gather=[s_pw, s_mkv, s_out, s_cw])
    w_pw = whole(g_pw, s_pw).reshape(CONV_W, CONV_W)
    w_mkv = whole(g_mkv, s_mkv).reshape(D, 2 * MEM_W)
    w_out = whole(g_out, s_out).reshape(MIX_W, D)
    w_out_t = w_out.T
    cw_pad = jnp.transpose(whole(g_cw, s_cw).reshape(N_CHIPS, CONV_HALO, CONV_W // N_CHIPS),
                           (1, 0, 2)).reshape(CONV_HALO, CONV_W)

    qkv = _mm(h, w_main, out_dtype=BF16, tm=1024, tn=512, tk=2048, n=_QKV_N, b_off=_PC_N // 512, name="proj_qkv")
    pg = _mm(h, w_main, tm=1024, tn=512, tk=2048, n=_PG_N, b_off=(_PC_N + _QKV_N) // 512, name="proj_gate")
    pfl = _mm(h, w_fl, tm=1024, tn=LANE, tk=2048, name="proj_logit")

    y, y_t, u1s = _conv_fwd(pc, cw_pad, conv_b, ln_g, ln_b, w_pw, tt=tt, name="conv_fwd")

    qa, ka = _fox_cumsum(pfl, bf_pad, tc=tt, name="fox_cumsum")
    y, y_t, o_fox, fox_m, fox_linv = _fox_fwd(qkv, pg, qa, ka, y, y_t, tq=tf, nsub=2, name="fox_fwd")

    hm = _rms_fwd(mem, mem_norm_g, tm=min(256, M), name="rms_mem")
    mkv = _mm(hm, w_mkv, tm=256, tn=512, tk=2048, name="mem_kv")
    y, y_t = _mem_fwd(pg, mkv, y, y_t, tq=tq, name="mem_fwd")

    x2 = _mm(y, w_out, add=x, tm=1024, tn=1024, tk=2048, name="out_proj")
    dx2, dx2_b, dfg, sq = _final(x2, target, final_g.reshape(1, D), tm=min(512, S), name="final")

    dy = _mm(dx2_b, w_out_t, tm=1024, tn=1024, tk=2048, name="d_y")
    dw_out = _mm(y_t, dx2_b, tm=1024, tn=1024, tk=2048, name="d_w_out")

    dpc, dw_pw, dcw, dsm = _conv_bwd(pc, u1s, dy, cw_pad, ln_g, ln_b, w_pw, tt=tt, name="conv_bwd")

    do, dfgate, delta = _fox_bwd_prep(dy, o_fox, pg, tq=tf, name="fox_bwd_prep")
    dq, dk, dv, dc = _fox_bwd(qkv, do, qa, ka, fox_m, fox_linv, delta, tq=tf, nsub=2, name="fox_bwd")
    dc_pad = jnp.pad(dc.reshape(N_FOX, S).T, ((0, 0), (0, LANE - N_FOX)))
    dfl, dbf = _fox_dlogit(dc_pad, pfl, bf_pad, tc=tt, name="fox_dlogit")

    dmq, dmgate, dmkv = _mem_bwd(pg, mkv, dy, tq=tq, name="mem_bwd")
    dw_mkv = _mm(hm, dmkv, ta=True, tm=512, tn=512, tk=256, name="d_w_mkv")
    dhm = _mm(dmkv, w_mkv, tb=True, tm=256, tn=512, tk=1024, name="d_hm")
    _, dmg = _rms_bwd(mem, mem_norm_g, dhm, None, tm=min(256, M), name="rms_mem_bwd")

    dproj = jnp.concatenate([dpc, dq.astype(BF16), dk, dv, dfgate, dmq, dmgate], axis=1)
    dw_main = _mm(h_t, dproj, tm=1024, tn=_WMAIN // 4, tk=2048, name="d_w_main")
    dw_fl = _mm(h_t, dfl, tm=1024, tn=LANE, tk=1024, name="d_w_logit")

    def chip_columns(k):
        lo, hi = k * nin, (k + 1) * nin
        parts = []
        if lo < _FL0:
            parts.append(dw_main[:, lo:min(hi, _FL0)])
        if lo < _FG0 and hi > _FL0:
            parts.append(dw_fl[:, max(lo, _FL0) - _FL0:min(hi, _FG0) - _FL0])
        if hi > _FG0:
            parts.append(dw_main[:, max(lo, _FG0) - N_FOX:hi - N_FOX])
        return parts[0] if len(parts) == 1 else jnp.concatenate(parts, axis=1)

    big = [jnp.stack([chip_columns(k) for k in range(N_CHIPS)]),
           dw_pw.reshape(N_CHIPS, CONV_W // N_CHIPS, CONV_W),
           dw_mkv.reshape(N_CHIPS, D // N_CHIPS, 2 * MEM_W),
           dw_out.reshape(N_CHIPS, MIX_W // N_CHIPS, D)]
    big = [g.reshape(N_CHIPS, 2, g.shape[1] // 2, g.shape[2]) for g in big]
    got = _swap_halves(big, name="grad_swap_halves")
    chip = [_add_sibling(g, o, ci, name=f"grad_add_sibling_{t}") for t, (g, o) in enumerate(zip(big, got))]

    dh, parts = _mm_scatter(dproj, w_main_t, dfl, w_fl_t, chip, tm=1024, tn=1024, tk=_WMAIN // 2,
                            name="d_h_grad_scatter")
    grad_x, dng = _rms_bwd(x, norm_g, dh, dx2, tm=min(512, S), name="rms_bwd")

    red = [_sum4(p, own, me, name=f"grad_sum_chips_{t}") for t, (p, own) in enumerate(zip(parts, chip))]
    other = _sibling_swap(red, name="grad_swap_result")
    full = [jnp.where(ci == 0, jnp.concatenate([r, o], axis=0), jnp.concatenate([o, r], axis=0))
            for r, o in zip(red, other)]

    small = dict(norm_g=dng, mem_norm_g=dmg, final_g=dfg, b_f=dbf[0:1, :], conv_w=dcw,
                 conv_b=dsm[0:1], conv_ln_g=dsm[1:2], conv_ln_b=dsm[2:3])
    return sq[0, 0], grad_x, full, small


_SMALL_ORDER = ("norm_g", "mem_norm_g", "final_g", "b_f", "conv_w", "conv_b", "conv_ln_g", "conv_ln_b")


def _pack_small(small):
    parts, layout = [], []
    row = 0
    for k in _SMALL_ORDER:
        p = _rows_pad8(small[k].reshape(-1, LANE))
        layout.append((k, row, small[k].shape))
        parts.append(p)
        row += p.shape[0]
    return jnp.concatenate(parts, axis=0), layout


def _unpack_small(packed, layout):
    out = {}
    for k, row, shape in layout:
        nrow = (shape[0] * shape[1]) // LANE
        out[k] = packed[row:row + nrow].reshape(shape)
    return out


def kernel(x, mem, norm_g, mem_norm_g, w_in, b_f, conv_w, conv_b, conv_ln_g, conv_ln_b, w_conv_pw, w_mem_kv, w_out, final_g, loss_target, m_norm_g, m_mem_norm_g, m_w_in, m_b_f, m_conv_w, m_conv_b, m_conv_ln_g, m_conv_ln_b, m_w_conv_pw, m_w_mem_kv, m_w_out, m_final_g, v_norm_g, v_mem_norm_g, v_w_in, v_b_f, v_conv_w, v_conv_b, v_conv_ln_g, v_conv_ln_b, v_w_conv_pw, v_w_mem_kv, v_w_out, v_final_g):
    S, D = x.shape[1], x.shape[2]
    xi, yi, ci = lax.axis_index("x"), lax.axis_index("y"), lax.axis_index("c")
    chip = 2 * xi + yi

    halves = lambda a: a.reshape(2, a.shape[0] // 2, a.shape[1])
    cw_shard = jnp.pad(conv_w[0], ((0, CONV_HALO - CONV_K), (0, 0)))
    sq, grad_x, (g_w_in, g_w_pw, g_w_mkv, g_w_out), small = _step(
        x[0], mem[0], loss_target[0], norm_g, mem_norm_g, final_g, b_f, conv_b, conv_ln_g, conv_ln_b,
        halves(w_in[0].astype(BF16)), halves(w_conv_pw[0].astype(BF16)), halves(w_mem_kv[0].astype(BF16)),
        halves(w_out[0].astype(BF16)), halves(cw_shard), ci, chip)

    loss = lax.psum(sq, ("x", "y", "c")) * (0.5 / D)

    packed, layout = _pack_small(small)
    sm = _unpack_small(_allreduce_small(packed, name="small_all_reduce"), layout)
    cshard = CONV_W // N_CHIPS
    g_conv_w = lax.dynamic_slice_in_dim(sm["conv_w"][:CONV_K], chip * cshard, cshard, axis=1)

    grads = dict(
        norm_g=sm["norm_g"], mem_norm_g=sm["mem_norm_g"], w_in=g_w_in[None], b_f=sm["b_f"][:, :N_FOX],
        conv_w=g_conv_w[None], conv_b=sm["conv_b"], conv_ln_g=sm["conv_ln_g"], conv_ln_b=sm["conv_ln_b"],
        w_conv_pw=g_w_pw[None], w_mem_kv=g_w_mkv[None], w_out=g_w_out[None], final_g=sm["final_g"].reshape(D))
    weights = dict(norm_g=norm_g, mem_norm_g=mem_norm_g, w_in=w_in, b_f=b_f, conv_w=conv_w, conv_b=conv_b,
                   conv_ln_g=conv_ln_g, conv_ln_b=conv_ln_b, w_conv_pw=w_conv_pw, w_mem_kv=w_mem_kv, w_out=w_out,
                   final_g=final_g)
    ms = dict(norm_g=m_norm_g, mem_norm_g=m_mem_norm_g, w_in=m_w_in, b_f=m_b_f, conv_w=m_conv_w, conv_b=m_conv_b,
              conv_ln_g=m_conv_ln_g, conv_ln_b=m_conv_ln_b, w_conv_pw=m_w_conv_pw, w_mem_kv=m_w_mem_kv,
              w_out=m_w_out, final_g=m_final_g)
    vs = dict(norm_g=v_norm_g, mem_norm_g=v_mem_norm_g, w_in=v_w_in, b_f=v_b_f, conv_w=v_conv_w, conv_b=v_conv_b,
              conv_ln_g=v_conv_ln_g, conv_ln_b=v_conv_ln_b, w_conv_pw=v_w_conv_pw, w_mem_kv=v_w_mem_kv,
              w_out=v_w_out, final_g=v_final_g)

    names = ("norm_g", "mem_norm_g", "w_in", "b_f", "conv_w", "conv_b", "conv_ln_g", "conv_ln_b", "w_conv_pw",
             "w_mem_kv", "w_out", "final_g")
    deltas, new_m, new_v = {}, {}, {}
    for k in names:
        shape = weights[k].shape
        if k == "w_in":
            two_d = lambda a: a.reshape(shape[-2], shape[-1]).T
            back = lambda a: a.T.reshape(shape)
        else:
            two_d = lambda a: a.reshape(-1, shape[-1])
            back = lambda a: a.reshape(shape)
        d, nm, nv = _adamw(two_d(weights[k]), two_d(grads[k]), two_d(ms[k]), two_d(vs[k]), name=f"adamw_{k}")
        deltas[k], new_m[k], new_v[k] = back(d), back(nm), back(nv)

    return (loss, grad_x[None], *[grads[k] for k in names], *[deltas[k] for k in names],
            *[new_m[k] for k in names], *[new_v[k] for k in names])
```

```python
import functools

import jax
import jax.numpy as jnp
from jax import lax
from jax.experimental import pallas as pl
from jax.experimental.pallas import tpu as pltpu

F32 = jnp.float32
BF16 = jnp.bfloat16
MESH = pl.DeviceIdType.MESH

HEAD = 128
N_FOX = 8
N_MEMH = 4
CONV_W = 512
FOX_W = N_FOX * HEAD
MEM_W = N_MEMH * HEAD
MIX_W = CONV_W + FOX_W + MEM_W
CONV_K = 31
CONV_HALO = 32
EPS = 1e-6
NEG = -1e30
LANE = 128
N_CHIPS = 4
N_DEV = 8
VMEM_LIMIT = 56 * 1024 * 1024

ADAM_LR = 0.001
ADAM_B1 = 0.9
ADAM_B2 = 0.999
ADAM_EPS = 1e-08
ADAM_WD = 0.01
ADAM_STEP = 10

NT = (((1,), (1,)), ((), ()))
TN = (((0,), (0,)), ((), ()))
NN = (((1,), (0,)), ((), ()))


def _params(sem=None):
    kw = dict(vmem_limit_bytes=VMEM_LIMIT)
    if sem is not None:
        kw["dimension_semantics"] = sem
    return pltpu.CompilerParams(**kw)


def _sigmoid(v):
    return jax.nn.sigmoid(v)


def _dsilu(v, s):
    return s * (1.0 + v * (1.0 - s))


def _rms_fwd(x, g, *, tm, name, gather=None):
    R, D = x.shape
    steps = R // tm

    def body(x_ref, g_ref, h_ref, *ht_ref):
        xv = x_ref[...]
        r = lax.rsqrt(jnp.mean(xv * xv, axis=-1, keepdims=True) + EPS)
        h = xv * r * g_ref[...]
        h_ref[...] = h.astype(BF16)
        if ht_ref:
            ht_ref[0][...] = jnp.transpose(h).astype(BF16)

    in_specs = [pl.BlockSpec((tm, D), lambda i: (i, 0)), pl.BlockSpec((1, D), lambda i: (0, 0))]
    out_spec = pl.BlockSpec((tm, D), lambda i: (i, 0))
    out_shape = jax.ShapeDtypeStruct((R, D), BF16)
    if gather is None:
        return pl.pallas_call(
            body, name=name, grid=(steps,), in_specs=in_specs, out_specs=out_spec, out_shape=out_shape,
            compiler_params=_params(("parallel",)),
        )(x, g)
    res = _call_with_gather(
        body, gather, lambda: pl.program_id(0) == 0, lambda: pl.program_id(0) == steps - 1,
        n_in=2, n_out=2, in_specs=in_specs, out_specs=[out_spec, pl.BlockSpec((D, tm), lambda i: (0, i))],
        out_shape=[out_shape, jax.ShapeDtypeStruct((D, R), BF16)], scratch_shapes=[],
        name=name, grid=(steps,), compiler_params=_params(("arbitrary",)),
    )(x, g, *gather)
    return res[0], res[1], res[2:]


def _rms_bwd(x, g, dh, dres, *, tm, name):
    R, D = x.shape
    has_res = dres is not None

    def body(*refs):
        if has_res:
            x_ref, g_ref, dh_ref, dres_ref, dx_ref, dg_ref = refs
        else:
            x_ref, g_ref, dh_ref, dx_ref, dg_ref = refs
        i = pl.program_id(0)

        @pl.when(i == 0)
        def _():
            dg_ref[...] = jnp.zeros_like(dg_ref)

        xv = x_ref[...]
        r = lax.rsqrt(jnp.mean(xv * xv, axis=-1, keepdims=True) + EPS)
        n = xv * r
        dh = dh_ref[...]
        dg_ref[...] += jnp.sum(dh * n, axis=0, keepdims=True)
        dn = dh * g_ref[...]
        dx = r * (dn - n * jnp.mean(dn * n, axis=-1, keepdims=True))
        if has_res:
            dx = dx + dres_ref[...]
        dx_ref[...] = dx

    row = pl.BlockSpec((tm, D), lambda i: (i, 0))
    vec = pl.BlockSpec((1, D), lambda i: (0, 0))
    ins = [row, vec, row] + ([row] if has_res else [])
    args = (x, g, dh) + ((dres,) if has_res else ())
    return pl.pallas_call(
        body, name=name, grid=(R // tm,),
        in_specs=ins, out_specs=[row, vec],
        out_shape=[jax.ShapeDtypeStruct((R, D), F32), jax.ShapeDtypeStruct((1, D), F32)],
        compiler_params=_params(("arbitrary",)),
    )(*args)


def _mm(a, b, *, ta=False, tb=False, out_dtype=F32, add=None, tm, tn, tk, n=None, b_off=0, name, gather=None):
    M, K = (a.shape[1], a.shape[0]) if ta else a.shape
    nb = b.shape[0] if tb else b.shape[1]
    n = nb if n is None else n
    tm, tn, tk = min(tm, M), min(tn, n), min(tk, K)
    assert M % tm == 0 and n % tn == 0 and K % tk == 0, (name, M, n, K, tm, tn, tk)
    nk = K // tk
    has_add = add is not None

    def body(*refs):
        if has_add:
            a_ref, b_ref, add_ref, o_ref, acc_ref = refs
        else:
            a_ref, b_ref, o_ref, acc_ref = refs
        k = pl.program_id(2)

        @pl.when(k == 0)
        def _():
            acc_ref[...] = jnp.zeros_like(acc_ref)

        av = a_ref[...].astype(BF16)
        bv = b_ref[...].astype(BF16)
        dims = (((0 if ta else 1,), (1 if tb else 0,)), ((), ()))
        acc_ref[...] += lax.dot_general(av, bv, dims, preferred_element_type=F32)

        @pl.when(k == nk - 1)
        def _():
            r = acc_ref[...]
            if has_add:
                r = r + add_ref[...]
            o_ref[...] = r.astype(out_dtype)

    a_spec = pl.BlockSpec((tk, tm), lambda i, j, k: (k, i)) if ta else pl.BlockSpec((tm, tk), lambda i, j, k: (i, k))
    b_spec = (pl.BlockSpec((tn, tk), lambda i, j, k: (j + b_off, k)) if tb
              else pl.BlockSpec((tk, tn), lambda i, j, k: (k, j + b_off)))
    o_spec = pl.BlockSpec((tm, tn), lambda i, j, k: (i, j))
    ins = [a_spec, b_spec] + ([o_spec] if has_add else [])
    args = (a, b) + ((add,) if has_add else ())
    gi, gj = M // tm, n // tn
    out_shape = jax.ShapeDtypeStruct((M, n), out_dtype)
    if gather is None:
        return pl.pallas_call(
            body, name=name, grid=(gi, gj, nk),
            in_specs=ins, out_specs=o_spec, out_shape=out_shape,
            scratch_shapes=[pltpu.VMEM((tm, tn), F32)],
            compiler_params=_params(("parallel", "parallel", "arbitrary")),
        )(*args)
    at = lambda i, j, k: (pl.program_id(0) == i) & (pl.program_id(1) == j) & (pl.program_id(2) == k)
    res = _call_with_gather(
        body, gather, lambda: at(0, 0, 0), lambda: at(gi - 1, gj - 1, nk - 1),
        n_in=len(ins), n_out=1, in_specs=ins, out_specs=[o_spec], out_shape=[out_shape],
        scratch_shapes=[pltpu.VMEM((tm, tn), F32)],
        name=name, grid=(gi, gj, nk), compiler_params=_params(("arbitrary", "arbitrary", "arbitrary")),
    )(*args, *gather)
    return res[0], res[1:]


SUB = 8


def _shifted_copies(es_ref, rows):
    e = es_ref[0]
    for b in range(1, SUB):
        es_ref[b] = pltpu.roll(e, shift=rows - b, axis=0)


def _window(es_ref, offset, tt):
    b = offset % SUB
    return es_ref[b, offset - b:offset - b + tt, :]


def _conv_taps(w_ref, es_ref, offsets, tt):
    acc = w_ref[0:1, :] * _window(es_ref, offsets[0], tt)
    for k in range(1, CONV_K):
        acc = acc + w_ref[k:k + 1, :] * _window(es_ref, offsets[k], tt)
    return acc


def _conv_fwd(pc, cw, cb, lng, lnb, wpw, *, tt, name):
    S = pc.shape[0]
    C = CONV_W
    lead = CONV_HALO - (CONV_K - 1)
    rows = tt + CONV_HALO

    def body(a_ref, b_ref, gc_ref, cw_ref, cb_ref, lng_ref, lnb_ref, wpw_ref, y_ref, yt_ref, u1_ref, es_ref):
        i = pl.program_id(0)

        @pl.when(i == 0)
        def _():
            es_ref[0, 0:CONV_HALO, :] = jnp.zeros((CONV_HALO, C), F32)

        @pl.when(i > 0)
        def _():
            es_ref[0, 0:CONV_HALO, :] = es_ref[0, tt:tt + CONV_HALO, :]

        es_ref[0, CONV_HALO:CONV_HALO + tt, :] = a_ref[...] * _sigmoid(b_ref[...])
        _shifted_copies(es_ref, rows)
        u1 = _conv_taps(cw_ref, es_ref, [lead + k for k in range(CONV_K)], tt) + cb_ref[...]
        u1_ref[...] = u1
        mu = jnp.mean(u1, axis=-1, keepdims=True)
        xc = u1 - mu
        rstd = lax.rsqrt(jnp.mean(xc * xc, axis=-1, keepdims=True) + EPS)
        u2 = xc * rstd * lng_ref[...] + lnb_ref[...]
        u3 = u2 * _sigmoid(u2)
        z = jnp.dot(u3.astype(BF16), wpw_ref[...], preferred_element_type=F32)
        gc = gc_ref[...]
        yv = z * gc * _sigmoid(gc)
        y_ref[...] = yv.astype(BF16)
        yt_ref[...] = jnp.transpose(yv).astype(BF16)

    col = lambda c: pl.BlockSpec((tt, C), lambda i, c=c: (i, c))
    vec = pl.BlockSpec((1, C), lambda i: (0, 0))
    return pl.pallas_call(
        body, name=name, grid=(S // tt,),
        in_specs=[col(0), col(1), col(2), pl.BlockSpec((CONV_HALO, C), lambda i: (0, 0)), vec, vec, vec,
                  pl.BlockSpec((C, C), lambda i: (0, 0))],
        out_specs=[pl.BlockSpec((tt, C), lambda i: (i, 0)), pl.BlockSpec((C, tt), lambda i: (0, i)),
                   pl.BlockSpec((tt, C), lambda i: (i, 0))],
        out_shape=[jax.ShapeDtypeStruct((S, MIX_W), BF16), jax.ShapeDtypeStruct((MIX_W, S), BF16),
                   jax.ShapeDtypeStruct((S, C), F32)],
        scratch_shapes=[pltpu.VMEM((SUB, rows, C), F32)],
        compiler_params=_params(("arbitrary",)),
    )(pc, pc, pc, cw, cb, lng, lnb, wpw)


def _conv_bwd(pc, u1s, dy, cw, lng, lnb, wpw, *, tt, name):
    S = pc.shape[0]
    C = CONV_W
    nt = S // tt
    hb = tt // CONV_HALO
    lead = CONV_HALO - (CONV_K - 1)
    rows = tt + CONV_HALO

    def body(a_ref, b_ref, gc_ref, ah_ref, bh_ref, u1_ref, dy_ref, cw_ref, lng_ref, lnb_ref, wpw_ref,
             dpc_ref, dwpw_ref, dcw_ref, dsm_ref, eu_ref, ed_ref, dcw8_ref):
        i = pl.program_id(0)
        ti = nt - 1 - i

        @pl.when(i == 0)
        def _():
            ed_ref[0, tt:tt + CONV_HALO, :] = jnp.zeros((CONV_HALO, C), F32)
            dwpw_ref[...] = jnp.zeros_like(dwpw_ref)
            dcw8_ref[...] = jnp.zeros_like(dcw8_ref)
            dsm_ref[...] = jnp.zeros_like(dsm_ref)

        @pl.when(i > 0)
        def _():
            ed_ref[0, tt:tt + CONV_HALO, :] = ed_ref[0, 0:CONV_HALO, :]

        keep = jnp.where(ti > 0, 1.0, 0.0).astype(F32)
        eu_ref[0, 0:CONV_HALO, :] = ah_ref[...] * _sigmoid(bh_ref[...]) * keep
        a = a_ref[...]
        sb = _sigmoid(b_ref[...])
        eu_ref[0, CONV_HALO:CONV_HALO + tt, :] = a * sb
        _shifted_copies(eu_ref, rows)

        u1 = u1_ref[...]
        mu = jnp.mean(u1, axis=-1, keepdims=True)
        xc = u1 - mu
        rstd = lax.rsqrt(jnp.mean(xc * xc, axis=-1, keepdims=True) + EPS)
        nhat = xc * rstd
        g = lng_ref[...]
        u2 = nhat * g + lnb_ref[...]
        s2 = _sigmoid(u2)
        u3 = (u2 * s2).astype(BF16)
        z = jnp.dot(u3, wpw_ref[...], preferred_element_type=F32)

        gc = gc_ref[...]
        sg = _sigmoid(gc)
        dyv = dy_ref[...]
        dz = (dyv * gc * sg).astype(BF16)
        dpc_ref[:, 2 * C:3 * C] = (dyv * z * _dsilu(gc, sg)).astype(BF16)

        du3 = lax.dot_general(dz, wpw_ref[...], NT, preferred_element_type=F32)
        dwpw_ref[...] += lax.dot_general(u3, dz, TN, preferred_element_type=F32)
        du2 = du3 * _dsilu(u2, s2)
        dsm_ref[1:2, :] += jnp.sum(du2 * nhat, axis=0, keepdims=True)
        dsm_ref[2:3, :] += jnp.sum(du2, axis=0, keepdims=True)
        dn = du2 * g
        du1 = rstd * (dn - jnp.mean(dn, axis=-1, keepdims=True)
                      - nhat * jnp.mean(dn * nhat, axis=-1, keepdims=True))
        dsm_ref[0:1, :] += jnp.sum(du1, axis=0, keepdims=True)
        ed_ref[0, 0:tt, :] = du1
        _shifted_copies(ed_ref, rows)

        du0 = _conv_taps(cw_ref, ed_ref, [CONV_K - 1 - k for k in range(CONV_K)], tt)
        for k in range(CONV_K):
            prod = du1 * _window(eu_ref, lead + k, tt)
            part = prod[0:SUB]
            for r in range(1, tt // SUB):
                part = part + prod[r * SUB:(r + 1) * SUB]
            dcw8_ref[k * SUB:(k + 1) * SUB, :] += part

        dpc_ref[:, 0:C] = (du0 * sb).astype(BF16)
        dpc_ref[:, C:2 * C] = (du0 * a * sb * (1.0 - sb)).astype(BF16)

        @pl.when(i == nt - 1)
        def _():
            dcw_ref[...] = jnp.zeros_like(dcw_ref)
            for k in range(CONV_K):
                dcw_ref[k:k + 1, :] = jnp.sum(dcw8_ref[k * SUB:(k + 1) * SUB, :], axis=0, keepdims=True)

    col = lambda c: pl.BlockSpec((tt, C), lambda i, c=c: (nt - 1 - i, c))
    halo = lambda c: pl.BlockSpec((CONV_HALO, C), lambda i, c=c: (jnp.maximum((nt - 1 - i) * hb - 1, 0), c))
    vec = pl.BlockSpec((1, C), lambda i: (0, 0))
    fixed = lambda r: pl.BlockSpec((r, C), lambda i: (0, 0))
    return pl.pallas_call(
        body, name=name, grid=(nt,),
        in_specs=[col(0), col(1), col(2), halo(0), halo(1),
                  pl.BlockSpec((tt, C), lambda i: (nt - 1 - i, 0)),
                  pl.BlockSpec((tt, C), lambda i: (nt - 1 - i, 0)),
                  fixed(CONV_HALO), vec, vec, fixed(C)],
        out_specs=[pl.BlockSpec((tt, 3 * C), lambda i: (nt - 1 - i, 0)), fixed(C), fixed(CONV_HALO), fixed(8)],
        out_shape=[jax.ShapeDtypeStruct((S, 3 * C), BF16), jax.ShapeDtypeStruct((C, C), F32),
                   jax.ShapeDtypeStruct((CONV_HALO, C), F32), jax.ShapeDtypeStruct((8, C), F32)],
        scratch_shapes=[pltpu.VMEM((SUB, rows, C), F32), pltpu.VMEM((SUB, rows, C), F32),
                        pltpu.VMEM((CONV_HALO * SUB, C), F32)],
        compiler_params=_params(("arbitrary",)),
    )(pc, pc, pc, pc, pc, u1s, dy, cw, lng, lnb, wpw)


def _tri(n, lower):
    r = lax.broadcasted_iota(jnp.int32, (n, n), 0)
    c = lax.broadcasted_iota(jnp.int32, (n, n), 1)
    return jnp.where((r >= c) if lower else (r <= c), 1.0, 0.0).astype(F32)


def _fox_cumsum(pfl, bf, *, tc, name):
    S = pfl.shape[0]

    def body(fl_ref, bf_ref, qa_ref, ka_ref, carry_ref):
        i = pl.program_id(0)

        @pl.when(i == 0)
        def _():
            carry_ref[...] = jnp.zeros_like(carry_ref)

        z = fl_ref[...] + bf_ref[...]
        logf = jnp.minimum(z, 0.0) - jnp.log1p(jnp.exp(-jnp.abs(z)))
        c = jnp.dot(_tri(tc, True), logf, precision=lax.Precision.HIGHEST,
                    preferred_element_type=F32) + carry_ref[0:1, :]
        carry_ref[0:1, :] = c[tc - 1:tc, :]

        cs = c * (HEAD ** 0.5)
        hi = cs.astype(BF16).astype(F32)
        r1 = cs - hi
        mid = r1.astype(BF16).astype(F32)
        lo = r1 - mid
        lane = lax.broadcasted_iota(jnp.int32, (tc, LANE), 1)
        is_hi = (lane == 0) | (lane == 3)
        is_mid = (lane == 1) | (lane == 4)
        for h in range(N_FOX):
            col = lambda v: jnp.sum(jnp.where(lane == h, v, 0.0), axis=1, keepdims=True)
            pieces = jnp.where(is_hi, col(hi), jnp.where(is_mid, col(mid), col(lo)))
            qa_ref[h] = jnp.where(lane < 3, pieces, jnp.where(lane < 6, 1.0, 0.0)).astype(BF16)
            ka_ref[h] = jnp.where(lane < 3, 1.0, jnp.where(lane < 6, -pieces, 0.0)).astype(BF16)

    out = pl.BlockSpec((N_FOX, tc, HEAD), lambda i: (0, i, 0))
    return pl.pallas_call(
        body, name=name, grid=(S // tc,),
        in_specs=[pl.BlockSpec((tc, LANE), lambda i: (i, 0)), pl.BlockSpec((1, LANE), lambda i: (0, 0))],
        out_specs=[out, out],
        out_shape=[jax.ShapeDtypeStruct((N_FOX, S, HEAD), BF16)] * 2,
        scratch_shapes=[pltpu.VMEM((8, LANE), F32)],
        compiler_params=_params(("arbitrary",)),
    )(pfl, bf)


def _fox_dlogit(dc, pfl, bf, *, tc, name):
    S = pfl.shape[0]
    nt = S // tc

    def body(dc_ref, fl_ref, bf_ref, dfl_ref, dbf_ref, carry_ref):
        i = pl.program_id(0)

        @pl.when(i == 0)
        def _():
            carry_ref[...] = jnp.zeros_like(carry_ref)
            dbf_ref[...] = jnp.zeros_like(dbf_ref)

        dlogf = jnp.dot(_tri(tc, False), dc_ref[...], precision=lax.Precision.HIGHEST,
                        preferred_element_type=F32) + carry_ref[0:1, :]
        carry_ref[0:1, :] = dlogf[0:1, :]
        dz = dlogf * _sigmoid(-(fl_ref[...] + bf_ref[...]))
        dfl_ref[...] = dz.astype(BF16)
        dbf_ref[0:1, :] += jnp.sum(dz, axis=0, keepdims=True)

    rev = pl.BlockSpec((tc, LANE), lambda i: (nt - 1 - i, 0))
    return pl.pallas_call(
        body, name=name, grid=(nt,),
        in_specs=[rev, rev, pl.BlockSpec((1, LANE), lambda i: (0, 0))],
        out_specs=[rev, pl.BlockSpec((8, LANE), lambda i: (0, 0))],
        out_shape=[jax.ShapeDtypeStruct((S, LANE), BF16), jax.ShapeDtypeStruct((8, LANE), F32)],
        scratch_shapes=[pltpu.VMEM((8, LANE), F32)],
        compiler_params=_params(("arbitrary",)),
    )(dc, pfl, bf)


def _as_row(col):
    return jnp.transpose(jnp.broadcast_to(col, (col.shape[0], LANE)))[0:1, :]


def _causal_part(rows, cols, row0, col0, q_is_row=True):
    r = lax.broadcasted_iota(jnp.int32, (rows, cols), 0) + row0
    c = lax.broadcasted_iota(jnp.int32, (rows, cols), 1) + col0
    return (r >= c) if q_is_row else (c >= r)


LOG2E = 1.4426950408889634
FOX_SCALE2 = (HEAD ** -0.5) * LOG2E


def _fox_fwd(qkv, pg, qa, ka, y_all, yt_all, *, tq, nsub, name):
    S = qkv.shape[0]
    nq = S // tq
    rs = tq // nsub
    ybase = CONV_W // HEAD

    def body(q_ref, qa_ref, k_ref, ka_ref, v_ref, g_ref, yin_ref, ytin_ref, y_ref, yt_ref, o_ref, m_ref, linv_ref):
        i = pl.program_id(1)
        q = jnp.concatenate([q_ref[...], qa_ref[...]], axis=1)
        qs = [q[r * rs:(r + 1) * rs] for r in range(nsub)]
        ones = jnp.where(lax.broadcasted_iota(jnp.int32, (tq, HEAD), 1) == 0, 1.0, 0.0).astype(BF16)

        def blk(j, carry, masked):
            off = pl.multiple_of(j * tq, tq)
            kj = jnp.concatenate([k_ref[pl.ds(off, tq), :], ka_ref[pl.ds(off, tq), :]], axis=1)
            vj = jnp.concatenate([v_ref[pl.ds(off, tq), :], ones], axis=1)
            out = []
            for r in range(nsub):
                m, acc = carry[r]
                nk = (r + 1) * rs if masked else tq
                s = lax.dot_general(qs[r], kj[:nk], NT, preferred_element_type=F32) * FOX_SCALE2
                if masked:
                    s = jnp.where(_causal_part(rs, nk, r * rs, 0), s, NEG)
                m_new = jnp.maximum(m, jnp.ceil(jnp.max(s, axis=-1, keepdims=True)))
                pb = jnp.exp2(s - m_new).astype(BF16)
                acc = jnp.exp2(m - m_new) * acc + jnp.dot(pb, vj[:nk], preferred_element_type=F32)
                out.append((m_new, acc))
            return tuple(out)

        init = tuple((jnp.full((rs, 1), NEG, F32), jnp.zeros((rs, 2 * HEAD), F32)) for _ in range(nsub))
        carry = lax.fori_loop(0, i, lambda j, c: blk(j, c, False), init)
        carry = blk(i, carry, True)
        m = jnp.concatenate([c[0] for c in carry], axis=0)
        acc = jnp.concatenate([c[1] for c in carry], axis=0)
        linv = 1.0 / acc[:, HEAD:HEAD + 1]
        o = acc[:, :HEAD] * linv
        g = g_ref[...]
        yv = o * g * _sigmoid(g)
        y_ref[...] = yv.astype(BF16)
        yt_ref[...] = jnp.transpose(yv).astype(BF16)
        o_ref[...] = o
        m_ref[...] = _as_row(m)
        linv_ref[...] = _as_row(linv)

    tile = lambda base: pl.BlockSpec((tq, HEAD), lambda h, i, base=base: (i, base + h))
    full = lambda base: pl.BlockSpec((S, HEAD), lambda h, i, base=base: (0, base + h))
    rowv = pl.BlockSpec((None, None, 1, tq), lambda h, i: (h, i, 0, 0))
    stat = jax.ShapeDtypeStruct((N_FOX, nq, 1, tq), F32)
    return pl.pallas_call(
        body, name=name, grid=(N_FOX, nq),
        in_specs=[tile(0), pl.BlockSpec((None, tq, HEAD), lambda h, i: (h, i, 0)),
                  full(N_FOX), pl.BlockSpec((None, S, HEAD), lambda h, i: (h, 0, 0)),
                  full(2 * N_FOX), tile(0), ANY, ANY],
        out_specs=[tile(ybase), pl.BlockSpec((HEAD, tq), lambda h, i: (ybase + h, i)), tile(0), rowv, rowv],
        out_shape=[jax.ShapeDtypeStruct(y_all.shape, BF16), jax.ShapeDtypeStruct(yt_all.shape, BF16),
                   jax.ShapeDtypeStruct((S, FOX_W), F32), stat, stat],
        input_output_aliases={6: 0, 7: 1},
        compiler_params=_params(("parallel", "arbitrary")),
    )(qkv, qa, qkv, ka, qkv, pg, y_all, yt_all)


def _fox_bwd_prep(dy, o, pg, *, tq, name):
    S = o.shape[0]
    base = CONV_W // HEAD

    def body(dy_ref, o_ref, g_ref, do_ref, dg_ref, dl_ref):
        g = g_ref[...]
        sg = _sigmoid(g)
        dyv = dy_ref[...]
        ov = o_ref[...]
        dob = (dyv * g * sg).astype(BF16)
        do_ref[...] = dob
        dg_ref[...] = (dyv * ov * _dsilu(g, sg)).astype(BF16)
        dl_ref[...] = _as_row(jnp.sum(dob.astype(F32) * ov, axis=-1, keepdims=True))

    tile = lambda b: pl.BlockSpec((tq, HEAD), lambda h, i, b=b: (i, b + h))
    return pl.pallas_call(
        body, name=name, grid=(N_FOX, S // tq),
        in_specs=[tile(base), tile(0), tile(0)],
        out_specs=[tile(0), tile(0), pl.BlockSpec((None, None, 1, tq), lambda h, i: (h, i, 0, 0))],
        out_shape=[jax.ShapeDtypeStruct((S, FOX_W), BF16), jax.ShapeDtypeStruct((S, FOX_W), BF16),
                   jax.ShapeDtypeStruct((N_FOX, S // tq, 1, tq), F32)],
        compiler_params=_params(("parallel", "parallel")),
    )(dy, o, pg)


def _fox_bwd(qkv, do, qa, ka, m_row, linv_row, delta_row, *, tq, nsub, name):
    S = qkv.shape[0]
    nq = S // tq
    cs = tq // nsub
    scale = HEAD ** -0.5

    def body(k_ref, ka_ref, v_ref, q_ref, qa_ref, do_ref, mr_ref, lir_ref, dlr_ref, dq_ref, dk_ref, dv_ref, dc_ref):
        j = pl.program_id(1)

        @pl.when(j == 0)
        def _():
            dq_ref[...] = jnp.zeros_like(dq_ref)

        kj = k_ref[...]
        kja = jnp.concatenate([kj, ka_ref[...]], axis=1)
        vj = v_ref[...]

        def blk(i, carry, masked):
            dk, dv, dc = carry
            m_i = mr_ref[i]
            linv_i = lir_ref[i]
            dl_i = dlr_ref[i]
            for c in range(nsub):
                off = pl.multiple_of(i * tq + c * cs, cs)
                cols = slice(c * cs, (c + 1) * cs)
                qi = q_ref[pl.ds(off, cs), :]
                qia = jnp.concatenate([qi, qa_ref[pl.ds(off, cs), :]], axis=1)
                doi = do_ref[pl.ds(off, cs), :]
                nk = (c + 1) * cs if masked else tq
                st = lax.dot_general(kja[:nk], qia, NT, preferred_element_type=F32) * FOX_SCALE2
                if masked:
                    st = jnp.where(_causal_part(nk, cs, 0, c * cs, q_is_row=False), st, NEG)
                grow = lambda u, nk=nk: u if nk == tq else jnp.concatenate(
                    [u, jnp.zeros((tq - nk, u.shape[1]), F32)], axis=0)
                pt = jnp.exp2(st - m_i[:, cols]).astype(BF16).astype(F32) * linv_i[:, cols]
                dv = dv + grow(jnp.dot(pt.astype(BF16), doi, preferred_element_type=F32))
                dpt = lax.dot_general(vj[:nk], doi, NT, preferred_element_type=F32)
                dst = pt * (dpt - dl_i[:, cols])
                dsb = dst.astype(BF16)
                dk = dk + grow(jnp.dot(dsb, qi, preferred_element_type=F32))
                dq_ref[pl.ds(off, cs), :] += lax.dot_general(dsb, kj[:nk], TN, preferred_element_type=F32) * scale
                dc = dc - grow(jnp.sum(dst, axis=-1, keepdims=True))
            return dk, dv, dc

        init = (jnp.zeros((tq, HEAD), F32), jnp.zeros((tq, HEAD), F32), jnp.zeros((tq, 1), F32))
        carry = blk(j, init, True)
        dk, dv, dc = lax.fori_loop(j + 1, nq, lambda i, c: blk(i, c, False), carry)
        dk_ref[...] = (dk * scale).astype(BF16)
        dv_ref[...] = dv.astype(BF16)
        dc_ref[...] = _as_row(dc)

    tile = lambda base: pl.BlockSpec((tq, HEAD), lambda h, j, base=base: (j, base + h))
    full = lambda base: pl.BlockSpec((S, HEAD), lambda h, j, base=base: (0, base + h))
    atile = pl.BlockSpec((None, tq, HEAD), lambda h, j: (h, j, 0))
    afull = pl.BlockSpec((None, S, HEAD), lambda h, j: (h, 0, 0))
    rowt = pl.BlockSpec((None, None, 1, tq), lambda h, j: (h, j, 0, 0))
    rowv = pl.BlockSpec((None, nq, 1, tq), lambda h, j: (h, 0, 0, 0))
    return pl.pallas_call(
        body, name=name, grid=(N_FOX, nq),
        in_specs=[tile(N_FOX), atile, tile(2 * N_FOX), full(0), afull, full(0), rowv, rowv, rowv],
        out_specs=[full(0), tile(0), tile(0), rowt],
        out_shape=[jax.ShapeDtypeStruct((S, FOX_W), F32), jax.ShapeDtypeStruct((S, FOX_W), BF16),
                   jax.ShapeDtypeStruct((S, FOX_W), BF16), jax.ShapeDtypeStruct((N_FOX, nq, 1, tq), F32)],
        compiler_params=_params(("arbitrary", "arbitrary")),
    )(qkv, ka, qkv, qkv, qa, do, m_row, linv_row, delta_row)


def _mem_heads(mq, mkv, h):
    lo = h * HEAD
    qh = mq[:, lo:lo + HEAD].astype(BF16)
    kh = mkv[:, lo:lo + HEAD].astype(BF16)
    vh = mkv[:, MEM_W + lo:MEM_W + lo + HEAD].astype(BF16)
    return qh, kh, vh


def _mem_softmax(qh, kh):
    s = lax.dot_general(qh, kh, NT, preferred_element_type=F32) * (HEAD ** -0.5)
    e = jnp.exp(s - jnp.max(s, axis=-1, keepdims=True))
    return e / jnp.sum(e, axis=-1, keepdims=True)


def _mem_fwd(pg, mkv, y_all, yt_all, *, tq, name):
    S = pg.shape[0]
    M = mkv.shape[0]
    qb = FOX_W // MEM_W
    yb = (CONV_W + FOX_W) // MEM_W

    def body(mq_ref, g_ref, mkv_ref, yin_ref, ytin_ref, y_ref, yt_ref):
        mq = mq_ref[...]
        mkvv = mkv_ref[...]
        for h in range(N_MEMH):
            qh, kh, vh = _mem_heads(mq, mkvv, h)
            p = _mem_softmax(qh, kh)
            o = jnp.dot(p.astype(BF16), vh, preferred_element_type=F32)
            g = g_ref[:, h * HEAD:(h + 1) * HEAD]
            yv = o * g * _sigmoid(g)
            y_ref[:, h * HEAD:(h + 1) * HEAD] = yv.astype(BF16)
            yt_ref[h * HEAD:(h + 1) * HEAD, :] = jnp.transpose(yv).astype(BF16)

    return pl.pallas_call(
        body, name=name, grid=(S // tq,),
        in_specs=[pl.BlockSpec((tq, MEM_W), lambda i: (i, qb)), pl.BlockSpec((tq, MEM_W), lambda i: (i, qb + 1)),
                  pl.BlockSpec((M, 2 * MEM_W), lambda i: (0, 0)), ANY, ANY],
        out_specs=[pl.BlockSpec((tq, MEM_W), lambda i: (i, yb)), pl.BlockSpec((MEM_W, tq), lambda i: (yb, i))],
        out_shape=[jax.ShapeDtypeStruct(y_all.shape, BF16), jax.ShapeDtypeStruct(yt_all.shape, BF16)],
        input_output_aliases={3: 0, 4: 1},
        compiler_params=_params(("parallel",)),
    )(pg, pg, mkv, y_all, yt_all)


def _mem_bwd(pg, mkv, dy, *, tq, name):
    S = pg.shape[0]
    M = mkv.shape[0]
    qb = FOX_W // MEM_W
    yb = (CONV_W + FOX_W) // MEM_W
    scale = HEAD ** -0.5

    def body(mq_ref, g_ref, mkv_ref, dy_ref, dmq_ref, dg_ref, dmkv_ref):
        i = pl.program_id(0)

        @pl.when(i == 0)
        def _():
            dmkv_ref[...] = jnp.zeros_like(dmkv_ref)

        mq = mq_ref[...]
        mkvv = mkv_ref[...]
        for h in range(N_MEMH):
            lo = h * HEAD
            qh, kh, vh = _mem_heads(mq, mkvv, h)
            p = _mem_softmax(qh, kh)
            o = jnp.dot(p.astype(BF16), vh, preferred_element_type=F32)
            g = g_ref[:, lo:lo + HEAD]
            sg = _sigmoid(g)
            dyh = dy_ref[:, lo:lo + HEAD]
            do = dyh * g * sg
            dg_ref[:, lo:lo + HEAD] = (dyh * o * _dsilu(g, sg)).astype(BF16)
            dob = do.astype(BF16)
            dp = lax.dot_general(dob, vh, NT, preferred_element_type=F32)
            ds = p * (dp - jnp.sum(do * o, axis=-1, keepdims=True))
            dsb = ds.astype(BF16)
            dmq_ref[:, lo:lo + HEAD] = (jnp.dot(dsb, kh, preferred_element_type=F32) * scale).astype(BF16)
            dmkv_ref[:, lo:lo + HEAD] += lax.dot_general(dsb, qh, TN, preferred_element_type=F32) * scale
            dmkv_ref[:, MEM_W + lo:MEM_W + lo + HEAD] += lax.dot_general(
                p.astype(BF16), dob, TN, preferred_element_type=F32)

    return pl.pallas_call(
        body, name=name, grid=(S // tq,),
        in_specs=[pl.BlockSpec((tq, MEM_W), lambda i: (i, qb)), pl.BlockSpec((tq, MEM_W), lambda i: (i, qb + 1)),
                  pl.BlockSpec((M, 2 * MEM_W), lambda i: (0, 0)), pl.BlockSpec((tq, MEM_W), lambda i: (i, yb))],
        out_specs=[pl.BlockSpec((tq, MEM_W), lambda i: (i, 0)), pl.BlockSpec((tq, MEM_W), lambda i: (i, 0)),
                   pl.BlockSpec((M, 2 * MEM_W), lambda i: (0, 0))],
        out_shape=[jax.ShapeDtypeStruct((S, MEM_W), BF16), jax.ShapeDtypeStruct((S, MEM_W), BF16),
                   jax.ShapeDtypeStruct((M, 2 * MEM_W), F32)],
        compiler_params=_params(("arbitrary",)),
    )(pg, pg, mkv, dy)


def _final(x2, target, fg, *, tm, name):
    S, D = x2.shape

    def body(x_ref, t_ref, g_ref, dx_ref, dxb_ref, dg_ref, ls_ref):
        i = pl.program_id(0)

        @pl.when(i == 0)
        def _():
            dg_ref[...] = jnp.zeros_like(dg_ref)
            ls_ref[...] = jnp.zeros_like(ls_ref)

        xv = x_ref[...]
        r = lax.rsqrt(jnp.mean(xv * xv, axis=-1, keepdims=True) + EPS)
        n = xv * r
        g = g_ref[...]
        diff = n * g - t_ref[...]
        ls_ref[...] += jnp.sum(diff * diff)
        dout = diff * (1.0 / D)
        dg_ref[...] += jnp.sum(dout * n, axis=0, keepdims=True)
        dn = dout * g
        dx = r * (dn - n * jnp.mean(dn * n, axis=-1, keepdims=True))
        dx_ref[...] = dx
        dxb_ref[...] = dx.astype(BF16)

    row = pl.BlockSpec((tm, D), lambda i: (i, 0))
    vec = pl.BlockSpec((1, D), lambda i: (0, 0))
    return pl.pallas_call(
        body, name=name, grid=(S // tm,),
        in_specs=[row, row, vec],
        out_specs=[row, row, vec, pl.BlockSpec((8, LANE), lambda i: (0, 0))],
        out_shape=[jax.ShapeDtypeStruct((S, D), F32), jax.ShapeDtypeStruct((S, D), BF16),
                   jax.ShapeDtypeStruct((1, D), F32), jax.ShapeDtypeStruct((8, LANE), F32)],
        compiler_params=_params(("arbitrary",)),
    )(x2, target, fg)


def _adamw(w, g, m, v, *, name):
    R, C = w.shape
    tr, tc = R, C
    for cand in (256, 128, 64, 32, 16, 8):
        if R % cand == 0 and R > cand:
            tr = cand
            break
    if tr == R and R > 256 and C % 256 == 0:
        tc = 256
    c1 = 1.0 - ADAM_B1 ** ADAM_STEP
    c2 = 1.0 - ADAM_B2 ** ADAM_STEP

    def body(w_ref, g_ref, m_ref, v_ref, d_ref, nm_ref, nv_ref):
        gv = g_ref[...]
        nm = ADAM_B1 * m_ref[...] + (1.0 - ADAM_B1) * gv
        nv = ADAM_B2 * v_ref[...] + (1.0 - ADAM_B2) * (gv * gv)
        nm_ref[...] = nm
        nv_ref[...] = nv
        d_ref[...] = -ADAM_LR * ((nm / c1) / (jnp.sqrt(nv / c2) + ADAM_EPS) + ADAM_WD * w_ref[...])

    spec = pl.BlockSpec((tr, tc), lambda i, j: (i, j))
    shp = jax.ShapeDtypeStruct((R, C), F32)
    return pl.pallas_call(
        body, name=name, grid=(R // tr, C // tc),
        in_specs=[spec] * 4, out_specs=[spec] * 3, out_shape=[shp] * 3,
        compiler_params=_params(("parallel", "parallel")),
    )(w, g, m, v)


def _sum4(q, own, me, *, name):
    _, R, C = q.shape
    tr = R
    for cand in (256, 128, 64, 32, 16, 8):
        if R % cand == 0 and R > cand:
            tr = cand
            break

    def body(me_ref, own_ref, q1_ref, q2_ref, q3_ref, o_ref):
        f = lambda r: r[...].astype(F32)
        o_ref[...] = ((f(own_ref) + f(q1_ref)) + f(q2_ref)) + f(q3_ref)

    blk = lambda d: pl.BlockSpec((None, tr, C), lambda i, me_ref, d=d: (me_ref[0] ^ d, i, 0))
    return pl.pallas_call(
        body, name=name,
        grid_spec=pltpu.PrefetchScalarGridSpec(
            num_scalar_prefetch=1, grid=(R // tr,),
            in_specs=[blk(0), blk(1), blk(2), blk(3)],
            out_specs=pl.BlockSpec((tr, C), lambda i, me_ref: (i, 0))),
        out_shape=jax.ShapeDtypeStruct((R, C), F32),
        compiler_params=_params(("parallel",)),
    )(jnp.reshape(me, (1,)).astype(jnp.int32), own, q, q, q)


def _add_sibling(g, got, c, *, name):
    K, _, R, C = g.shape
    tr = R
    for cand in (256, 128, 64, 32, 16, 8):
        if R % cand == 0 and R > cand:
            tr = cand
            break

    def body(c_ref, a_ref, b_ref, o_ref):
        o_ref[...] = (a_ref[...] + b_ref[...]).astype(BF16)

    spec = pl.BlockSpec((None, tr, C), lambda k, i, c_ref: (k, i, 0))
    return pl.pallas_call(
        body, name=name,
        grid_spec=pltpu.PrefetchScalarGridSpec(
            num_scalar_prefetch=1, grid=(K, R // tr),
            in_specs=[pl.BlockSpec((None, None, tr, C), lambda k, i, c_ref: (k, c_ref[0], i, 0)), spec],
            out_specs=spec),
        out_shape=jax.ShapeDtypeStruct((K, R, C), BF16),
        compiler_params=_params(("parallel", "parallel")),
    )(jnp.reshape(c, (1,)).astype(jnp.int32), g, got)


ANY = pl.BlockSpec(memory_space=pl.ANY)


def _other_chips(x, y):
    return [(d, 1 - x if d & 2 else x, 1 - y if d & 1 else y) for d in (1, 2, 3)]


def _gather_ici_copies(src_refs, out_refs, ici_send, ici_recv):
    x, y, c = lax.axis_index("x"), lax.axis_index("y"), lax.axis_index("c")
    me = 2 * x + y
    return [pltpu.make_async_remote_copy(
        src_ref=src_refs[t].at[c], dst_ref=out_refs[t].at[me, c],
        send_sem=ici_send.at[3 * t + d - 1], recv_sem=ici_recv.at[3 * t + d - 1],
        device_id=(tx, ty, c), device_id_type=MESH)
        for t in range(len(src_refs)) for d, tx, ty in _other_chips(x, y)]


def _gather_finish(src_refs, out_refs, ici_send, ici_recv, d2d_send, d2d_recv):
    n = len(src_refs)
    x, y, c = lax.axis_index("x"), lax.axis_index("y"), lax.axis_index("c")
    passed = []
    for t in range(n):
        for d, tx, ty in _other_chips(x, y):
            landed = out_refs[t].at[2 * tx + ty, c]
            pltpu.make_async_remote_copy(
                src_ref=landed, dst_ref=landed,
                send_sem=ici_send.at[3 * t + d - 1], recv_sem=ici_recv.at[3 * t + d - 1],
                device_id=(tx, ty, c), device_id_type=MESH).wait_recv()
            cp = pltpu.make_async_remote_copy(
                src_ref=landed, dst_ref=landed,
                send_sem=d2d_send.at[3 * t + d - 1], recv_sem=d2d_recv.at[3 * t + d - 1],
                device_id=(x, y, 1 - c), device_id_type=MESH)
            cp.start()
            passed.append(cp)
    for t in range(n):
        for d, tx, ty in _other_chips(x, y):
            theirs = out_refs[t].at[2 * tx + ty, 1 - c]
            pltpu.make_async_remote_copy(
                src_ref=theirs, dst_ref=theirs,
                send_sem=d2d_send.at[3 * t + d - 1], recv_sem=d2d_recv.at[3 * t + d - 1],
                device_id=(x, y, 1 - c), device_id_type=MESH).wait_recv()
    for cp in _gather_ici_copies(src_refs, out_refs, ici_send, ici_recv) + passed:
        cp.wait_send()


def _call_with_gather(body, shards, first, last, *, n_in, n_out, in_specs, out_specs, out_shape, scratch_shapes,
                      **kw):
    n = len(shards)
    n_scr = len(scratch_shapes)

    def wrapped(*refs):
        ins, srcs = refs[:n_in], refs[n_in:n_in + n]
        outs = refs[n_in + n:n_in + n + n_out]
        gouts = refs[n_in + n + n_out:n_in + 2 * n + n_out]
        scr = refs[n_in + 2 * n + n_out:n_in + 2 * n + n_out + n_scr]
        ici_send, ici_recv, d2d_send, d2d_recv = refs[n_in + 2 * n + n_out + n_scr:]

        @pl.when(first())
        def _():
            for cp in _gather_ici_copies(srcs, gouts, ici_send, ici_recv):
                cp.start()

        body(*ins, *outs, *scr)

        @pl.when(last())
        def _():
            _gather_finish(srcs, gouts, ici_send, ici_recv, d2d_send, d2d_recv)

    sem = pltpu.SemaphoreType.DMA((3 * n,))
    return pl.pallas_call(
        wrapped,
        in_specs=list(in_specs) + [ANY] * n,
        out_specs=list(out_specs) + [ANY] * n,
        out_shape=list(out_shape) + [jax.ShapeDtypeStruct((N_CHIPS,) + s.shape, s.dtype) for s in shards],
        scratch_shapes=list(scratch_shapes) + [sem, sem, sem, sem],
        **kw)


def _scatter_copies(src_refs, out_refs, send_sems, recv_sems):
    n = len(src_refs)
    x, y, c = lax.axis_index("x"), lax.axis_index("y"), lax.axis_index("c")
    me = 2 * x + y
    copies = []
    for t in range(n):
        for d, tx, ty in _other_chips(x, y):
            copies.append(pltpu.make_async_remote_copy(
                src_ref=src_refs[t].at[2 * tx + ty], dst_ref=out_refs[t].at[me],
                send_sem=send_sems.at[3 * t + d - 1], recv_sem=recv_sems.at[3 * t + d - 1],
                device_id=(tx, ty, c), device_id_type=MESH))
    return copies


def _mm_scatter(a, b, a2, b2, srcs, *, tm, tn, tk, name):
    M, K = a.shape
    N = b.shape[1]
    K2 = a2.shape[1]
    tm, tn, tk = min(tm, M), min(tn, N), min(tk, K)
    assert M % tm == 0 and N % tn == 0 and K % tk == 0, (name, M, N, K, tm, tn, tk)
    gi, gj, gk = M // tm, N // tn, K // tk
    n = len(srcs)

    def body(*refs):
        a_ref, b_ref, a2_ref, b2_ref = refs[:4]
        src_refs = refs[4:4 + n]
        o_ref = refs[4 + n]
        out_refs = refs[5 + n:5 + 2 * n]
        acc_ref, send_sems, recv_sems = refs[5 + 2 * n:]
        i, j, k = pl.program_id(0), pl.program_id(1), pl.program_id(2)

        @pl.when((i == 0) & (j == 0) & (k == 0))
        def _():
            for cp in _scatter_copies(src_refs, out_refs, send_sems, recv_sems):
                cp.start()

        @pl.when(k == 0)
        def _():
            acc_ref[...] = jnp.dot(a2_ref[...].astype(BF16), b2_ref[...].astype(BF16), preferred_element_type=F32)

        acc_ref[...] += jnp.dot(a_ref[...].astype(BF16), b_ref[...].astype(BF16), preferred_element_type=F32)

        @pl.when(k == gk - 1)
        def _():
            o_ref[...] = acc_ref[...]

        @pl.when((i == gi - 1) & (j == gj - 1) & (k == gk - 1))
        def _():
            for cp in _scatter_copies(src_refs, out_refs, send_sems, recv_sems):
                cp.wait()

    o_spec = pl.BlockSpec((tm, tn), lambda i, j, k: (i, j))
    sem = pltpu.SemaphoreType.DMA((3 * n,))
    res = pl.pallas_call(
        body, name=name, grid=(gi, gj, gk),
        in_specs=[pl.BlockSpec((tm, tk), lambda i, j, k: (i, k)), pl.BlockSpec((tk, tn), lambda i, j, k: (k, j)),
                  pl.BlockSpec((tm, K2), lambda i, j, k: (i, 0)), pl.BlockSpec((K2, tn), lambda i, j, k: (0, j))]
        + [ANY] * n,
        out_specs=[o_spec] + [ANY] * n,
        out_shape=[jax.ShapeDtypeStruct((M, N), F32)] + [jax.ShapeDtypeStruct(s.shape, s.dtype) for s in srcs],
        scratch_shapes=[pltpu.VMEM((tm, tn), F32), sem, sem],
        compiler_params=_params(("arbitrary", "arbitrary", "arbitrary")),
    )(a, b, a2, b2, *srcs)
    return res[0], res[1:]


def _swap_halves(grads, *, name):
    n = len(grads)

    def body(*refs):
        src_refs, out_refs = refs[:n], refs[n:2 * n]
        send_sems, recv_sems = refs[2 * n:]
        x, y, c = lax.axis_index("x"), lax.axis_index("y"), lax.axis_index("c")
        copies = []
        for t in range(n):
            for k in range(N_CHIPS):
                cp = pltpu.make_async_remote_copy(
                    src_ref=src_refs[t].at[k, 1 - c], dst_ref=out_refs[t].at[k],
                    send_sem=send_sems.at[N_CHIPS * t + k], recv_sem=recv_sems.at[N_CHIPS * t + k],
                    device_id=(x, y, 1 - c), device_id_type=MESH)
                cp.start()
                copies.append(cp)
        for cp in copies:
            cp.wait()

    sem = pltpu.SemaphoreType.DMA((N_CHIPS * n,))
    return pl.pallas_call(
        body, name=name,
        in_specs=[ANY] * n, out_specs=[ANY] * n,
        out_shape=[jax.ShapeDtypeStruct((N_CHIPS,) + g.shape[2:], g.dtype) for g in grads],
        scratch_shapes=[sem, sem],
    )(*grads)


def _sibling_swap(srcs, *, name):
    n = len(srcs)

    def body(*refs):
        src_refs, out_refs = refs[:n], refs[n:2 * n]
        send_sems, recv_sems = refs[2 * n:]
        x, y, c = lax.axis_index("x"), lax.axis_index("y"), lax.axis_index("c")
        copies = []
        for t in range(n):
            cp = pltpu.make_async_remote_copy(
                src_ref=src_refs[t], dst_ref=out_refs[t],
                send_sem=send_sems.at[t], recv_sem=recv_sems.at[t],
                device_id=(x, y, 1 - c), device_id_type=MESH)
            cp.start()
            copies.append(cp)
        for cp in copies:
            cp.wait()

    sem = pltpu.SemaphoreType.DMA((n,))
    return pl.pallas_call(
        body, name=name,
        in_specs=[ANY] * n, out_specs=[ANY] * n,
        out_shape=[jax.ShapeDtypeStruct(s.shape, s.dtype) for s in srcs],
        scratch_shapes=[sem, sem],
    )(*srcs)


def _allreduce_small(v, *, name):
    R, C = v.shape

    def body(v_ref, o_ref, buf_ref, send_sems, recv_sems):
        x, y, c = lax.axis_index("x"), lax.axis_index("y"), lax.axis_index("c")
        me = 4 * x + 2 * y + c
        buf_ref[me] = v_ref[...]
        copies = []
        for d in range(1, N_DEV):
            tx = 1 - x if d & 4 else x
            ty = 1 - y if d & 2 else y
            tc = 1 - c if d & 1 else c
            cp = pltpu.make_async_remote_copy(
                src_ref=v_ref, dst_ref=buf_ref.at[me],
                send_sem=send_sems.at[d - 1], recv_sem=recv_sems.at[d - 1],
                device_id=(tx, ty, tc), device_id_type=MESH)
            cp.start()
            copies.append(cp)
        for cp in copies:
            cp.wait()
        acc = buf_ref[0]
        for k in range(1, N_DEV):
            acc = acc + buf_ref[k]
        o_ref[...] = acc

    return pl.pallas_call(
        body, name=name,
        in_specs=[pl.BlockSpec(memory_space=pltpu.VMEM)],
        out_specs=pl.BlockSpec(memory_space=pltpu.VMEM),
        out_shape=jax.ShapeDtypeStruct((R, C), F32),
        scratch_shapes=[pltpu.VMEM((N_DEV, R, C), F32), pltpu.SemaphoreType.DMA((N_DEV - 1,)),
                        pltpu.SemaphoreType.DMA((N_DEV - 1,))],
    )(v)


_A0, _B0, _GC0 = 0, CONV_W, 2 * CONV_W
_Q0 = 3 * CONV_W
_FL0 = _Q0 + 3 * FOX_W
_FG0 = _FL0 + N_FOX
_MQ0 = _FG0 + FOX_W
_MG0 = _MQ0 + MEM_W
_DIN = _MG0 + MEM_W
_WMAIN = _DIN - N_FOX
_PC_N = 3 * CONV_W
_QKV_N = 3 * FOX_W
_PG_N = FOX_W + 2 * MEM_W


def _rows_pad8(a):
    r = a.shape[0]
    p = (-r) % 8
    return jnp.pad(a, ((0, p), (0, 0))) if p else a


def _step(x, mem, target, norm_g, mem_norm_g, final_g, b_f, conv_b, ln_g, ln_b,
          s_in, s_pw, s_mkv, s_out, s_cw, ci, me):
    S, D = x.shape
    M = mem.shape[0]
    tq = min(512, S)
    tf = min(1024, S)
    tt = min(256, S)
    tm = min(256, S)
    nin = s_in.shape[2]
    bf_pad = jnp.pad(b_f, ((0, 0), (0, LANE - N_FOX)))
    whole = lambda g, s: lax.dynamic_update_slice(g, s[None], (me, 0, 0, 0))

    h, h_t, (g_in,) = _rms_fwd(x, norm_g, tm=tm, name="rms_fwd_gather_w_in", gather=[s_in])
    w_in = jnp.transpose(whole(g_in, s_in).reshape(N_CHIPS, D, nin), (1, 0, 2)).reshape(D, N_CHIPS * nin)
    w_main = jnp.concatenate([w_in[:, :_FL0], w_in[:, _FG0:]], axis=1)
    w_fl = jnp.pad(w_in[:, _FL0:_FG0], ((0, 0), (0, LANE - N_FOX)))
    w_main_t, w_fl_t = w_main.T, w_fl.T

    pc, (g_pw, g_mkv, g_out, g_cw) = _mm(h, w_main, tm=1024, tn=512, tk=2048, n=_PC_N, b_off=0,
                                         name="proj_conv_gather_rest", gather=[s_pw, s_mkv, s_out, s_cw])
    w_pw = whole(g_pw, s_pw).reshape(CONV_W, CONV_W)
    w_mkv = whole(g_mkv, s_mkv).reshape(D, 2 * MEM_W)
    w_out = whole(g_out, s_out).reshape(MIX_W, D)
    w_out_t = w_out.T
    cw_pad = jnp.transpose(whole(g_cw, s_cw).reshape(N_CHIPS, CONV_HALO, CONV_W // N_CHIPS),
                           (1, 0, 2)).reshape(CONV_HALO, CONV_W)

    qkv = _mm(h, w_main, out_dtype=BF16, tm=2048, tn=512, tk=2048, n=_QKV_N, b_off=_PC_N // 512, name="proj_qkv")
    pg = _mm(h, w_main, tm=2048, tn=512, tk=2048, n=_PG_N, b_off=(_PC_N + _QKV_N) // 512, name="proj_gate")
    pfl = _mm(h, w_fl, tm=1024, tn=LANE, tk=2048, name="proj_logit")

    y, y_t, u1s = _conv_fwd(pc, cw_pad, conv_b, ln_g, ln_b, w_pw, tt=tt, name="conv_fwd")

    qa, ka = _fox_cumsum(pfl, bf_pad, tc=tt, name="fox_cumsum")
    y, y_t, o_fox, fox_m, fox_linv = _fox_fwd(qkv, pg, qa, ka, y, y_t, tq=tf, nsub=2, name="fox_fwd")

    hm = _rms_fwd(mem, mem_norm_g, tm=min(256, M), name="rms_mem")
    mkv = _mm(hm, w_mkv, tm=256, tn=512, tk=2048, name="mem_kv")
    y, y_t = _mem_fwd(pg, mkv, y, y_t, tq=tf, name="mem_fwd")

    x2 = _mm(y, w_out, add=x, tm=1024, tn=1024, tk=2048, name="out_proj")
    dx2, dx2_b, dfg, sq = _final(x2, target, final_g.reshape(1, D), tm=min(512, S), name="final")

    dy = _mm(dx2_b, w_out_t, tm=1024, tn=1024, tk=2048, name="d_y")
    dw_out = _mm(y_t, dx2_b, tm=1024, tn=1024, tk=2048, name="d_w_out")

    dpc, dw_pw, dcw, dsm = _conv_bwd(pc, u1s, dy, cw_pad, ln_g, ln_b, w_pw, tt=tt, name="conv_bwd")

    do, dfgate, delta = _fox_bwd_prep(dy, o_fox, pg, tq=tf, name="fox_bwd_prep")
    dq, dk, dv, dc = _fox_bwd(qkv, do, qa, ka, fox_m, fox_linv, delta, tq=tf, nsub=2, name="fox_bwd")
    dc_pad = jnp.pad(dc.reshape(N_FOX, S).T, ((0, 0), (0, LANE - N_FOX)))
    dfl, dbf = _fox_dlogit(dc_pad, pfl, bf_pad, tc=tt, name="fox_dlogit")

    dmq, dmgate, dmkv = _mem_bwd(pg, mkv, dy, tq=tf, name="mem_bwd")
    dw_mkv = _mm(hm, dmkv, ta=True, tm=512, tn=512, tk=256, name="d_w_mkv")
    dhm = _mm(dmkv, w_mkv, tb=True, tm=256, tn=512, tk=1024, name="d_hm")
    _, dmg = _rms_bwd(mem, mem_norm_g, dhm, None, tm=min(256, M), name="rms_mem_bwd")

    dproj = jnp.concatenate([dpc, dq.astype(BF16), dk, dv, dfgate, dmq, dmgate], axis=1)
    dw_main = _mm(h_t, dproj, tm=1024, tn=_WMAIN // 4, tk=2048, name="d_w_main")
    dw_fl = _mm(h_t, dfl, tm=1024, tn=LANE, tk=1024, name="d_w_logit")

    def chip_columns(k):
        lo, hi = k * nin, (k + 1) * nin
        parts = []
        if lo < _FL0:
            parts.append(dw_main[:, lo:min(hi, _FL0)])
        if lo < _FG0 and hi > _FL0:
            parts.append(dw_fl[:, max(lo, _FL0) - _FL0:min(hi, _FG0) - _FL0])
        if hi > _FG0:
            parts.append(dw_main[:, max(lo, _FG0) - N_FOX:hi - N_FOX])
        return parts[0] if len(parts) == 1 else jnp.concatenate(parts, axis=1)

    big = [jnp.stack([chip_columns(k) for k in range(N_CHIPS)]),
           dw_pw.reshape(N_CHIPS, CONV_W // N_CHIPS, CONV_W),
           dw_mkv.reshape(N_CHIPS, D // N_CHIPS, 2 * MEM_W),
           dw_out.reshape(N_CHIPS, MIX_W // N_CHIPS, D)]
    big = [g.reshape(N_CHIPS, 2, g.shape[1] // 2, g.shape[2]) for g in big]
    got = _swap_halves(big, name="grad_swap_halves")
    chip = [_add_sibling(g, o, ci, name=f"grad_add_sibling_{t}") for t, (g, o) in enumerate(zip(big, got))]

    dh, parts = _mm_scatter(dproj, w_main_t, dfl, w_fl_t, chip, tm=1024, tn=1024, tk=_WMAIN // 2,
                            name="d_h_grad_scatter")
    grad_x, dng = _rms_bwd(x, norm_g, dh, dx2, tm=min(512, S), name="rms_bwd")

    red = [_sum4(p, own, me, name=f"grad_sum_chips_{t}") for t, (p, own) in enumerate(zip(parts, chip))]
    other = _sibling_swap(red, name="grad_swap_result")
    full = [jnp.where(ci == 0, jnp.concatenate([r, o], axis=0), jnp.concatenate([o, r], axis=0))
            for r, o in zip(red, other)]

    small = dict(norm_g=dng, mem_norm_g=dmg, final_g=dfg, b_f=dbf[0:1, :], conv_w=dcw,
                 conv_b=dsm[0:1], conv_ln_g=dsm[1:2], conv_ln_b=dsm[2:3])
    return sq[0, 0], grad_x, full, small


_SMALL_ORDER = ("norm_g", "mem_norm_g", "final_g", "b_f", "conv_w", "conv_b", "conv_ln_g", "conv_ln_b")


def _pack_small(small):
    parts, layout = [], []
    row = 0
    for k in _SMALL_ORDER:
        p = _rows_pad8(small[k].reshape(-1, LANE))
        layout.append((k, row, small[k].shape))
        parts.append(p)
        row += p.shape[0]
    return jnp.concatenate(parts, axis=0), layout


def _unpack_small(packed, layout):
    out = {}
    for k, row, shape in layout:
        nrow = (shape[0] * shape[1]) // LANE
        out[k] = packed[row:row + nrow].reshape(shape)
    return out


def kernel(x, mem, norm_g, mem_norm_g, w_in, b_f, conv_w, conv_b, conv_ln_g, conv_ln_b, w_conv_pw, w_mem_kv, w_out, final_g, loss_target, m_norm_g, m_mem_norm_g, m_w_in, m_b_f, m_conv_w, m_conv_b, m_conv_ln_g, m_conv_ln_b, m_w_conv_pw, m_w_mem_kv, m_w_out, m_final_g, v_norm_g, v_mem_norm_g, v_w_in, v_b_f, v_conv_w, v_conv_b, v_conv_ln_g, v_conv_ln_b, v_w_conv_pw, v_w_mem_kv, v_w_out, v_final_g):
    S, D = x.shape[1], x.shape[2]
    xi, yi, ci = lax.axis_index("x"), lax.axis_index("y"), lax.axis_index("c")
    chip = 2 * xi + yi

    halves = lambda a: a.reshape(2, a.shape[0] // 2, a.shape[1])
    cw_shard = jnp.pad(conv_w[0], ((0, CONV_HALO - CONV_K), (0, 0)))
    sq, grad_x, (g_w_in, g_w_pw, g_w_mkv, g_w_out), small = _step(
        x[0], mem[0], loss_target[0], norm_g, mem_norm_g, final_g, b_f, conv_b, conv_ln_g, conv_ln_b,
        halves(w_in[0].astype(BF16)), halves(w_conv_pw[0].astype(BF16)), halves(w_mem_kv[0].astype(BF16)),
        halves(w_out[0].astype(BF16)), halves(cw_shard), ci, chip)

    loss = lax.psum(sq, ("x", "y", "c")) * (0.5 / D)

    packed, layout = _pack_small(small)
    sm = _unpack_small(_allreduce_small(packed, name="small_all_reduce"), layout)
    cshard = CONV_W // N_CHIPS
    g_conv_w = lax.dynamic_slice_in_dim(sm["conv_w"][:CONV_K], chip * cshard, cshard, axis=1)

    grads = dict(
        norm_g=sm["norm_g"], mem_norm_g=sm["mem_norm_g"], w_in=g_w_in[None], b_f=sm["b_f"][:, :N_FOX],
        conv_w=g_conv_w[None], conv_b=sm["conv_b"], conv_ln_g=sm["conv_ln_g"], conv_ln_b=sm["conv_ln_b"],
        w_conv_pw=g_w_pw[None], w_mem_kv=g_w_mkv[None], w_out=g_w_out[None], final_g=sm["final_g"].reshape(D))
    weights = dict(norm_g=norm_g, mem_norm_g=mem_norm_g, w_in=w_in, b_f=b_f, conv_w=conv_w, conv_b=conv_b,
                   conv_ln_g=conv_ln_g, conv_ln_b=conv_ln_b, w_conv_pw=w_conv_pw, w_mem_kv=w_mem_kv, w_out=w_out,
                   final_g=final_g)
    ms = dict(norm_g=m_norm_g, mem_norm_g=m_mem_norm_g, w_in=m_w_in, b_f=m_b_f, conv_w=m_conv_w, conv_b=m_conv_b,
              conv_ln_g=m_conv_ln_g, conv_ln_b=m_conv_ln_b, w_conv_pw=m_w_conv_pw, w_mem_kv=m_w_mem_kv,
              w_out=m_w_out, final_g=m_final_g)
    vs = dict(norm_g=v_norm_g, mem_norm_g=v_mem_norm_g, w_in=v_w_in, b_f=v_b_f, conv_w=v_conv_w, conv_b=v_conv_b,
              conv_ln_g=v_conv_ln_g, conv_ln_b=v_conv_ln_b, w_conv_pw=v_w_conv_pw, w_mem_kv=v_w_mem_kv,
              w_out=v_w_out, final_g=v_final_g)

    names = ("norm_g", "mem_norm_g", "w_in", "b_f", "conv_w", "conv_b", "conv_ln_g", "conv_ln_b", "w_conv_pw",
             "w_mem_kv", "w_out", "final_g")
    deltas, new_m, new_v = {}, {}, {}
    for k in names:
        shape = weights[k].shape
        if k == "w_in":
            two_d = lambda a: a.reshape(shape[-2], shape[-1]).T
            back = lambda a: a.T.reshape(shape)
        else:
            two_d = lambda a: a.reshape(-1, shape[-1])
            back = lambda a: a.reshape(shape)
        d, nm, nv = _adamw(two_d(weights[k]), two_d(grads[k]), two_d(ms[k]), two_d(vs[k]), name=f"adamw_{k}")
        deltas[k], new_m[k], new_v[k] = back(d), back(nm), back(nv)

    return (loss, grad_x[None], *[grads[k] for k in names], *[deltas[k] for k in names],
            *[new_m[k] for k in names], *[new_v[k] for k in names])
```
